```python
import jax, jax.numpy as jnp
from jax import lax
import numpy as np

D_MODEL = 2048
BATCH = 8
SEQ = 2048
DEPTH = 1

HEAD_DIM = 128
N_HEADS_A = 8
N_HEADS_B = 8
D_A = N_HEADS_A * HEAD_DIM
D_B = N_HEADS_B * HEAD_DIM
D_MIX = D_A + D_B
IDX_HEADS = 16
IDX_DIM = 64
DSA_TOPK_MAX = 256
MOBA_BLOCK = 256
MOBA_TOPK_MAX = 3
DSA_Q_CHUNK = 64
MOBA_Q_CHUNK = 16
RMS_EPS = 1e-6
NEG_INF = -1e30

COL_WIDTHS = [D_A, D_A, D_A, D_A,
              IDX_HEADS * IDX_DIM, IDX_DIM, IDX_HEADS,
              D_B, D_B, D_B, D_B]
D_IN = sum(COL_WIDTHS)
COL_SPLITS = [int(s) for s in np.cumsum(COL_WIDTHS)[:-1]]

kernel_name = "hybrid_dsa_moba_adaln_block"


def _rms_norm(x, g):
    xf = x.astype(jnp.float32)
    y = xf * lax.rsqrt(jnp.mean(xf * xf, axis=-1, keepdims=True) + RMS_EPS)
    return (y * g.astype(jnp.float32)).astype(x.dtype)


def _alibi_slopes(n):
    return jnp.asarray(2.0 ** (-8.0 * np.arange(1, n + 1) / n), dtype=jnp.float32)


def _dsa_attention(q, k, v, q_idx, k_idx, w_idx, slopes):
    B, L, H, Dh = q.shape
    top_k = min(DSA_TOPK_MAX, L // 4)
    n_chunks = L // DSA_Q_CHUNK
    key_pos = jnp.arange(L)
    gather = jax.vmap(lambda t, i: t[i])

    def to_chunks(t):
        return jnp.swapaxes(t.reshape((B, n_chunks, DSA_Q_CHUNK) + t.shape[2:]), 0, 1)

    def chunk_fn(args):
        start, qc, qic, wc = args
        q_pos = start + jnp.arange(DSA_Q_CHUNK)
        logits = jnp.einsum('bqhd,bsd->bqhs', qic, k_idx).astype(jnp.float32) * IDX_DIM ** -0.5
        score = jnp.einsum('bqh,bqhs->bqs', wc.astype(jnp.float32), jax.nn.relu(logits)) * IDX_HEADS ** -0.5
        causal = key_pos[None, :] <= q_pos[:, None]
        score = jnp.where(causal[None], score, NEG_INF)
        _, sel = lax.top_k(score, top_k)
        valid = sel <= q_pos[None, :, None]
        k_sel = gather(k, sel)
        v_sel = gather(v, sel)
        s = jnp.einsum('bqhd,bqkhd->bqhk', qc, k_sel).astype(jnp.float32) * Dh ** -0.5
        dist = (q_pos[None, :, None] - sel).astype(jnp.float32)
        s = s - slopes[None, None, :, None] * dist[:, :, None, :]
        s = jnp.where(valid[:, :, None, :], s, NEG_INF)
        p = jax.nn.softmax(s, axis=-1).astype(v.dtype)
        return jnp.einsum('bqhk,bqkhd->bqhd', p, v_sel)

    starts = jnp.arange(n_chunks) * DSA_Q_CHUNK
    out = lax.map(chunk_fn, (starts, to_chunks(q), to_chunks(q_idx), to_chunks(w_idx)))
    return jnp.swapaxes(out, 0, 1).reshape(B, L, H, Dh)


def _moba_attention(q, k, v, slopes):
    B, L, H, Dh = q.shape
    nb = -(-L // MOBA_BLOCK)
    Lp = nb * MOBA_BLOCK
    pad = ((0, 0), (0, Lp - L), (0, 0), (0, 0))
    kp = jnp.pad(k, pad)
    vp = jnp.pad(v, pad)
    qp = jnp.pad(q, pad)
    scale = Dh ** -0.5
    offs = jnp.arange(MOBA_BLOCK)
    qb = qp.reshape(B, nb, MOBA_BLOCK, H, Dh)
    kb = kp.reshape(B, nb, MOBA_BLOCK, H, Dh)
    vb = vp.reshape(B, nb, MOBA_BLOCK, H, Dh)

    s_self = jnp.einsum('bnqhd,bnshd->bnhqs', qb, kb).astype(jnp.float32) * scale
    rel = offs[:, None] - offs[None, :]
    s_self = s_self - slopes[:, None, None] * rel.astype(jnp.float32)
    s_self = jnp.where(rel >= 0, s_self, NEG_INF)
    lse_self = jax.nn.logsumexp(s_self, axis=-1)
    p_self = jnp.exp(s_self - lse_self[..., None]).astype(v.dtype)
    o_self = jnp.einsum('bnhqs,bnshd->bnqhd', p_self, vb).reshape(B, Lp, H, Dh)[:, :L]
    lse_self = jnp.swapaxes(lse_self, 2, 3).reshape(B, Lp, H)[:, :L]

    top_k = min(MOBA_TOPK_MAX, nb - 1)
    if top_k == 0:
        return o_self.astype(q.dtype)

    k_mean = jnp.mean(kb.astype(jnp.float32), axis=2)
    gate = jnp.einsum('bthd,bnhd->bthn', q.astype(jnp.float32), k_mean)
    q_blk = jnp.arange(L) // MOBA_BLOCK
    past = jnp.arange(nb)[None, :] < q_blk[:, None]
    gate = jnp.where(past[None, :, None, :], gate, NEG_INF)
    _, sel = lax.top_k(gate, top_k)

    k_bh = jnp.transpose(kb, (0, 3, 1, 2, 4))
    v_bh = jnp.transpose(vb, (0, 3, 1, 2, 4))
    bi = jnp.arange(B)[:, None, None, None]
    hi = jnp.arange(H)[None, None, :, None]
    n_chunks = L // MOBA_Q_CHUNK

    def to_chunks(t):
        return jnp.swapaxes(t.reshape((B, n_chunks, MOBA_Q_CHUNK) + t.shape[2:]), 0, 1)

    def chunk_fn(args):
        start, qc, sc = args
        q_pos = start + jnp.arange(MOBA_Q_CHUNK)
        valid = sc < (q_pos // MOBA_BLOCK)[None, :, None, None]
        k_g = k_bh[bi, hi, sc]
        v_g = v_bh[bi, hi, sc]
        s = jnp.einsum('bqhd,bqhjsd->bqhjs', qc, k_g).astype(jnp.float32) * scale
        key_pos = sc[..., None] * MOBA_BLOCK + offs
        dist = (q_pos[None, :, None, None, None] - key_pos).astype(jnp.float32)
        s = s - slopes[None, None, :, None, None] * dist
        s = jnp.where(valid[..., None], s, NEG_INF)
        s = s.reshape(B, MOBA_Q_CHUNK, H, top_k * MOBA_BLOCK)
        lse = jax.nn.logsumexp(s, axis=-1)
        p = jnp.exp(s - lse[..., None]).astype(v.dtype).reshape(B, MOBA_Q_CHUNK, H, top_k, MOBA_BLOCK)
        o = jnp.einsum('bqhjs,bqhjsd->bqhd', p, v_g)
        return o, lse

    starts = jnp.arange(n_chunks) * MOBA_Q_CHUNK
    o_hist, lse_hist = lax.map(chunk_fn, (starts, to_chunks(q), to_chunks(sel)))
    o_hist = jnp.swapaxes(o_hist, 0, 1).reshape(B, L, H, Dh)
    lse_hist = jnp.swapaxes(lse_hist, 0, 1).reshape(B, L, H)

    lse = jnp.logaddexp(lse_self, lse_hist)
    w_self = jnp.exp(lse_self - lse)[..., None]
    w_hist = jnp.exp(lse_hist - lse)[..., None]
    return (o_self * w_self + o_hist * w_hist).astype(q.dtype)


def _hybrid_layer(x, c, w_ada, b_ada, g_norm, w_in, q_norm_a, k_norm_a, k_norm_idx,
                  q_norm_b, k_norm_b, w_out):
    B, L, _ = x.shape
    mod = jax.nn.silu(c) @ w_ada + b_ada
    shift, scale, gate = jnp.split(mod, 3, axis=-1)
    h = _rms_norm(x, g_norm) * (1.0 + scale[:, None, :]) + shift[:, None, :]

    proj = h @ w_in
    qa, ka, va, za, q_idx, k_idx, w_idx, qb, kb, vb, zb = jnp.split(proj, COL_SPLITS, axis=-1)

    heads_a = (B, L, N_HEADS_A, HEAD_DIM)
    heads_b = (B, L, N_HEADS_B, HEAD_DIM)
    qa = _rms_norm(qa.reshape(heads_a), q_norm_a)
    ka = _rms_norm(ka.reshape(heads_a), k_norm_a)
    va = va.reshape(heads_a)
    q_idx = q_idx.reshape(B, L, IDX_HEADS, IDX_DIM)
    k_idx = _rms_norm(k_idx, k_norm_idx)
    qb = _rms_norm(qb.reshape(heads_b), q_norm_b)
    kb = _rms_norm(kb.reshape(heads_b), k_norm_b)
    vb = vb.reshape(heads_b)

    ya = _dsa_attention(qa, ka, va, q_idx, k_idx, w_idx, _alibi_slopes(N_HEADS_A))
    yb = _moba_attention(qb, kb, vb, _alibi_slopes(N_HEADS_B))

    ya = ya.reshape(B, L, D_A) * jax.nn.silu(za)
    yb = yb.reshape(B, L, D_B) * jax.nn.silu(zb)
    y = jnp.concatenate([ya, yb], axis=-1) @ w_out
    return x + gate[:, None, :] * y


def setup_inputs(seed: int = 0) -> dict:
    key = jax.random.key(seed)
    ks = jax.random.split(key, 12)
    nrm = jax.random.normal
    x = nrm(ks[0], (BATCH, SEQ, D_MODEL), jnp.float32)
    c = nrm(ks[1], (BATCH, D_MODEL), jnp.float32)
    w_ada = nrm(ks[2], (DEPTH, D_MODEL, 3 * D_MODEL), jnp.float32) * (0.5 * D_MODEL ** -0.5)
    b_ada = 0.01 * nrm(ks[3], (DEPTH, 3 * D_MODEL), jnp.float32)
    g_norm = 1.0 + 0.02 * nrm(ks[4], (DEPTH, D_MODEL), jnp.float32)
    w_in = nrm(ks[5], (DEPTH, D_MODEL, D_IN), jnp.float32) * D_MODEL ** -0.5
    q_norm_a = 1.0 + 0.02 * nrm(ks[6], (DEPTH, HEAD_DIM), jnp.float32)
    k_norm_a = 1.0 + 0.02 * nrm(ks[7], (DEPTH, HEAD_DIM), jnp.float32)
    k_norm_idx = 1.0 + 0.02 * nrm(ks[8], (DEPTH, IDX_DIM), jnp.float32)
    q_norm_b = 1.0 + 0.02 * nrm(ks[9], (DEPTH, HEAD_DIM), jnp.float32)
    k_norm_b = 1.0 + 0.02 * nrm(ks[10], (DEPTH, HEAD_DIM), jnp.float32)
    w_out = nrm(ks[11], (DEPTH, D_MIX, D_MODEL), jnp.float32) * D_MIX ** -0.5
    return {"x": x, "c": c, "w_ada": w_ada, "b_ada": b_ada, "g_norm": g_norm, "w_in": w_in,
            "q_norm_a": q_norm_a, "k_norm_a": k_norm_a, "k_norm_idx": k_norm_idx,
            "q_norm_b": q_norm_b, "k_norm_b": k_norm_b, "w_out": w_out}


def reference(x, c, w_ada, b_ada, g_norm, w_in, q_norm_a, k_norm_a, k_norm_idx,
              q_norm_b, k_norm_b, w_out):
    for i in range(DEPTH):
        x = _hybrid_layer(x, c, w_ada[i], b_ada[i], g_norm[i], w_in[i], q_norm_a[i], k_norm_a[i],
                          k_norm_idx[i], q_norm_b[i], k_norm_b[i], w_out[i])
    return x
```

```python
import functools

import jax
import jax.numpy as jnp
import numpy as np
from jax import lax
from jax.experimental import pallas as pl
from jax.experimental.pallas import tpu as pltpu

F32 = jnp.float32
BF16 = jnp.bfloat16

HEAD_DIM = 128
N_HEADS = 8
D_GROUP = N_HEADS * HEAD_DIM
IDX_HEADS = 16
IDX_DIM = 64
DSA_TOPK_MAX = 256
MOBA_BLOCK = 256
MOBA_TOPK_MAX = 3
RMS_EPS = 1e-6
NEG_INF = -1e30
BIG = 1e30

LANES = 128
VMEM_LIMIT_BYTES = 56 * 1024 * 1024

SEG_QA, SEG_KA, SEG_VA, SEG_ZA, SEG_QIDX, SEG_QB, SEG_KB, SEG_VB, SEG_ZB = range(9)
N_SEGS = 9
BLOCKS_PER_SEG = D_GROUP // LANES


def _alibi_slope(h, n):
    return float(2.0 ** (-8.0 * (h + 1) / n))


def _dot_nt(a, b):
    return lax.dot_general(a, b, (((1,), (1,)), ((), ())), preferred_element_type=F32)


def _mod_kernel(c_ref, w_ref, b_ref, o_ref):
    c = c_ref[...]
    s = c * jax.nn.sigmoid(c)
    o_ref[...] = jnp.dot(s, w_ref[...], preferred_element_type=F32) + b_ref[...]


def _modulation(c, w_ada, b_ada):
    bsz, d = c.shape
    n = w_ada.shape[1]
    tn = 1024
    return pl.pallas_call(
        _mod_kernel,
        grid=(n // tn,),
        in_specs=[
            pl.BlockSpec((bsz, d), lambda j: (0, 0)),
            pl.BlockSpec((d, tn), lambda j: (0, j)),
            pl.BlockSpec((1, tn), lambda j: (0, j)),
        ],
        out_specs=pl.BlockSpec((bsz, tn), lambda j: (0, j)),
        out_shape=jax.ShapeDtypeStruct((bsz, n), F32),
        compiler_params=pltpu.CompilerParams(
            dimension_semantics=("arbitrary",), vmem_limit_bytes=VMEM_LIMIT_BYTES),
        name="adaln_mod",
    )(c, w_ada, b_ada.reshape(1, n))


def _proj_kernel(x_ref, mod_ref, g_ref, w_ref, wt_ref, gain_ref, gk_ref,
                 p_ref, kk_ref, wi_ref, h_ref):
    j = pl.program_id(1)

    @pl.when(j == 0)
    def _():
        x = x_ref[...]
        ms = jnp.mean(x * x, axis=-1, keepdims=True)
        y = x * lax.rsqrt(ms + RMS_EPS) * g_ref[...]
        shift = mod_ref[0, 0:1, :]
        scale = mod_ref[0, 1:2, :]
        hb = (y * (1.0 + scale) + shift).astype(BF16)
        h_ref[...] = hb
        t = jnp.dot(hb, wt_ref[...], preferred_element_type=F32)
        tk = t[:, :LANES]
        kms = jnp.mean(tk * tk, axis=-1, keepdims=True)
        kk_ref[...] = (tk * lax.rsqrt(kms + RMS_EPS) * gk_ref[...]).astype(BF16)
        wi_ref[...] = t[:, LANES:]

    acc = jnp.dot(h_ref[...], w_ref[...], preferred_element_type=F32)
    is_norm = (j == SEG_QA) | (j == SEG_KA) | (j == SEG_QB) | (j == SEG_KB)

    @pl.when(is_norm)
    def _():
        g = gain_ref[0]
        for c in range(BLOCKS_PER_SEG):
            a = acc[:, c * LANES:(c + 1) * LANES]
            ms = jnp.mean(a * a, axis=-1, keepdims=True)
            p_ref[c] = (a * lax.rsqrt(ms + RMS_EPS) * g).astype(BF16)

    @pl.when(jnp.logical_not(is_norm))
    def _():
        for c in range(BLOCKS_PER_SEG):
            p_ref[c] = acc[:, c * LANES:(c + 1) * LANES].astype(BF16)


def _projection(x2, mod3, g_norm, w_main, w_tail, gains, gk, seq_len):
    m, d = x2.shape
    tm = 1024
    tiles_per_seq = seq_len // tm
    return pl.pallas_call(
        _proj_kernel,
        grid=(m // tm, N_SEGS),
        in_specs=[
            pl.BlockSpec((tm, d), lambda i, j: (i, 0)),
            pl.BlockSpec((1, 3, d), lambda i, j: (i // tiles_per_seq, 0, 0)),
            pl.BlockSpec((1, d), lambda i, j: (0, 0)),
            pl.BlockSpec((d, D_GROUP), lambda i, j: (0, j)),
            pl.BlockSpec((d, 2 * LANES), lambda i, j: (0, 0)),
            pl.BlockSpec((1, 1, LANES), lambda i, j: (j, 0, 0)),
            pl.BlockSpec((1, LANES), lambda i, j: (0, 0)),
        ],
        out_specs=[
            pl.BlockSpec((BLOCKS_PER_SEG, tm, LANES), lambda i, j: (j, i, 0)),
            pl.BlockSpec((tm, LANES), lambda i, j: (i, 0)),
            pl.BlockSpec((tm, LANES), lambda i, j: (i, 0)),
        ],
        out_shape=[
            jax.ShapeDtypeStruct((N_SEGS * BLOCKS_PER_SEG, m, LANES), BF16),
            jax.ShapeDtypeStruct((m, LANES), BF16),
            jax.ShapeDtypeStruct((m, LANES), F32),
        ],
        scratch_shapes=[pltpu.VMEM((tm, d), BF16)],
        compiler_params=pltpu.CompilerParams(
            dimension_semantics=("arbitrary", "arbitrary"), vmem_limit_bytes=VMEM_LIMIT_BYTES),
        name="in_proj",
    )(x2, mod3, g_norm, w_main, w_tail, gains, gk)


DSA_TQ = 256
DSA_TK = 256
SEARCH_STEPS = 4


def _dsa_kernel(q_ref, k_ref, v_ref, z_ref, qi_ref, kk_ref, wi_ref, o_ref,
                s_ref, qs_ref, wb_ref, t_ref, j_ref, *, top_k, seq_len):
    tq, tk = DSA_TQ, DSA_TK
    i = pl.program_id(1)
    t0 = i * tq
    nk = i + 1
    scale = HEAD_DIM ** -0.5
    idx_scale = (IDX_DIM ** -0.5) * (IDX_HEADS ** -0.5)

    lane = lax.broadcasted_iota(jnp.int32, (tq, LANES), 1)
    row_f = lax.broadcasted_iota(jnp.int32, (tq, tk), 0).astype(F32)
    col_f = lax.broadcasted_iota(jnp.int32, (tq, tk), 1).astype(F32)
    ones_bf = jnp.ones((LANES, LANES), BF16)

    def rep2(x):
        return jnp.concatenate([x] * (tk // LANES), axis=1)

    def lane_total(x):
        return jnp.dot(x.astype(BF16), ones_bf, preferred_element_type=F32)

    def fold(x):
        out = x[:, :LANES]
        for c in range(1, tk // LANES):
            out = out + x[:, c * LANES:(c + 1) * LANES]
        return out

    for c in range(BLOCKS_PER_SEG):
        qp = qi_ref[c].astype(F32)
        qs_ref[2 * c] = jnp.where(lane < IDX_DIM, qp, 0.0).astype(BF16)
        qs_ref[2 * c + 1] = jnp.where(lane >= IDX_DIM, qp, 0.0).astype(BF16)
    w = wi_ref[...] * idx_scale
    for h in range(IDX_HEADS):
        wb_ref[h] = jnp.broadcast_to(w[:, h:h + 1], (tq, LANES))

    def score_tile(kt, carry):
        kk = kk_ref[pl.ds(pl.multiple_of(kt * tk, tk), tk), :]
        acc = jnp.zeros((tq, tk), F32)
        for h in range(IDX_HEADS):
            lg = _dot_nt(qs_ref[h], kk)
            acc = acc + rep2(wb_ref[h]) * jnp.maximum(lg, 0.0)
        off = (kt * tk - t0).astype(F32)
        causal = (col_f + off) <= row_f
        s_ref[kt] = jnp.where(causal, acc, -jnp.inf)
        return carry

    lax.fori_loop(0, nk, score_tile, 0)

    t_ref[...] = jnp.full((tq, LANES), NEG_INF, F32)
    j_ref[...] = jnp.full((tq, LANES), -1.0, F32)

    def count(pred):
        def body(kt, acc):
            off = (kt * tk).astype(F32)
            return acc + fold(jnp.where(pred(s_ref[kt], col_f + off), 1.0, 0.0))
        return lane_total(lax.fori_loop(0, nk, body, jnp.zeros((tq, LANES), F32)))

    @pl.when(t0 + tq > top_k)
    def _():
        kf = float(top_k)
        row_pos = lax.broadcasted_iota(jnp.int32, (tq, LANES), 0) + t0
        done0 = jnp.where(row_pos + 1 <= top_k, 1.0, 0.0)

        def minmax(kt, carry):
            mn, mx = carry
            s = s_ref[kt]
            mx = jnp.maximum(mx, jnp.max(s, axis=-1, keepdims=True))
            mn = jnp.minimum(mn, jnp.min(jnp.where(s > 0.5 * NEG_INF, s, BIG), axis=-1, keepdims=True))
            return mn, mx

        mn, mx = lax.fori_loop(0, nk, minmax,
                               (jnp.full((tq, 1), BIG, F32), jnp.full((tq, 1), NEG_INF, F32)))
        lo0 = jnp.broadcast_to(mn, (tq, LANES))
        hi0 = jnp.broadcast_to(mx, (tq, LANES))

        def step(x, degen, state):
            lo, hi, thr, done, tie = state
            c = count(lambda s, col: s > rep2(x))
            active = done == 0.0
            live = jnp.logical_and(active, jnp.logical_not(degen))
            found = jnp.logical_and(live, c == kf)
            new_tie = jnp.logical_and(active, degen)
            thr = jnp.where(found, x, jnp.where(new_tie, hi, thr))
            tie = jnp.where(new_tie, 1.0, tie)
            done = jnp.where(jnp.logical_or(found, new_tie), 1.0, done)
            lo = jnp.where(jnp.logical_and(live, c > kf), x, lo)
            hi = jnp.where(jnp.logical_and(live, c < kf), x, hi)
            return lo, hi, thr, done, tie

        state0 = (lo0, hi0, jnp.full((tq, LANES), NEG_INF, F32), done0, jnp.zeros((tq, LANES), F32))
        state0 = step(lo0, jnp.zeros((tq, LANES), jnp.bool_), state0)

        def cond(carry):
            return carry[1] > 0.0

        def body(carry):
            state, _ = carry
            for _ in range(SEARCH_STEPS):
                lo, hi = state[0], state[1]
                mid = 0.5 * lo + 0.5 * hi
                degen = jnp.logical_or(mid <= lo, mid >= hi)
                state = step(mid, degen, state)
            return state, jnp.max(1.0 - state[3])

        (lo, hi, thr, done, tie), _ = lax.while_loop(cond, body, (state0, jnp.max(1.0 - state0[3])))
        t_ref[...] = thr

        @pl.when(jnp.max(tie) > 0.0)
        def _():
            need = kf - count(lambda s, col: s > rep2(thr))
            n_bits = int(np.ceil(np.log2(seq_len + 1)))

            def jstep(_, carry):
                jlo, jhi = carry
                jmid = jnp.floor((jlo + jhi) * 0.5)
                c = count(lambda s, col: jnp.logical_and(s == rep2(thr), col <= rep2(jmid)))
                ge = c >= need
                return jnp.where(ge, jlo, jmid), jnp.where(ge, jmid, jhi)

            _, jhi = lax.fori_loop(0, n_bits, jstep,
                                   (jnp.full((tq, LANES), -1.0, F32),
                                    jnp.full((tq, LANES), float(seq_len - 1), F32)))
            j_ref[...] = jnp.where(tie > 0.0, jhi, -1.0)

    def mask_tile(kt, carry):
        s = s_ref[kt]
        thr = rep2(t_ref[...])
        off = (kt * tk).astype(F32)
        sel = jnp.logical_or(s > thr, jnp.logical_and(s == thr, (col_f + off) <= rep2(j_ref[...])))
        s_ref[kt] = jnp.where(sel, 0.0, NEG_INF)
        return carry

    lax.fori_loop(0, nk, mask_tile, 0)

    key_rel = lax.broadcasted_iota(jnp.int32, (1, tk), 1).astype(F32)
    for h in range(N_HEADS):
        q = q_ref[h]
        slope = _alibi_slope(h, N_HEADS)

        def att_tile(kt, carry, q=q, slope=slope, h=h):
            m, l, acc = carry
            ks = pl.ds(pl.multiple_of(kt * tk, tk), tk)
            s = _dot_nt(q, k_ref[h, ks, :]) * scale
            off = (kt * tk - t0).astype(F32)
            s = s + slope * (key_rel + off) + s_ref[kt]
            m_new = jnp.maximum(m, jnp.max(s, axis=-1, keepdims=True))
            alpha = jnp.exp(m - m_new)
            p = jnp.exp(s - m_new)
            l = alpha * l + jnp.sum(p, axis=-1, keepdims=True)
            acc = alpha * acc + jnp.dot(p.astype(BF16), v_ref[h, ks, :], preferred_element_type=F32)
            return m_new, l, acc

        m, l, acc = lax.fori_loop(
            0, nk, att_tile,
            (jnp.full((tq, 1), NEG_INF, F32), jnp.zeros((tq, 1), F32), jnp.zeros((tq, HEAD_DIM), F32)))
        z = z_ref[h].astype(F32)
        o = (acc / l) * (z * jax.nn.sigmoid(z))
        o_ref[:, h * HEAD_DIM:(h + 1) * HEAD_DIM] = o.astype(BF16)


def _dsa_attention(p, kk, wi, bsz, seq_len):
    tq = DSA_TQ
    nq = seq_len // tq
    m = bsz * seq_len
    top_k = min(DSA_TOPK_MAX, seq_len // 4)
    seg = lambda s: (lambda b, i: (s, b * nq + i, 0))
    seg_full = lambda s: (lambda b, i: (s, b, 0))
    kern = functools.partial(_dsa_kernel, top_k=top_k, seq_len=seq_len)
    return pl.pallas_call(
        kern,
        grid=(bsz, nq),
        in_specs=[
            pl.BlockSpec((BLOCKS_PER_SEG, tq, LANES), seg(SEG_QA)),
            pl.BlockSpec((BLOCKS_PER_SEG, seq_len, LANES), seg_full(SEG_KA)),
            pl.BlockSpec((BLOCKS_PER_SEG, seq_len, LANES), seg_full(SEG_VA)),
            pl.BlockSpec((BLOCKS_PER_SEG, tq, LANES), seg(SEG_ZA)),
            pl.BlockSpec((BLOCKS_PER_SEG, tq, LANES), seg(SEG_QIDX)),
            pl.BlockSpec((seq_len, LANES), lambda b, i: (b, 0)),
            pl.BlockSpec((tq, LANES), lambda b, i: (b * nq + i, 0)),
        ],
        out_specs=pl.BlockSpec((tq, D_GROUP), lambda b, i: (b * nq + i, 0)),
        out_shape=jax.ShapeDtypeStruct((m, D_GROUP), BF16),
        scratch_shapes=[
            pltpu.VMEM((seq_len // DSA_TK, tq, DSA_TK), F32),
            pltpu.VMEM((IDX_HEADS, tq, LANES), BF16),
            pltpu.VMEM((IDX_HEADS, tq, LANES), F32),
            pltpu.VMEM((tq, LANES), F32),
            pltpu.VMEM((tq, LANES), F32),
        ],
        compiler_params=pltpu.CompilerParams(
            dimension_semantics=("arbitrary", "arbitrary"), vmem_limit_bytes=VMEM_LIMIT_BYTES),
        name="dsa_attn",
    )(p, p, p, p, p, kk, wi)


def _moba_kernel(q_ref, k_ref, v_ref, z_ref, o_ref, kmean_ref, vt_ref, selb_ref, *, n_blocks):
    bs = MOBA_BLOCK
    n = pl.program_id(1)
    scale = HEAD_DIM ** -0.5
    top_k = min(MOBA_TOPK_MAX, n_blocks - 1)
    nb_pad = kmean_ref.shape[1]

    @pl.when(n == 0)
    def _():
        kmean_ref[...] = jnp.zeros(kmean_ref.shape, F32)
        for h in range(N_HEADS):
            for j in range(n_blocks):
                kb = k_ref[h, j * bs:(j + 1) * bs, :].astype(F32)
                kmean_ref[h, j:j + 1, :] = jnp.mean(kb, axis=0, keepdims=True)
                vb = v_ref[h, j * bs:(j + 1) * bs, :].astype(F32)
                vt_ref[h, j] = vb.T.astype(BF16)

    key_loc = lax.broadcasted_iota(jnp.int32, (bs, bs), 0).astype(F32)
    qry_loc = lax.broadcasted_iota(jnp.int32, (bs, bs), 1).astype(F32)
    rel = key_loc - qry_loc
    blk = lax.broadcasted_iota(jnp.int32, (nb_pad, bs), 0)

    for h in range(N_HEADS):
        q = q_ref[h]
        slope = _alibi_slope(h, N_HEADS)

        g = _dot_nt(kmean_ref[h].astype(BF16), q)
        past = blk < n
        for j in range(n_blocks):
            gj = g[j:j + 1, :]
            beats = jnp.logical_or(g > gj, jnp.logical_and(g == gj, blk < j))
            rank = jnp.sum(jnp.where(jnp.logical_and(beats, past), 1.0, 0.0), axis=0, keepdims=True)
            keep = jnp.logical_and(rank < float(top_k), j < n)
            selb_ref[j:j + 1, :] = jnp.where(keep, 0.0, NEG_INF)

        s = _dot_nt(k_ref[h, pl.ds(pl.multiple_of(n * bs, bs), bs), :], q) * scale + slope * rel
        s = jnp.where(rel <= 0.0, s, NEG_INF)
        m = jnp.max(s, axis=0, keepdims=True)
        p = jnp.exp(s - m)
        l = jnp.sum(p, axis=0, keepdims=True)
        acc = jnp.dot(vt_ref[h, n], p.astype(BF16), preferred_element_type=F32)

        def hist(j, carry, q=q, slope=slope, h=h):
            m, l, acc = carry
            ks = pl.ds(pl.multiple_of(j * bs, bs), bs)
            dist = ((j - n) * bs).astype(F32)
            s = _dot_nt(k_ref[h, ks, :], q) * scale + slope * (rel + dist) + selb_ref[pl.ds(j, 1), :]
            m_new = jnp.maximum(m, jnp.max(s, axis=0, keepdims=True))
            alpha = jnp.exp(m - m_new)
            p = jnp.exp(s - m_new)
            l = alpha * l + jnp.sum(p, axis=0, keepdims=True)
            acc = alpha * acc + jnp.dot(vt_ref[h, j], p.astype(BF16), preferred_element_type=F32)
            return m_new, l, acc

        m, l, acc = lax.fori_loop(0, n, hist, (m, l, acc))
        z = z_ref[h].astype(F32)
        o = (acc / l).T * (z * jax.nn.sigmoid(z))
        o_ref[:, h * HEAD_DIM:(h + 1) * HEAD_DIM] = o.astype(BF16)


def _moba_attention(p, bsz, seq_len):
    bs = MOBA_BLOCK
    nb = seq_len // bs
    m = bsz * seq_len
    seg = lambda s: (lambda b, i: (s, b * nb + i, 0))
    seg_full = lambda s: (lambda b, i: (s, b, 0))
    kern = functools.partial(_moba_kernel, n_blocks=nb)
    nb_pad = max(16, nb)
    return pl.pallas_call(
        kern,
        grid=(bsz, nb),
        in_specs=[
            pl.BlockSpec((BLOCKS_PER_SEG, bs, LANES), seg(SEG_QB)),
            pl.BlockSpec((BLOCKS_PER_SEG, seq_len, LANES), seg_full(SEG_KB)),
            pl.BlockSpec((BLOCKS_PER_SEG, seq_len, LANES), seg_full(SEG_VB)),
            pl.BlockSpec((BLOCKS_PER_SEG, bs, LANES), seg(SEG_ZB)),
        ],
        out_specs=pl.BlockSpec((bs, D_GROUP), lambda b, i: (b * nb + i, 0)),
        out_shape=jax.ShapeDtypeStruct((m, D_GROUP), BF16),
        scratch_shapes=[
            pltpu.VMEM((N_HEADS, nb_pad, HEAD_DIM), F32),
            pltpu.VMEM((N_HEADS, nb, HEAD_DIM, bs), BF16),
            pltpu.VMEM((nb_pad, bs), F32),
        ],
        compiler_params=pltpu.CompilerParams(
            dimension_semantics=("arbitrary", "arbitrary"), vmem_limit_bytes=VMEM_LIMIT_BYTES),
        name="moba_attn",
    )(p, p, p, p)


def _out_kernel(ya_ref, yb_ref, wa_ref, wb_ref, x_ref, mod_ref, o_ref):
    y = jnp.dot(ya_ref[...], wa_ref[...], preferred_element_type=F32)
    y = y + jnp.dot(yb_ref[...], wb_ref[...], preferred_element_type=F32)
    gate = mod_ref[0, 2:3, :]
    o_ref[...] = x_ref[...] + gate * y


def _out_projection(ya, yb, w_out_bf, x2, mod3, seq_len):
    m, d = x2.shape
    tm, tn = 512, 1024
    tiles_per_seq = seq_len // tm
    return pl.pallas_call(
        _out_kernel,
        grid=(m // tm, d // tn),
        in_specs=[
            pl.BlockSpec((tm, D_GROUP), lambda i, j: (i, 0)),
            pl.BlockSpec((tm, D_GROUP), lambda i, j: (i, 0)),
            pl.BlockSpec((D_GROUP, tn), lambda i, j: (0, j)),
            pl.BlockSpec((D_GROUP, tn), lambda i, j: (1, j)),
            pl.BlockSpec((tm, tn), lambda i, j: (i, j)),
            pl.BlockSpec((1, 3, tn), lambda i, j: (i // tiles_per_seq, 0, j)),
        ],
        out_specs=pl.BlockSpec((tm, tn), lambda i, j: (i, j)),
        out_shape=jax.ShapeDtypeStruct((m, d), F32),
        compiler_params=pltpu.CompilerParams(
            dimension_semantics=("arbitrary", "arbitrary"), vmem_limit_bytes=VMEM_LIMIT_BYTES),
        name="out_proj",
    )(ya, yb, w_out_bf, w_out_bf, x2, mod3)


def _layer(x, c, w_ada, b_ada, g_norm, w_in, q_norm_a, k_norm_a, k_norm_idx, q_norm_b, k_norm_b, w_out):
    bsz, seq_len, d = x.shape
    assert seq_len % MOBA_BLOCK == 0 and seq_len % 1024 == 0 and d % 1024 == 0
    x2 = x.reshape(bsz * seq_len, d)

    mod3 = _modulation(c, w_ada, b_ada).reshape(bsz, 3, d)

    o = 0
    cols = {}
    for name, width in (("qa", D_GROUP), ("ka", D_GROUP), ("va", D_GROUP), ("za", D_GROUP),
                        ("qidx", IDX_HEADS * IDX_DIM), ("kidx", IDX_DIM), ("widx", IDX_HEADS),
                        ("qb", D_GROUP), ("kb", D_GROUP), ("vb", D_GROUP), ("zb", D_GROUP)):
        cols[name] = w_in[:, o:o + width]
        o += width
    w_main = jnp.concatenate(
        [cols[k] for k in ("qa", "ka", "va", "za", "qidx", "qb", "kb", "vb", "zb")], axis=1).astype(BF16)
    w_tail = jnp.concatenate(
        [cols["kidx"], cols["kidx"], cols["widx"],
         jnp.zeros((d, LANES - IDX_HEADS), w_in.dtype)], axis=1).astype(BF16)
    ones = jnp.ones((HEAD_DIM,), F32)
    gains = jnp.stack([q_norm_a, k_norm_a, ones, ones, ones, q_norm_b, k_norm_b, ones, ones]
                      ).reshape(N_SEGS, 1, HEAD_DIM)
    gk = jnp.concatenate([k_norm_idx, k_norm_idx]).reshape(1, LANES)

    p, kk, wi = _projection(x2, mod3, g_norm.reshape(1, d), w_main, w_tail, gains, gk, seq_len)
    ya = _dsa_attention(p, kk, wi, bsz, seq_len)
    yb = _moba_attention(p, bsz, seq_len)
    out = _out_projection(ya, yb, w_out.astype(BF16), x2, mod3, seq_len)
    return out.reshape(bsz, seq_len, d)


def kernel(x, c, w_ada, b_ada, g_norm, w_in, q_norm_a, k_norm_a, k_norm_idx, q_norm_b, k_norm_b, w_out):
    for i in range(w_ada.shape[0]):
        x = _layer(x, c, w_ada[i], b_ada[i], g_norm[i], w_in[i], q_norm_a[i], k_norm_a[i],
                   k_norm_idx[i], q_norm_b[i], k_norm_b[i], w_out[i])
    return x
```

```python
import functools

import jax
import jax.numpy as jnp
import numpy as np
from jax import lax
from jax.experimental import pallas as pl
from jax.experimental.pallas import tpu as pltpu

F32 = jnp.float32
BF16 = jnp.bfloat16

HEAD_DIM = 128
N_HEADS = 8
D_GROUP = N_HEADS * HEAD_DIM
IDX_HEADS = 16
IDX_DIM = 64
DSA_TOPK_MAX = 256
MOBA_BLOCK = 256
MOBA_TOPK_MAX = 3
RMS_EPS = 1e-6
NEG_INF = -1e30
BIG = 1e30

LANES = 128
VMEM_LIMIT_BYTES = 56 * 1024 * 1024

SEG_QA, SEG_KA, SEG_VA, SEG_ZA, SEG_QIDX, SEG_QB, SEG_KB, SEG_VB, SEG_ZB = range(9)
N_SEGS = 9
BLOCKS_PER_SEG = D_GROUP // LANES


def _alibi_slope(h, n):
    return float(2.0 ** (-8.0 * (h + 1) / n))


def _dot_nt(a, b):
    return lax.dot_general(a, b, (((1,), (1,)), ((), ())), preferred_element_type=F32)


def _mod_kernel(c_ref, w_ref, b_ref, o_ref):
    c = c_ref[...]
    s = c * jax.nn.sigmoid(c)
    o_ref[...] = jnp.dot(s, w_ref[...], preferred_element_type=F32) + b_ref[...]


def _modulation(c, w_ada, b_ada):
    bsz, d = c.shape
    n = w_ada.shape[1]
    tn = 1024
    return pl.pallas_call(
        _mod_kernel,
        grid=(n // tn,),
        in_specs=[
            pl.BlockSpec((bsz, d), lambda j: (0, 0)),
            pl.BlockSpec((d, tn), lambda j: (0, j)),
            pl.BlockSpec((1, tn), lambda j: (0, j)),
        ],
        out_specs=pl.BlockSpec((bsz, tn), lambda j: (0, j)),
        out_shape=jax.ShapeDtypeStruct((bsz, n), F32),
        compiler_params=pltpu.CompilerParams(
            dimension_semantics=("arbitrary",), vmem_limit_bytes=VMEM_LIMIT_BYTES),
        name="adaln_mod",
    )(c, w_ada, b_ada.reshape(1, n))


def _proj_kernel(x_ref, mod_ref, g_ref, w_ref, wt_ref, gain_ref, gk_ref,
                 p_ref, kk_ref, wi_ref, h_ref):
    j = pl.program_id(1)

    @pl.when(j == 0)
    def _():
        x = x_ref[...]
        ms = jnp.mean(x * x, axis=-1, keepdims=True)
        y = x * lax.rsqrt(ms + RMS_EPS) * g_ref[...]
        shift = mod_ref[0, 0:1, :]
        scale = mod_ref[0, 1:2, :]
        hb = (y * (1.0 + scale) + shift).astype(BF16)
        h_ref[...] = hb
        t = jnp.dot(hb, wt_ref[...], preferred_element_type=F32)
        tk = t[:, :LANES]
        kms = jnp.mean(tk * tk, axis=-1, keepdims=True)
        kk_ref[...] = (tk * lax.rsqrt(kms + RMS_EPS) * gk_ref[...]).astype(BF16)
        wi_ref[...] = t[:, LANES:]

    acc = jnp.dot(h_ref[...], w_ref[...], preferred_element_type=F32)
    is_norm = (j == SEG_QA) | (j == SEG_KA) | (j == SEG_QB) | (j == SEG_KB)

    @pl.when(is_norm)
    def _():
        g = gain_ref[0]
        for c in range(BLOCKS_PER_SEG):
            a = acc[:, c * LANES:(c + 1) * LANES]
            ms = jnp.mean(a * a, axis=-1, keepdims=True)
            p_ref[c] = (a * lax.rsqrt(ms + RMS_EPS) * g).astype(BF16)

    @pl.when(jnp.logical_not(is_norm))
    def _():
        for c in range(BLOCKS_PER_SEG):
            p_ref[c] = acc[:, c * LANES:(c + 1) * LANES].astype(BF16)


def _projection(x2, mod3, g_norm, w_main, w_tail, gains, gk, seq_len):
    m, d = x2.shape
    tm = 1024
    tiles_per_seq = seq_len // tm
    return pl.pallas_call(
        _proj_kernel,
        grid=(m // tm, N_SEGS),
        in_specs=[
            pl.BlockSpec((tm, d), lambda i, j: (i, 0)),
            pl.BlockSpec((1, 3, d), lambda i, j: (i // tiles_per_seq, 0, 0)),
            pl.BlockSpec((1, d), lambda i, j: (0, 0)),
            pl.BlockSpec((d, D_GROUP), lambda i, j: (0, j)),
            pl.BlockSpec((d, 2 * LANES), lambda i, j: (0, 0)),
            pl.BlockSpec((1, 1, LANES), lambda i, j: (j, 0, 0)),
            pl.BlockSpec((1, LANES), lambda i, j: (0, 0)),
        ],
        out_specs=[
            pl.BlockSpec((BLOCKS_PER_SEG, tm, LANES), lambda i, j: (j, i, 0)),
            pl.BlockSpec((tm, LANES), lambda i, j: (i, 0)),
            pl.BlockSpec((tm, LANES), lambda i, j: (i, 0)),
        ],
        out_shape=[
            jax.ShapeDtypeStruct((N_SEGS * BLOCKS_PER_SEG, m, LANES), BF16),
            jax.ShapeDtypeStruct((m, LANES), BF16),
            jax.ShapeDtypeStruct((m, LANES), F32),
        ],
        scratch_shapes=[pltpu.VMEM((tm, d), BF16)],
        compiler_params=pltpu.CompilerParams(
            dimension_semantics=("arbitrary", "arbitrary"), vmem_limit_bytes=VMEM_LIMIT_BYTES),
        name="in_proj",
    )(x2, mod3, g_norm, w_main, w_tail, gains, gk)


DSA_TQ = 256
DSA_TK = 256
SEARCH_STEPS = 4


def _dsa_kernel(q_ref, k_ref, v_ref, z_ref, qi_ref, kk_ref, wi_ref, o_ref,
                s_ref, qs_ref, wb_ref, t_ref, j_ref, *, top_k, seq_len):
    tq, tk = DSA_TQ, DSA_TK
    i = pl.program_id(1)
    t0 = i * tq
    nk = i + 1
    idx_scale = (IDX_DIM ** -0.5) * (IDX_HEADS ** -0.5)

    lane = lax.broadcasted_iota(jnp.int32, (tq, LANES), 1)
    row_f = lax.broadcasted_iota(jnp.int32, (tq, tk), 0).astype(F32)
    col_f = lax.broadcasted_iota(jnp.int32, (tq, tk), 1).astype(F32)
    ones_bf = jnp.ones((LANES, LANES), BF16)

    def rep2(x):
        return jnp.concatenate([x] * (tk // LANES), axis=1)

    def lane_total(x):
        return jnp.dot(x.astype(BF16), ones_bf, preferred_element_type=F32)

    def fold(x):
        out = x[:, :LANES]
        for c in range(1, tk // LANES):
            out = out + x[:, c * LANES:(c + 1) * LANES]
        return out

    for c in range(BLOCKS_PER_SEG):
        qp = qi_ref[c].astype(F32)
        qs_ref[2 * c] = jnp.where(lane < IDX_DIM, qp, 0.0).astype(BF16)
        qs_ref[2 * c + 1] = jnp.where(lane >= IDX_DIM, qp, 0.0).astype(BF16)
    w = wi_ref[...] * idx_scale
    for h in range(IDX_HEADS):
        wb_ref[h] = jnp.broadcast_to(w[:, h:h + 1], (tq, LANES))

    def score_tile(kt, carry):
        kk = kk_ref[pl.ds(pl.multiple_of(kt * tk, tk), tk), :]
        acc = jnp.zeros((tq, tk), F32)
        for h in range(IDX_HEADS):
            lg = _dot_nt(qs_ref[h], kk)
            acc = acc + rep2(wb_ref[h]) * jnp.maximum(lg, 0.0)
        off = (kt * tk - t0).astype(F32)
        causal = (col_f + off) <= row_f
        s_ref[kt] = jnp.where(causal, acc, -jnp.inf)
        return carry

    lax.fori_loop(0, nk, score_tile, 0)

    t_ref[...] = jnp.full((tq, LANES), NEG_INF, F32)
    j_ref[...] = jnp.full((tq, LANES), -1.0, F32)

    def count(pred):
        def body(kt, acc):
            off = (kt * tk).astype(F32)
            return acc + fold(jnp.where(pred(s_ref[kt], col_f + off), 1.0, 0.0))
        return lane_total(lax.fori_loop(0, nk, body, jnp.zeros((tq, LANES), F32)))

    @pl.when(t0 + tq > top_k)
    def _():
        kf = float(top_k)
        row_pos = lax.broadcasted_iota(jnp.int32, (tq, LANES), 0) + t0
        done0 = jnp.where(row_pos + 1 <= top_k, 1.0, 0.0)

        def minmax(kt, carry):
            mn, mx = carry
            s = s_ref[kt]
            mx = jnp.maximum(mx, jnp.max(s, axis=-1, keepdims=True))
            mn = jnp.minimum(mn, jnp.min(jnp.where(s > 0.5 * NEG_INF, s, BIG), axis=-1, keepdims=True))
            return mn, mx

        mn, mx = lax.fori_loop(0, nk, minmax,
                               (jnp.full((tq, 1), BIG, F32), jnp.full((tq, 1), NEG_INF, F32)))
        lo0 = jnp.broadcast_to(mn, (tq, LANES))
        hi0 = jnp.broadcast_to(mx, (tq, LANES))

        def step(x, degen, state):
            lo, hi, thr, done, tie = state
            c = count(lambda s, col: s > rep2(x))
            active = done == 0.0
            live = jnp.logical_and(active, jnp.logical_not(degen))
            found = jnp.logical_and(live, c == kf)
            new_tie = jnp.logical_and(active, degen)
            thr = jnp.where(found, x, jnp.where(new_tie, hi, thr))
            tie = jnp.where(new_tie, 1.0, tie)
            done = jnp.where(jnp.logical_or(found, new_tie), 1.0, done)
            lo = jnp.where(jnp.logical_and(live, c > kf), x, lo)
            hi = jnp.where(jnp.logical_and(live, c < kf), x, hi)
            return lo, hi, thr, done, tie

        state0 = (lo0, hi0, jnp.full((tq, LANES), NEG_INF, F32), done0, jnp.zeros((tq, LANES), F32))
        state0 = step(lo0, jnp.zeros((tq, LANES), jnp.bool_), state0)

        def cond(carry):
            return carry[1] > 0.0

        def body(carry):
            state, _ = carry
            for _ in range(SEARCH_STEPS):
                lo, hi = state[0], state[1]
                mid = 0.5 * lo + 0.5 * hi
                degen = jnp.logical_or(mid <= lo, mid >= hi)
                state = step(mid, degen, state)
            return state, jnp.max(1.0 - state[3])

        (lo, hi, thr, done, tie), _ = lax.while_loop(cond, body, (state0, jnp.max(1.0 - state0[3])))
        t_ref[...] = thr

        @pl.when(jnp.max(tie) > 0.0)
        def _():
            need = kf - count(lambda s, col: s > rep2(thr))
            n_bits = int(np.ceil(np.log2(seq_len + 1)))

            def jstep(_, carry):
                jlo, jhi = carry
                jmid = jnp.floor((jlo + jhi) * 0.5)
                c = count(lambda s, col: jnp.logical_and(s == rep2(thr), col <= rep2(jmid)))
                ge = c >= need
                return jnp.where(ge, jlo, jmid), jnp.where(ge, jmid, jhi)

            _, jhi = lax.fori_loop(0, n_bits, jstep,
                                   (jnp.full((tq, LANES), -1.0, F32),
                                    jnp.full((tq, LANES), float(seq_len - 1), F32)))
            j_ref[...] = jnp.where(tie > 0.0, jhi, -1.0)

    def mask_tile(kt, carry):
        s = s_ref[kt]
        thr = rep2(t_ref[...])
        off = (kt * tk).astype(F32)
        sel = jnp.logical_or(s > thr, jnp.logical_and(s == thr, (col_f + off) <= rep2(j_ref[...])))
        s_ref[kt] = jnp.where(sel, 0.0, NEG_INF)
        return carry

    lax.fori_loop(0, nk, mask_tile, 0)

    key_rel = lax.broadcasted_iota(jnp.int32, (1, tk), 1).astype(F32)
    for h in range(N_HEADS):
        q = q_ref[h]
        slope = _alibi_slope(h, N_HEADS)

        def att_tile(kt, carry, q=q, slope=slope, h=h):
            m, l, acc = carry
            ks = pl.ds(pl.multiple_of(kt * tk, tk), tk)
            s = _dot_nt(q, k_ref[h, ks, :])
            off = (kt * tk - t0).astype(F32)
            s = s + slope * (key_rel + off) + s_ref[kt]
            m_new = jnp.maximum(m, jnp.max(s, axis=-1, keepdims=True))
            alpha = jnp.exp(m - m_new)
            p = jnp.exp(s - m_new)
            l = alpha * l + jnp.sum(p, axis=-1, keepdims=True)
            acc = alpha * acc + jnp.dot(p.astype(BF16), v_ref[h, ks, :], preferred_element_type=F32)
            return m_new, l, acc

        m, l, acc = lax.fori_loop(
            0, nk, att_tile,
            (jnp.full((tq, 1), NEG_INF, F32), jnp.zeros((tq, 1), F32), jnp.zeros((tq, HEAD_DIM), F32)))
        z = z_ref[h].astype(F32)
        o = (acc / l) * (z * jax.nn.sigmoid(z))
        o_ref[:, h * HEAD_DIM:(h + 1) * HEAD_DIM] = o.astype(BF16)


def _dsa_attention(p, kk, wi, bsz, seq_len):
    tq = DSA_TQ
    nq = seq_len // tq
    m = bsz * seq_len
    top_k = min(DSA_TOPK_MAX, seq_len // 4)
    seg = lambda s: (lambda b, i: (s, b * nq + i, 0))
    seg_full = lambda s: (lambda b, i: (s, b, 0))
    kern = functools.partial(_dsa_kernel, top_k=top_k, seq_len=seq_len)
    return pl.pallas_call(
        kern,
        grid=(bsz, nq),
        in_specs=[
            pl.BlockSpec((BLOCKS_PER_SEG, tq, LANES), seg(SEG_QA)),
            pl.BlockSpec((BLOCKS_PER_SEG, seq_len, LANES), seg_full(SEG_KA)),
            pl.BlockSpec((BLOCKS_PER_SEG, seq_len, LANES), seg_full(SEG_VA)),
            pl.BlockSpec((BLOCKS_PER_SEG, tq, LANES), seg(SEG_ZA)),
            pl.BlockSpec((BLOCKS_PER_SEG, tq, LANES), seg(SEG_QIDX)),
            pl.BlockSpec((seq_len, LANES), lambda b, i: (b, 0)),
            pl.BlockSpec((tq, LANES), lambda b, i: (b * nq + i, 0)),
        ],
        out_specs=pl.BlockSpec((tq, D_GROUP), lambda b, i: (b * nq + i, 0)),
        out_shape=jax.ShapeDtypeStruct((m, D_GROUP), BF16),
        scratch_shapes=[
            pltpu.VMEM((seq_len // DSA_TK, tq, DSA_TK), F32),
            pltpu.VMEM((IDX_HEADS, tq, LANES), BF16),
            pltpu.VMEM((IDX_HEADS, tq, LANES), F32),
            pltpu.VMEM((tq, LANES), F32),
            pltpu.VMEM((tq, LANES), F32),
        ],
        compiler_params=pltpu.CompilerParams(
            dimension_semantics=("arbitrary", "arbitrary"), vmem_limit_bytes=VMEM_LIMIT_BYTES),
        name="dsa_attn",
    )(p, p, p, p, p, kk, wi)


POS_TILE = MOBA_BLOCK
AUG_KLOC, AUG_KBLK, AUG_ONE_LOC, AUG_ONE_BLK, AUG_ONEHOT0 = 0, 1, 2, 3, 8


def _key_features(seq_len):
    pos = np.arange(seq_len)
    f = np.zeros((seq_len, LANES), np.float32)
    f[:, AUG_KLOC] = pos % POS_TILE
    f[:, AUG_KBLK] = pos // POS_TILE
    f[:, AUG_ONE_LOC] = 1.0
    f[:, AUG_ONE_BLK] = 1.0
    f[pos, AUG_ONEHOT0 + pos // POS_TILE] = 1.0
    return jnp.asarray(f, BF16)


def _alibi_query_features(slope, q_tile, tq):
    lane = lax.broadcasted_iota(jnp.int32, (tq, LANES), 1)
    row = lax.broadcasted_iota(jnp.int32, (tq, LANES), 0).astype(F32)
    f = jnp.where(lane == AUG_KLOC, slope, 0.0)
    f = jnp.where(lane == AUG_KBLK, slope * POS_TILE, f)
    f = jnp.where(lane == AUG_ONE_LOC, -slope * row, f)
    return jnp.where(lane == AUG_ONE_BLK, (-slope * POS_TILE) * q_tile.astype(F32), f)


def _fold_rows(x, op):
    return op(x.reshape(x.shape[0] // 8, 8, x.shape[1]), axis=0)


def _moba_kernel(q_ref, k_ref, v_ref, z_ref, kaug_ref, o_ref,
                 kmean_ref, vt_ref, selb_ref, qcat_ref, s_ref, acc_ref, *, n_blocks):
    bs = MOBA_BLOCK
    n = pl.program_id(1)
    top_k = min(MOBA_TOPK_MAX, n_blocks - 1)
    nb_pad = kmean_ref.shape[1]

    @pl.when(n == 0)
    def _():
        kmean_ref[...] = jnp.zeros(kmean_ref.shape, F32)
        for h in range(N_HEADS):
            for j in range(n_blocks):
                kb = k_ref[h, j * bs:(j + 1) * bs, :].astype(F32)
                kmean_ref[h, j:j + 1, :] = jnp.mean(kb, axis=0, keepdims=True)
                vb = v_ref[h, j * bs:(j + 1) * bs, :].astype(F32)
                vt_ref[h, j] = vb.T.astype(BF16)

    blk = lax.broadcasted_iota(jnp.int32, (nb_pad, bs), 0)
    past = blk < n

    selb_ref[...] = jnp.zeros(selb_ref.shape, F32)
    for h in range(N_HEADS):
        q = q_ref[h]
        g = _dot_nt(kmean_ref[h].astype(BF16), q)
        for j in range(n_blocks):
            gj = g[j:j + 1, :]
            beats = jnp.logical_or(g > gj, jnp.logical_and(g == gj, blk < j))
            rank = jnp.sum(jnp.where(jnp.logical_and(beats, past), 1.0, 0.0), axis=0, keepdims=True)
            dropped = jnp.logical_and(rank >= float(top_k), j < n)
            selb_ref[AUG_ONEHOT0 + j:AUG_ONEHOT0 + j + 1, :] = jnp.where(dropped, NEG_INF, 0.0)
        feat = _alibi_query_features(_alibi_slope(h, N_HEADS), n, bs) + selb_ref[...].T
        qcat_ref[h] = jnp.concatenate([q, feat.astype(BF16)], axis=1)

    key_loc = lax.broadcasted_iota(jnp.int32, (bs, bs), 0)
    qry_loc = lax.broadcasted_iota(jnp.int32, (bs, bs), 1)
    causal = key_loc <= qry_loc
    own = pl.ds(pl.multiple_of(n * bs, bs), bs)
    kaug_own = kaug_ref[own, :]
    mx = []
    for h in range(N_HEADS):
        s = _dot_nt(jnp.concatenate([k_ref[h, own, :], kaug_own], axis=1), qcat_ref[h])
        s = jnp.where(causal, s, NEG_INF)
        s_ref[h, n] = s
        mx.append(_fold_rows(s, jnp.max))

    def pass_a(j, mx):
        ks = pl.ds(pl.multiple_of(j * bs, bs), bs)
        kaug = kaug_ref[ks, :]
        out = []
        for h in range(N_HEADS):
            s = _dot_nt(jnp.concatenate([k_ref[h, ks, :], kaug], axis=1), qcat_ref[h])
            s_ref[h, j] = s
            out.append(jnp.maximum(mx[h], _fold_rows(s, jnp.max)))
        return tuple(out)

    mx = lax.fori_loop(0, n, pass_a, tuple(mx))
    m = [jnp.max(mx[h], axis=0, keepdims=True) for h in range(N_HEADS)]

    acc_ref[...] = jnp.zeros(acc_ref.shape, F32)

    def pass_b(j, ls):
        out = []
        for h in range(N_HEADS):
            p = jnp.exp(s_ref[h, j] - m[h])
            out.append(ls[h] + _fold_rows(p, jnp.sum))
            acc_ref[h] += jnp.dot(vt_ref[h, j], p.astype(BF16), preferred_element_type=F32)
        return tuple(out)

    ls = lax.fori_loop(0, n + 1, pass_b, tuple(jnp.zeros((8, bs), F32) for _ in range(N_HEADS)))

    for h in range(N_HEADS):
        l = jnp.sum(ls[h], axis=0, keepdims=True)
        z = z_ref[h].astype(F32)
        o = (acc_ref[h] / l).T * (z * jax.nn.sigmoid(z))
        o_ref[:, h * HEAD_DIM:(h + 1) * HEAD_DIM] = o.astype(BF16)


def _moba_attention(p, kaug, bsz, seq_len):
    bs = MOBA_BLOCK
    nb = seq_len // bs
    m = bsz * seq_len
    seg = lambda s: (lambda b, i: (s, b * nb + i, 0))
    seg_full = lambda s: (lambda b, i: (s, b, 0))
    kern = functools.partial(_moba_kernel, n_blocks=nb)
    nb_pad = max(16, nb)
    assert AUG_ONEHOT0 + nb <= LANES
    return pl.pallas_call(
        kern,
        grid=(bsz, nb),
        in_specs=[
            pl.BlockSpec((BLOCKS_PER_SEG, bs, LANES), seg(SEG_QB)),
            pl.BlockSpec((BLOCKS_PER_SEG, seq_len, LANES), seg_full(SEG_KB)),
            pl.BlockSpec((BLOCKS_PER_SEG, seq_len, LANES), seg_full(SEG_VB)),
            pl.BlockSpec((BLOCKS_PER_SEG, bs, LANES), seg(SEG_ZB)),
            pl.BlockSpec((seq_len, LANES), lambda b, i: (0, 0)),
        ],
        out_specs=pl.BlockSpec((bs, D_GROUP), lambda b, i: (b * nb + i, 0)),
        out_shape=jax.ShapeDtypeStruct((m, D_GROUP), BF16),
        scratch_shapes=[
            pltpu.VMEM((N_HEADS, nb_pad, HEAD_DIM), F32),
            pltpu.VMEM((N_HEADS, nb, HEAD_DIM, bs), BF16),
            pltpu.VMEM((LANES, bs), F32),
            pltpu.VMEM((N_HEADS, bs, 2 * LANES), BF16),
            pltpu.VMEM((N_HEADS, nb, bs, bs), F32),
            pltpu.VMEM((N_HEADS, HEAD_DIM, bs), F32),
        ],
        compiler_params=pltpu.CompilerParams(
            dimension_semantics=("arbitrary", "arbitrary"), vmem_limit_bytes=VMEM_LIMIT_BYTES),
        name="moba_attn",
    )(p, p, p, p, kaug)


def _out_kernel(ya_ref, yb_ref, wa_ref, wb_ref, x_ref, mod_ref, o_ref):
    y = jnp.dot(ya_ref[...], wa_ref[...], preferred_element_type=F32)
    y = y + jnp.dot(yb_ref[...], wb_ref[...], preferred_element_type=F32)
    gate = mod_ref[0, 2:3, :]
    o_ref[...] = x_ref[...] + gate * y


def _out_projection(ya, yb, w_out_bf, x2, mod3, seq_len):
    m, d = x2.shape
    tm, tn = 512, 1024
    tiles_per_seq = seq_len // tm
    return pl.pallas_call(
        _out_kernel,
        grid=(m // tm, d // tn),
        in_specs=[
            pl.BlockSpec((tm, D_GROUP), lambda i, j: (i, 0)),
            pl.BlockSpec((tm, D_GROUP), lambda i, j: (i, 0)),
            pl.BlockSpec((D_GROUP, tn), lambda i, j: (0, j)),
            pl.BlockSpec((D_GROUP, tn), lambda i, j: (1, j)),
            pl.BlockSpec((tm, tn), lambda i, j: (i, j)),
            pl.BlockSpec((1, 3, tn), lambda i, j: (i // tiles_per_seq, 0, j)),
        ],
        out_specs=pl.BlockSpec((tm, tn), lambda i, j: (i, j)),
        out_shape=jax.ShapeDtypeStruct((m, d), F32),
        compiler_params=pltpu.CompilerParams(
            dimension_semantics=("arbitrary", "arbitrary"), vmem_limit_bytes=VMEM_LIMIT_BYTES),
        name="out_proj",
    )(ya, yb, w_out_bf, w_out_bf, x2, mod3)


def _layer(x, c, w_ada, b_ada, g_norm, w_in, q_norm_a, k_norm_a, k_norm_idx, q_norm_b, k_norm_b, w_out):
    bsz, seq_len, d = x.shape
    assert seq_len % MOBA_BLOCK == 0 and seq_len % 1024 == 0 and d % 1024 == 0
    x2 = x.reshape(bsz * seq_len, d)

    mod3 = _modulation(c, w_ada, b_ada).reshape(bsz, 3, d)

    o = 0
    cols = {}
    for name, width in (("qa", D_GROUP), ("ka", D_GROUP), ("va", D_GROUP), ("za", D_GROUP),
                        ("qidx", IDX_HEADS * IDX_DIM), ("kidx", IDX_DIM), ("widx", IDX_HEADS),
                        ("qb", D_GROUP), ("kb", D_GROUP), ("vb", D_GROUP), ("zb", D_GROUP)):
        cols[name] = w_in[:, o:o + width]
        o += width
    w_main = jnp.concatenate(
        [cols[k] for k in ("qa", "ka", "va", "za", "qidx", "qb", "kb", "vb", "zb")], axis=1).astype(BF16)
    w_tail = jnp.concatenate(
        [cols["kidx"], cols["kidx"], cols["widx"],
         jnp.zeros((d, LANES - IDX_HEADS), w_in.dtype)], axis=1).astype(BF16)
    ones = jnp.ones((HEAD_DIM,), F32)
    sm_scale = HEAD_DIM ** -0.5
    gains = jnp.stack([q_norm_a * sm_scale, k_norm_a, ones, ones, ones,
                       q_norm_b * sm_scale, k_norm_b, ones, ones]).reshape(N_SEGS, 1, HEAD_DIM)
    for h in range(N_HEADS):
        slope = _alibi_slope(h, N_HEADS)
        assert float(np.float32(slope).astype(jnp.bfloat16)) == slope, "ALiBi slopes must be exact in bf16"
    assert seq_len // POS_TILE <= 256 and POS_TILE <= 256, "positions must split into bf16-exact parts"
    kaug = _key_features(seq_len)
    gk = jnp.concatenate([k_norm_idx, k_norm_idx]).reshape(1, LANES)

    p, kk, wi = _projection(x2, mod3, g_norm.reshape(1, d), w_main, w_tail, gains, gk, seq_len)
    ya = _dsa_attention(p, kk, wi, bsz, seq_len)
    yb = _moba_attention(p, kaug, bsz, seq_len)
    out = _out_projection(ya, yb, w_out.astype(BF16), x2, mod3, seq_len)
    return out.reshape(bsz, seq_len, d)


def kernel(x, c, w_ada, b_ada, g_norm, w_in, q_norm_a, k_norm_a, k_norm_idx, q_norm_b, k_norm_b, w_out):
    for i in range(w_ada.shape[0]):
        x = _layer(x, c, w_ada[i], b_ada[i], g_norm[i], w_in[i], q_norm_a[i], k_norm_a[i],
                   k_norm_idx[i], q_norm_b[i], k_norm_b[i], w_out[i])
    return x
```

```python
import functools

import jax
import jax.numpy as jnp
import numpy as np
from jax import lax
from jax.experimental import pallas as pl
from jax.experimental.pallas import tpu as pltpu

F32 = jnp.float32
BF16 = jnp.bfloat16

HEAD_DIM = 128
N_HEADS = 8
D_GROUP = N_HEADS * HEAD_DIM
IDX_HEADS = 16
IDX_DIM = 64
DSA_TOPK_MAX = 256
MOBA_BLOCK = 256
MOBA_TOPK_MAX = 3
RMS_EPS = 1e-6
NEG_INF = -1e30
BIG = 1e30

LANES = 128
VMEM_LIMIT_BYTES = 56 * 1024 * 1024

SEG_QA, SEG_KA, SEG_VA, SEG_ZA, SEG_QIDX, SEG_QB, SEG_KB, SEG_VB, SEG_ZB = range(9)
N_SEGS = 9
BLOCKS_PER_SEG = D_GROUP // LANES


def _alibi_slope(h, n):
    return float(2.0 ** (-8.0 * (h + 1) / n))


def _dot_nt(a, b):
    return lax.dot_general(a, b, (((1,), (1,)), ((), ())), preferred_element_type=F32)


def _mod_kernel(c_ref, w_ref, b_ref, o_ref):
    c = c_ref[...]
    s = c * jax.nn.sigmoid(c)
    o_ref[...] = jnp.dot(s, w_ref[...], preferred_element_type=F32) + b_ref[...]


def _modulation(c, w_ada, b_ada):
    bsz, d = c.shape
    n = w_ada.shape[1]
    tn = 1024
    return pl.pallas_call(
        _mod_kernel,
        grid=(n // tn,),
        in_specs=[
            pl.BlockSpec((bsz, d), lambda j: (0, 0)),
            pl.BlockSpec((d, tn), lambda j: (0, j)),
            pl.BlockSpec((1, tn), lambda j: (0, j)),
        ],
        out_specs=pl.BlockSpec((bsz, tn), lambda j: (0, j)),
        out_shape=jax.ShapeDtypeStruct((bsz, n), F32),
        compiler_params=pltpu.CompilerParams(
            dimension_semantics=("arbitrary",), vmem_limit_bytes=VMEM_LIMIT_BYTES),
        name="adaln_mod",
    )(c, w_ada, b_ada.reshape(1, n))


def _proj_kernel(x_ref, mod_ref, g_ref, w_ref, wt_ref, gain_ref, gk_ref,
                 p_ref, kk_ref, wi_ref, h_ref):
    j = pl.program_id(1)

    @pl.when(j == 0)
    def _():
        x = x_ref[...]
        ms = jnp.mean(x * x, axis=-1, keepdims=True)
        y = x * lax.rsqrt(ms + RMS_EPS) * g_ref[...]
        shift = mod_ref[0, 0:1, :]
        scale = mod_ref[0, 1:2, :]
        hb = (y * (1.0 + scale) + shift).astype(BF16)
        h_ref[...] = hb
        t = jnp.dot(hb, wt_ref[...], preferred_element_type=F32)
        tk = t[:, :LANES]
        kms = jnp.mean(tk * tk, axis=-1, keepdims=True)
        kk_ref[...] = (tk * lax.rsqrt(kms + RMS_EPS) * gk_ref[...]).astype(BF16)
        wi_ref[...] = t[:, LANES:]

    acc = jnp.dot(h_ref[...], w_ref[...], preferred_element_type=F32)
    is_norm = (j == SEG_QA) | (j == SEG_KA) | (j == SEG_QB) | (j == SEG_KB)

    @pl.when(is_norm)
    def _():
        g = gain_ref[0]
        for c in range(BLOCKS_PER_SEG):
            a = acc[:, c * LANES:(c + 1) * LANES]
            ms = jnp.mean(a * a, axis=-1, keepdims=True)
            p_ref[c] = (a * lax.rsqrt(ms + RMS_EPS) * g).astype(BF16)

    @pl.when(jnp.logical_not(is_norm))
    def _():
        for c in range(BLOCKS_PER_SEG):
            p_ref[c] = acc[:, c * LANES:(c + 1) * LANES].astype(BF16)


def _projection(x2, mod3, g_norm, w_main, w_tail, gains, gk, seq_len):
    m, d = x2.shape
    tm = 1024
    tiles_per_seq = seq_len // tm
    return pl.pallas_call(
        _proj_kernel,
        grid=(m // tm, N_SEGS),
        in_specs=[
            pl.BlockSpec((tm, d), lambda i, j: (i, 0)),
            pl.BlockSpec((1, 3, d), lambda i, j: (i // tiles_per_seq, 0, 0)),
            pl.BlockSpec((1, d), lambda i, j: (0, 0)),
            pl.BlockSpec((d, D_GROUP), lambda i, j: (0, j)),
            pl.BlockSpec((d, 2 * LANES), lambda i, j: (0, 0)),
            pl.BlockSpec((1, 1, LANES), lambda i, j: (j, 0, 0)),
            pl.BlockSpec((1, LANES), lambda i, j: (0, 0)),
        ],
        out_specs=[
            pl.BlockSpec((BLOCKS_PER_SEG, tm, LANES), lambda i, j: (j, i, 0)),
            pl.BlockSpec((tm, LANES), lambda i, j: (i, 0)),
            pl.BlockSpec((tm, LANES), lambda i, j: (i, 0)),
        ],
        out_shape=[
            jax.ShapeDtypeStruct((N_SEGS * BLOCKS_PER_SEG, m, LANES), BF16),
            jax.ShapeDtypeStruct((m, LANES), BF16),
            jax.ShapeDtypeStruct((m, LANES), F32),
        ],
        scratch_shapes=[pltpu.VMEM((tm, d), BF16)],
        compiler_params=pltpu.CompilerParams(
            dimension_semantics=("arbitrary", "arbitrary"), vmem_limit_bytes=VMEM_LIMIT_BYTES),
        name="in_proj",
    )(x2, mod3, g_norm, w_main, w_tail, gains, gk)


POS_TILE = MOBA_BLOCK
AUG_KLOC, AUG_KBLK, AUG_ONE_LOC, AUG_ONE_BLK, AUG_ONEHOT0 = 0, 1, 2, 3, 8


def _key_features(seq_len):
    pos = np.arange(seq_len)
    f = np.zeros((seq_len, LANES), np.float32)
    f[:, AUG_KLOC] = pos % POS_TILE
    f[:, AUG_KBLK] = pos // POS_TILE
    f[:, AUG_ONE_LOC] = 1.0
    f[:, AUG_ONE_BLK] = 1.0
    f[pos, AUG_ONEHOT0 + pos // POS_TILE] = 1.0
    return jnp.asarray(f, BF16)


def _alibi_query_features(slope, q_tile, tq):
    lane = lax.broadcasted_iota(jnp.int32, (tq, LANES), 1)
    row = lax.broadcasted_iota(jnp.int32, (tq, LANES), 0).astype(F32)
    f = jnp.where(lane == AUG_KLOC, slope, 0.0)
    f = jnp.where(lane == AUG_KBLK, slope * POS_TILE, f)
    f = jnp.where(lane == AUG_ONE_LOC, -slope * row, f)
    return jnp.where(lane == AUG_ONE_BLK, (-slope * POS_TILE) * q_tile.astype(F32), f)


def _fold_rows(x, op):
    r = x.reshape(x.shape[0] // 8, 8, x.shape[1])
    while r.shape[0] > 1:
        half = r.shape[0] // 2
        r = op(r[:half], r[half:])
    return r[0]


def _transpose_values(v_ref, vt_ref, n_tiles, tile):
    for h in range(N_HEADS):
        for j in range(n_tiles):
            vt_ref[h, j] = v_ref[h, j * tile:(j + 1) * tile, :].astype(F32).T.astype(BF16)


DSA_TQ = 256
DSA_TK = POS_TILE
SEARCH_STEPS = 4


def _dsa_kernel(q_ref, k_ref, v_ref, z_ref, qi_ref, kk_ref, wi_ref, kaug_ref, o_ref,
                vt_ref, qs_ref, sc_ref, t_ref, j_ref, qcat_ref, s_ref, acc_ref, *, top_k, seq_len):
    tq, tk = DSA_TQ, DSA_TK
    i = pl.program_id(1)
    t0 = i * tq
    nk = i + 1
    idx_scale = (IDX_DIM ** -0.5) * (IDX_HEADS ** -0.5)

    @pl.when(i == 0)
    def _():
        _transpose_values(v_ref, vt_ref, seq_len // tk, tk)

    lane = lax.broadcasted_iota(jnp.int32, (tq, LANES), 1)
    key_loc = lax.broadcasted_iota(jnp.int32, (tk, tq), 0).astype(F32)
    qry_loc = lax.broadcasted_iota(jnp.int32, (tk, tq), 1).astype(F32)

    for c in range(BLOCKS_PER_SEG):
        qp = qi_ref[c].astype(F32)
        qs_ref[2 * c] = jnp.where(lane < IDX_DIM, qp, 0.0).astype(BF16)
        qs_ref[2 * c + 1] = jnp.where(lane >= IDX_DIM, qp, 0.0).astype(BF16)
    w_t = (wi_ref[...] * idx_scale).T
    w_rows = [w_t[h:h + 1, :] for h in range(IDX_HEADS)]

    def score_tile(kt, carry):
        kk = kk_ref[pl.ds(pl.multiple_of(kt * tk, tk), tk), :]
        acc = jnp.zeros((tk, tq), F32)
        for h in range(IDX_HEADS):
            acc = acc + w_rows[h] * jnp.maximum(_dot_nt(kk, qs_ref[h]), 0.0)
        off = (kt * tk - t0).astype(F32)
        sc_ref[kt] = jnp.where((key_loc + off) <= qry_loc, acc, -jnp.inf)
        return carry

    lax.fori_loop(0, nk, score_tile, 0)

    t_ref[...] = jnp.full((1, tq), NEG_INF, F32)
    j_ref[...] = jnp.full((1, tq), -1.0, F32)

    def count(pred):
        def body(kt, acc):
            off = (kt * tk).astype(F32)
            return acc + _fold_rows(jnp.where(pred(sc_ref[kt], key_loc + off), 1.0, 0.0), jnp.add)
        return jnp.sum(lax.fori_loop(0, nk, body, jnp.zeros((8, tq), F32)), axis=0, keepdims=True)

    @pl.when(t0 + tq > top_k)
    def _():
        kf = float(top_k)
        qpos = lax.broadcasted_iota(jnp.int32, (1, tq), 1) + t0
        done0 = jnp.where(qpos + 1 <= top_k, 1.0, 0.0)

        def minmax(kt, carry):
            mn, mx = carry
            s = sc_ref[kt]
            mx = jnp.maximum(mx, _fold_rows(s, jnp.maximum))
            mn = jnp.minimum(mn, _fold_rows(jnp.where(s > 0.5 * NEG_INF, s, BIG), jnp.minimum))
            return mn, mx

        mn, mx = lax.fori_loop(0, nk, minmax,
                               (jnp.full((8, tq), BIG, F32), jnp.full((8, tq), NEG_INF, F32)))
        lo0 = jnp.min(mn, axis=0, keepdims=True)
        hi0 = jnp.max(mx, axis=0, keepdims=True)

        def step(x, degen, state):
            lo, hi, thr, done, tie = state
            c = count(lambda s, pos: s > x)
            active = done == 0.0
            live = jnp.logical_and(active, jnp.logical_not(degen))
            found = jnp.logical_and(live, c == kf)
            new_tie = jnp.logical_and(active, degen)
            thr = jnp.where(found, x, jnp.where(new_tie, hi, thr))
            tie = jnp.where(new_tie, 1.0, tie)
            done = jnp.where(jnp.logical_or(found, new_tie), 1.0, done)
            lo = jnp.where(jnp.logical_and(live, c > kf), x, lo)
            hi = jnp.where(jnp.logical_and(live, c < kf), x, hi)
            return lo, hi, thr, done, tie

        state0 = (lo0, hi0, jnp.full((1, tq), NEG_INF, F32), done0, jnp.zeros((1, tq), F32))
        state0 = step(lo0, jnp.zeros((1, tq), jnp.bool_), state0)

        def cond(carry):
            return carry[1] > 0.0

        def body(carry):
            state, _ = carry
            for _ in range(SEARCH_STEPS):
                lo, hi = state[0], state[1]
                mid = 0.5 * lo + 0.5 * hi
                degen = jnp.logical_or(mid <= lo, mid >= hi)
                state = step(mid, degen, state)
            return state, jnp.max(1.0 - state[3])

        (lo, hi, thr, done, tie), _ = lax.while_loop(cond, body, (state0, jnp.max(1.0 - state0[3])))
        t_ref[...] = thr

        @pl.when(jnp.max(tie) > 0.0)
        def _():
            need = kf - count(lambda s, pos: s > thr)
            n_bits = int(np.ceil(np.log2(seq_len + 1)))

            def jstep(_, carry):
                jlo, jhi = carry
                jmid = jnp.floor((jlo + jhi) * 0.5)
                c = count(lambda s, pos: jnp.logical_and(s == thr, pos <= jmid))
                ge = c >= need
                return jnp.where(ge, jlo, jmid), jnp.where(ge, jmid, jhi)

            _, jhi = lax.fori_loop(0, n_bits, jstep,
                                   (jnp.full((1, tq), -1.0, F32),
                                    jnp.full((1, tq), float(seq_len - 1), F32)))
            j_ref[...] = jnp.where(tie > 0.0, jhi, -1.0)

    def mask_tile(kt, carry):
        s = sc_ref[kt]
        thr = t_ref[...]
        off = (kt * tk).astype(F32)
        sel = jnp.logical_or(s > thr, jnp.logical_and(s == thr, (key_loc + off) <= j_ref[...]))
        sc_ref[kt] = jnp.where(sel, 0.0, NEG_INF)
        return carry

    lax.fori_loop(0, nk, mask_tile, 0)

    for h in range(N_HEADS):
        feat = _alibi_query_features(_alibi_slope(h, N_HEADS), i, tq)
        qcat_ref[h] = jnp.concatenate([q_ref[h], feat.astype(BF16)], axis=1)

    def pass_a(kt, mx):
        ks = pl.ds(pl.multiple_of(kt * tk, tk), tk)
        kaug = kaug_ref[ks, :]
        bias = sc_ref[kt]
        out = []
        for h in range(N_HEADS):
            s = _dot_nt(jnp.concatenate([k_ref[h, ks, :], kaug], axis=1), qcat_ref[h]) + bias
            s_ref[h, kt] = s
            out.append(jnp.maximum(mx[h], _fold_rows(s, jnp.maximum)))
        return tuple(out)

    mx = lax.fori_loop(0, nk, pass_a, tuple(jnp.full((8, tq), NEG_INF, F32) for _ in range(N_HEADS)))
    m = [jnp.max(mx[h], axis=0, keepdims=True) for h in range(N_HEADS)]

    acc_ref[...] = jnp.zeros(acc_ref.shape, F32)

    def pass_b(kt, ls):
        out = []
        for h in range(N_HEADS):
            p = jnp.exp(s_ref[h, kt] - m[h])
            out.append(ls[h] + _fold_rows(p, jnp.add))
            acc_ref[h] += jnp.dot(vt_ref[h, kt], p.astype(BF16), preferred_element_type=F32)
        return tuple(out)

    ls = lax.fori_loop(0, nk, pass_b, tuple(jnp.zeros((8, tq), F32) for _ in range(N_HEADS)))

    for h in range(N_HEADS):
        l = jnp.sum(ls[h], axis=0, keepdims=True)
        z = z_ref[h].astype(F32)
        o = (acc_ref[h] / l).T * (z * jax.nn.sigmoid(z))
        o_ref[:, h * HEAD_DIM:(h + 1) * HEAD_DIM] = o.astype(BF16)


def _dsa_attention(p, kk, wi, kaug, bsz, seq_len):
    tq, tk = DSA_TQ, DSA_TK
    nq = seq_len // tq
    nkt = seq_len // tk
    m = bsz * seq_len
    top_k = min(DSA_TOPK_MAX, seq_len // 4)
    seg = lambda s: (lambda b, i: (s, b * nq + i, 0))
    seg_full = lambda s: (lambda b, i: (s, b, 0))
    kern = functools.partial(_dsa_kernel, top_k=top_k, seq_len=seq_len)
    return pl.pallas_call(
        kern,
        grid=(bsz, nq),
        in_specs=[
            pl.BlockSpec((BLOCKS_PER_SEG, tq, LANES), seg(SEG_QA)),
            pl.BlockSpec((BLOCKS_PER_SEG, seq_len, LANES), seg_full(SEG_KA)),
            pl.BlockSpec((BLOCKS_PER_SEG, seq_len, LANES), seg_full(SEG_VA)),
            pl.BlockSpec((BLOCKS_PER_SEG, tq, LANES), seg(SEG_ZA)),
            pl.BlockSpec((BLOCKS_PER_SEG, tq, LANES), seg(SEG_QIDX)),
            pl.BlockSpec((seq_len, LANES), lambda b, i: (b, 0)),
            pl.BlockSpec((tq, LANES), lambda b, i: (b * nq + i, 0)),
            pl.BlockSpec((seq_len, LANES), lambda b, i: (0, 0)),
        ],
        out_specs=pl.BlockSpec((tq, D_GROUP), lambda b, i: (b * nq + i, 0)),
        out_shape=jax.ShapeDtypeStruct((m, D_GROUP), BF16),
        scratch_shapes=[
            pltpu.VMEM((N_HEADS, nkt, HEAD_DIM, tk), BF16),
            pltpu.VMEM((IDX_HEADS, tq, LANES), BF16),
            pltpu.VMEM((nkt, tk, tq), F32),
            pltpu.VMEM((1, tq), F32),
            pltpu.VMEM((1, tq), F32),
            pltpu.VMEM((N_HEADS, tq, 2 * LANES), BF16),
            pltpu.VMEM((N_HEADS, nkt, tk, tq), F32),
            pltpu.VMEM((N_HEADS, HEAD_DIM, tq), F32),
        ],
        compiler_params=pltpu.CompilerParams(
            dimension_semantics=("arbitrary", "arbitrary"), vmem_limit_bytes=VMEM_LIMIT_BYTES),
        name="dsa_attn",
    )(p, p, p, p, p, kk, wi, kaug)


def _moba_kernel(q_ref, k_ref, v_ref, z_ref, kaug_ref, o_ref,
                 kmean_ref, vt_ref, selb_ref, qcat_ref, s_ref, acc_ref, *, n_blocks):
    bs = MOBA_BLOCK
    n = pl.program_id(1)
    top_k = min(MOBA_TOPK_MAX, n_blocks - 1)
    nb_pad = kmean_ref.shape[1]

    @pl.when(n == 0)
    def _():
        kmean_ref[...] = jnp.zeros(kmean_ref.shape, F32)
        for h in range(N_HEADS):
            for j in range(n_blocks):
                kb = k_ref[h, j * bs:(j + 1) * bs, :].astype(F32)
                kmean_ref[h, j:j + 1, :] = jnp.mean(kb, axis=0, keepdims=True)
        _transpose_values(v_ref, vt_ref, n_blocks, bs)

    blk = lax.broadcasted_iota(jnp.int32, (nb_pad, bs), 0)
    past = blk < n

    selb_ref[...] = jnp.zeros(selb_ref.shape, F32)
    for h in range(N_HEADS):
        q = q_ref[h]
        g = _dot_nt(kmean_ref[h].astype(BF16), q)
        for j in range(n_blocks):
            gj = g[j:j + 1, :]
            beats = jnp.logical_or(g > gj, jnp.logical_and(g == gj, blk < j))
            rank = jnp.sum(jnp.where(jnp.logical_and(beats, past), 1.0, 0.0), axis=0, keepdims=True)
            dropped = jnp.logical_and(rank >= float(top_k), j < n)
            selb_ref[AUG_ONEHOT0 + j:AUG_ONEHOT0 + j + 1, :] = jnp.where(dropped, NEG_INF, 0.0)
        feat = _alibi_query_features(_alibi_slope(h, N_HEADS), n, bs) + selb_ref[...].T
        qcat_ref[h] = jnp.concatenate([q, feat.astype(BF16)], axis=1)

    key_loc = lax.broadcasted_iota(jnp.int32, (bs, bs), 0)
    qry_loc = lax.broadcasted_iota(jnp.int32, (bs, bs), 1)
    causal = key_loc <= qry_loc
    own = pl.ds(pl.multiple_of(n * bs, bs), bs)
    kaug_own = kaug_ref[own, :]
    mx = []
    for h in range(N_HEADS):
        s = _dot_nt(jnp.concatenate([k_ref[h, own, :], kaug_own], axis=1), qcat_ref[h])
        s = jnp.where(causal, s, NEG_INF)
        s_ref[h, n] = s
        mx.append(_fold_rows(s, jnp.maximum))

    def pass_a(j, mx):
        ks = pl.ds(pl.multiple_of(j * bs, bs), bs)
        kaug = kaug_ref[ks, :]
        out = []
        for h in range(N_HEADS):
            s = _dot_nt(jnp.concatenate([k_ref[h, ks, :], kaug], axis=1), qcat_ref[h])
            s_ref[h, j] = s
            out.append(jnp.maximum(mx[h], _fold_rows(s, jnp.maximum)))
        return tuple(out)

    mx = lax.fori_loop(0, n, pass_a, tuple(mx))
    m = [jnp.max(mx[h], axis=0, keepdims=True) for h in range(N_HEADS)]

    acc_ref[...] = jnp.zeros(acc_ref.shape, F32)

    def pass_b(j, ls):
        out = []
        for h in range(N_HEADS):
            p = jnp.exp(s_ref[h, j] - m[h])
            out.append(ls[h] + _fold_rows(p, jnp.add))
            acc_ref[h] += jnp.dot(vt_ref[h, j], p.astype(BF16), preferred_element_type=F32)
        return tuple(out)

    ls = lax.fori_loop(0, n + 1, pass_b, tuple(jnp.zeros((8, bs), F32) for _ in range(N_HEADS)))

    for h in range(N_HEADS):
        l = jnp.sum(ls[h], axis=0, keepdims=True)
        z = z_ref[h].astype(F32)
        o = (acc_ref[h] / l).T * (z * jax.nn.sigmoid(z))
        o_ref[:, h * HEAD_DIM:(h + 1) * HEAD_DIM] = o.astype(BF16)


def _moba_attention(p, kaug, bsz, seq_len):
    bs = MOBA_BLOCK
    nb = seq_len // bs
    m = bsz * seq_len
    seg = lambda s: (lambda b, i: (s, b * nb + i, 0))
    seg_full = lambda s: (lambda b, i: (s, b, 0))
    kern = functools.partial(_moba_kernel, n_blocks=nb)
    nb_pad = max(16, nb)
    assert AUG_ONEHOT0 + nb <= LANES
    return pl.pallas_call(
        kern,
        grid=(bsz, nb),
        in_specs=[
            pl.BlockSpec((BLOCKS_PER_SEG, bs, LANES), seg(SEG_QB)),
            pl.BlockSpec((BLOCKS_PER_SEG, seq_len, LANES), seg_full(SEG_KB)),
            pl.BlockSpec((BLOCKS_PER_SEG, seq_len, LANES), seg_full(SEG_VB)),
            pl.BlockSpec((BLOCKS_PER_SEG, bs, LANES), seg(SEG_ZB)),
            pl.BlockSpec((seq_len, LANES), lambda b, i: (0, 0)),
        ],
        out_specs=pl.BlockSpec((bs, D_GROUP), lambda b, i: (b * nb + i, 0)),
        out_shape=jax.ShapeDtypeStruct((m, D_GROUP), BF16),
        scratch_shapes=[
            pltpu.VMEM((N_HEADS, nb_pad, HEAD_DIM), F32),
            pltpu.VMEM((N_HEADS, nb, HEAD_DIM, bs), BF16),
            pltpu.VMEM((LANES, bs), F32),
            pltpu.VMEM((N_HEADS, bs, 2 * LANES), BF16),
            pltpu.VMEM((N_HEADS, nb, bs, bs), F32),
            pltpu.VMEM((N_HEADS, HEAD_DIM, bs), F32),
        ],
        compiler_params=pltpu.CompilerParams(
            dimension_semantics=("arbitrary", "arbitrary"), vmem_limit_bytes=VMEM_LIMIT_BYTES),
        name="moba_attn",
    )(p, p, p, p, kaug)


def _out_kernel(ya_ref, yb_ref, wa_ref, wb_ref, x_ref, mod_ref, o_ref):
    y = jnp.dot(ya_ref[...], wa_ref[...], preferred_element_type=F32)
    y = y + jnp.dot(yb_ref[...], wb_ref[...], preferred_element_type=F32)
    gate = mod_ref[0, 2:3, :]
    o_ref[...] = x_ref[...] + gate * y


def _out_projection(ya, yb, w_out_bf, x2, mod3, seq_len):
    m, d = x2.shape
    tm, tn = 512, 1024
    tiles_per_seq = seq_len // tm
    return pl.pallas_call(
        _out_kernel,
        grid=(m // tm, d // tn),
        in_specs=[
            pl.BlockSpec((tm, D_GROUP), lambda i, j: (i, 0)),
            pl.BlockSpec((tm, D_GROUP), lambda i, j: (i, 0)),
            pl.BlockSpec((D_GROUP, tn), lambda i, j: (0, j)),
            pl.BlockSpec((D_GROUP, tn), lambda i, j: (1, j)),
            pl.BlockSpec((tm, tn), lambda i, j: (i, j)),
            pl.BlockSpec((1, 3, tn), lambda i, j: (i // tiles_per_seq, 0, j)),
        ],
        out_specs=pl.BlockSpec((tm, tn), lambda i, j: (i, j)),
        out_shape=jax.ShapeDtypeStruct((m, d), F32),
        compiler_params=pltpu.CompilerParams(
            dimension_semantics=("arbitrary", "arbitrary"), vmem_limit_bytes=VMEM_LIMIT_BYTES),
        name="out_proj",
    )(ya, yb, w_out_bf, w_out_bf, x2, mod3)


def _layer(x, c, w_ada, b_ada, g_norm, w_in, q_norm_a, k_norm_a, k_norm_idx, q_norm_b, k_norm_b, w_out):
    bsz, seq_len, d = x.shape
    assert seq_len % MOBA_BLOCK == 0 and seq_len % 1024 == 0 and d % 1024 == 0
    x2 = x.reshape(bsz * seq_len, d)

    mod3 = _modulation(c, w_ada, b_ada).reshape(bsz, 3, d)

    o = 0
    cols = {}
    for name, width in (("qa", D_GROUP), ("ka", D_GROUP), ("va", D_GROUP), ("za", D_GROUP),
                        ("qidx", IDX_HEADS * IDX_DIM), ("kidx", IDX_DIM), ("widx", IDX_HEADS),
                        ("qb", D_GROUP), ("kb", D_GROUP), ("vb", D_GROUP), ("zb", D_GROUP)):
        cols[name] = w_in[:, o:o + width]
        o += width
    w_main = jnp.concatenate(
        [cols[k] for k in ("qa", "ka", "va", "za", "qidx", "qb", "kb", "vb", "zb")], axis=1).astype(BF16)
    w_tail = jnp.concatenate(
        [cols["kidx"], cols["kidx"], cols["widx"],
         jnp.zeros((d, LANES - IDX_HEADS), w_in.dtype)], axis=1).astype(BF16)
    ones = jnp.ones((HEAD_DIM,), F32)
    sm_scale = HEAD_DIM ** -0.5
    gains = jnp.stack([q_norm_a * sm_scale, k_norm_a, ones, ones, ones,
                       q_norm_b * sm_scale, k_norm_b, ones, ones]).reshape(N_SEGS, 1, HEAD_DIM)
    for h in range(N_HEADS):
        slope = _alibi_slope(h, N_HEADS)
        assert float(np.float32(slope).astype(jnp.bfloat16)) == slope, "ALiBi slopes must be exact in bf16"
    assert seq_len // POS_TILE <= 256 and POS_TILE <= 256, "positions must split into bf16-exact parts"
    kaug = _key_features(seq_len)
    gk = jnp.concatenate([k_norm_idx, k_norm_idx]).reshape(1, LANES)

    p, kk, wi = _projection(x2, mod3, g_norm.reshape(1, d), w_main, w_tail, gains, gk, seq_len)
    ya = _dsa_attention(p, kk, wi, kaug, bsz, seq_len)
    yb = _moba_attention(p, kaug, bsz, seq_len)
    out = _out_projection(ya, yb, w_out.astype(BF16), x2, mod3, seq_len)
    return out.reshape(bsz, seq_len, d)


def kernel(x, c, w_ada, b_ada, g_norm, w_in, q_norm_a, k_norm_a, k_norm_idx, q_norm_b, k_norm_b, w_out):
    for i in range(w_ada.shape[0]):
        x = _layer(x, c, w_ada[i], b_ada[i], g_norm[i], w_in[i], q_norm_a[i], k_norm_a[i],
                   k_norm_idx[i], q_norm_b[i], k_norm_b[i], w_out[i])
    return x
```

```python
import functools

import jax
import jax.numpy as jnp
import numpy as np
from jax import lax
from jax.experimental import pallas as pl
from jax.experimental.pallas import tpu as pltpu

F32 = jnp.float32
BF16 = jnp.bfloat16

HEAD_DIM = 128
N_HEADS = 8
D_GROUP = N_HEADS * HEAD_DIM
IDX_HEADS = 16
IDX_DIM = 64
DSA_TOPK_MAX = 256
MOBA_BLOCK = 256
MOBA_TOPK_MAX = 3
RMS_EPS = 1e-6
NEG_INF = -1e30
BIG = 1e30

LANES = 128
VMEM_LIMIT_BYTES = 56 * 1024 * 1024

SEG_QA, SEG_KA, SEG_VA, SEG_ZA, SEG_QIDX, SEG_QB, SEG_KB, SEG_VB, SEG_ZB = range(9)
N_SEGS = 9
BLOCKS_PER_SEG = D_GROUP // LANES


def _alibi_slope(h, n):
    return float(2.0 ** (-8.0 * (h + 1) / n))


def _dot_nt(a, b):
    return lax.dot_general(a, b, (((1,), (1,)), ((), ())), preferred_element_type=F32)


def _mod_kernel(c_ref, w_ref, b_ref, o_ref):
    c = c_ref[...]
    s = c * jax.nn.sigmoid(c)
    o_ref[...] = jnp.dot(s, w_ref[...], preferred_element_type=F32) + b_ref[...]


def _modulation(c, w_ada, b_ada):
    bsz, d = c.shape
    n = w_ada.shape[1]
    tn = 1024
    return pl.pallas_call(
        _mod_kernel,
        grid=(n // tn,),
        in_specs=[
            pl.BlockSpec((bsz, d), lambda j: (0, 0)),
            pl.BlockSpec((d, tn), lambda j: (0, j)),
            pl.BlockSpec((1, tn), lambda j: (0, j)),
        ],
        out_specs=pl.BlockSpec((bsz, tn), lambda j: (0, j)),
        out_shape=jax.ShapeDtypeStruct((bsz, n), F32),
        compiler_params=pltpu.CompilerParams(
            dimension_semantics=("arbitrary",), vmem_limit_bytes=VMEM_LIMIT_BYTES),
        name="adaln_mod",
    )(c, w_ada, b_ada.reshape(1, n))


PROJ_CHUNK = 2 * LANES


def _proj_kernel(x_ref, mod_ref, g_ref, w_ref, wt_ref, gain_ref, gk_ref,
                 p_ref, kk_ref, wi_ref, h_ref):
    j = pl.program_id(1)

    @pl.when(j == 0)
    def _():
        x = x_ref[...]
        ms = jnp.mean(x * x, axis=-1, keepdims=True)
        shift = mod_ref[0, 0:1, :]
        gs = g_ref[...] * (1.0 + mod_ref[0, 1:2, :])
        hb = (x * lax.rsqrt(ms + RMS_EPS) * gs + shift).astype(BF16)
        h_ref[...] = hb
        t = jnp.dot(hb, wt_ref[...], preferred_element_type=F32)
        tk = t[:, :LANES]
        kms = jnp.mean(tk * tk, axis=-1, keepdims=True)
        kk_ref[...] = (tk * lax.rsqrt(kms + RMS_EPS) * gk_ref[...]).astype(BF16)
        wi_ref[...] = t[:, LANES:]

    is_norm = (j == SEG_QA) | (j == SEG_KA) | (j == SEG_QB) | (j == SEG_KB)
    g = gain_ref[0]
    h = h_ref[...]
    for cc in range(D_GROUP // PROJ_CHUNK):
        acc = jnp.dot(h, w_ref[:, cc * PROJ_CHUNK:(cc + 1) * PROJ_CHUNK], preferred_element_type=F32)
        for c in range(PROJ_CHUNK // LANES):
            a = acc[:, c * LANES:(c + 1) * LANES]
            ms = jnp.mean(a * a, axis=-1, keepdims=True)
            f = jnp.where(is_norm, lax.rsqrt(ms + RMS_EPS), 1.0)
            p_ref[cc * (PROJ_CHUNK // LANES) + c] = (a * f * g).astype(BF16)


def _projection(x2, mod3, g_norm, w_main, w_tail, gains, gk, seq_len):
    m, d = x2.shape
    tm = 1024
    tiles_per_seq = seq_len // tm
    return pl.pallas_call(
        _proj_kernel,
        grid=(m // tm, N_SEGS),
        in_specs=[
            pl.BlockSpec((tm, d), lambda i, j: (i, 0)),
            pl.BlockSpec((1, 3, d), lambda i, j: (i // tiles_per_seq, 0, 0)),
            pl.BlockSpec((1, d), lambda i, j: (0, 0)),
            pl.BlockSpec((d, D_GROUP), lambda i, j: (0, j)),
            pl.BlockSpec((d, 2 * LANES), lambda i, j: (0, 0)),
            pl.BlockSpec((1, 1, LANES), lambda i, j: (j, 0, 0)),
            pl.BlockSpec((1, LANES), lambda i, j: (0, 0)),
        ],
        out_specs=[
            pl.BlockSpec((BLOCKS_PER_SEG, tm, LANES), lambda i, j: (j, i, 0)),
            pl.BlockSpec((tm, LANES), lambda i, j: (i, 0)),
            pl.BlockSpec((tm, LANES), lambda i, j: (i, 0)),
        ],
        out_shape=[
            jax.ShapeDtypeStruct((N_SEGS * BLOCKS_PER_SEG, m, LANES), BF16),
            jax.ShapeDtypeStruct((m, LANES), BF16),
            jax.ShapeDtypeStruct((m, LANES), F32),
        ],
        scratch_shapes=[pltpu.VMEM((tm, d), BF16)],
        compiler_params=pltpu.CompilerParams(
            dimension_semantics=("arbitrary", "arbitrary"), vmem_limit_bytes=VMEM_LIMIT_BYTES),
        name="in_proj",
    )(x2, mod3, g_norm, w_main, w_tail, gains, gk)


POS_TILE = MOBA_BLOCK
AUG_KLOC, AUG_KBLK, AUG_ONE_LOC, AUG_ONE_BLK, AUG_ONEHOT0 = 0, 1, 2, 3, 8


def _key_features(seq_len):
    pos = np.arange(seq_len)
    f = np.zeros((seq_len, LANES), np.float32)
    f[:, AUG_KLOC] = pos % POS_TILE
    f[:, AUG_KBLK] = pos // POS_TILE
    f[:, AUG_ONE_LOC] = 1.0
    f[:, AUG_ONE_BLK] = 1.0
    f[pos, AUG_ONEHOT0 + pos // POS_TILE] = 1.0
    return jnp.asarray(f, BF16)


def _alibi_query_features(slope, q_tile, tq):
    lane = lax.broadcasted_iota(jnp.int32, (tq, LANES), 1)
    row = lax.broadcasted_iota(jnp.int32, (tq, LANES), 0).astype(F32)
    f = jnp.where(lane == AUG_KLOC, slope, 0.0)
    f = jnp.where(lane == AUG_KBLK, slope * POS_TILE, f)
    f = jnp.where(lane == AUG_ONE_LOC, -slope * row, f)
    return jnp.where(lane == AUG_ONE_BLK, (-slope * POS_TILE) * q_tile.astype(F32), f)


def _fold_rows(x, op):
    r = x.reshape(x.shape[0] // 8, 8, x.shape[1])
    while r.shape[0] > 1:
        half = r.shape[0] // 2
        r = op(r[:half], r[half:])
    return r[0]


def _transpose_values(v_ref, vt_ref, n_tiles, tile):
    for h in range(N_HEADS):
        for j in range(n_tiles):
            vt_ref[h, j] = v_ref[h, j * tile:(j + 1) * tile, :].astype(F32).T.astype(BF16)


DSA_TQ = 256
DSA_TK = POS_TILE
SEARCH_STEPS = 4


def _dsa_kernel(q_ref, k_ref, v_ref, z_ref, qi_ref, kk_ref, wi_ref, kaug_ref, o_ref,
                vt_ref, qs_ref, sc_ref, t_ref, j_ref, qcat_ref, s_ref, acc_ref, *, top_k, seq_len):
    tq, tk = DSA_TQ, DSA_TK
    i = pl.program_id(1)
    t0 = i * tq
    nk = i + 1
    idx_scale = (IDX_DIM ** -0.5) * (IDX_HEADS ** -0.5)

    @pl.when(i == 0)
    def _():
        _transpose_values(v_ref, vt_ref, seq_len // tk, tk)

    lane = lax.broadcasted_iota(jnp.int32, (tq, LANES), 1)
    key_loc = lax.broadcasted_iota(jnp.int32, (tk, tq), 0).astype(F32)
    qry_loc = lax.broadcasted_iota(jnp.int32, (tk, tq), 1).astype(F32)

    for c in range(BLOCKS_PER_SEG):
        qp = qi_ref[c].astype(F32)
        qs_ref[2 * c] = jnp.where(lane < IDX_DIM, qp, 0.0).astype(BF16)
        qs_ref[2 * c + 1] = jnp.where(lane >= IDX_DIM, qp, 0.0).astype(BF16)
    w_t = (wi_ref[...] * idx_scale).T
    w_rows = [w_t[h:h + 1, :] for h in range(IDX_HEADS)]

    def score_tile(kt, carry):
        kk = kk_ref[pl.ds(pl.multiple_of(kt * tk, tk), tk), :]
        acc = jnp.zeros((tk, tq), F32)
        for h in range(IDX_HEADS):
            acc = acc + w_rows[h] * jnp.maximum(_dot_nt(kk, qs_ref[h]), 0.0)
        off = (kt * tk - t0).astype(F32)
        sc_ref[kt] = jnp.where((key_loc + off) <= qry_loc, acc, -jnp.inf)
        return carry

    lax.fori_loop(0, nk, score_tile, 0)

    t_ref[...] = jnp.full((1, tq), NEG_INF, F32)
    j_ref[...] = jnp.full((1, tq), -1.0, F32)

    def count(pred):
        def body(kt, acc):
            off = (kt * tk).astype(F32)
            return acc + _fold_rows(jnp.where(pred(sc_ref[kt], key_loc + off), 1.0, 0.0), jnp.add)
        return jnp.sum(lax.fori_loop(0, nk, body, jnp.zeros((8, tq), F32)), axis=0, keepdims=True)

    @pl.when(t0 + tq > top_k)
    def _():
        kf = float(top_k)
        qpos = lax.broadcasted_iota(jnp.int32, (1, tq), 1) + t0
        done0 = jnp.where(qpos + 1 <= top_k, 1.0, 0.0)

        def minmax(kt, carry):
            mn, mx = carry
            s = sc_ref[kt]
            mx = jnp.maximum(mx, _fold_rows(s, jnp.maximum))
            mn = jnp.minimum(mn, _fold_rows(jnp.where(s > 0.5 * NEG_INF, s, BIG), jnp.minimum))
            return mn, mx

        mn, mx = lax.fori_loop(0, nk, minmax,
                               (jnp.full((8, tq), BIG, F32), jnp.full((8, tq), NEG_INF, F32)))
        lo0 = jnp.min(mn, axis=0, keepdims=True)
        hi0 = jnp.max(mx, axis=0, keepdims=True)

        def step(x, degen, state):
            lo, hi, thr, done, tie = state
            c = count(lambda s, pos: s > x)
            active = done == 0.0
            live = jnp.logical_and(active, jnp.logical_not(degen))
            found = jnp.logical_and(live, c == kf)
            new_tie = jnp.logical_and(active, degen)
            thr = jnp.where(found, x, jnp.where(new_tie, hi, thr))
            tie = jnp.where(new_tie, 1.0, tie)
            done = jnp.where(jnp.logical_or(found, new_tie), 1.0, done)
            lo = jnp.where(jnp.logical_and(live, c > kf), x, lo)
            hi = jnp.where(jnp.logical_and(live, c < kf), x, hi)
            return lo, hi, thr, done, tie

        state0 = (lo0, hi0, jnp.full((1, tq), NEG_INF, F32), done0, jnp.zeros((1, tq), F32))
        state0 = step(lo0, jnp.zeros((1, tq), jnp.bool_), state0)

        def cond(carry):
            return carry[1] > 0.0

        def body(carry):
            state, _ = carry
            for _ in range(SEARCH_STEPS):
                lo, hi = state[0], state[1]
                mid = 0.5 * lo + 0.5 * hi
                degen = jnp.logical_or(mid <= lo, mid >= hi)
                state = step(mid, degen, state)
            return state, jnp.max(1.0 - state[3])

        (lo, hi, thr, done, tie), _ = lax.while_loop(cond, body, (state0, jnp.max(1.0 - state0[3])))
        t_ref[...] = thr

        @pl.when(jnp.max(tie) > 0.0)
        def _():
            need = kf - count(lambda s, pos: s > thr)
            n_bits = int(np.ceil(np.log2(seq_len + 1)))

            def jstep(_, carry):
                jlo, jhi = carry
                jmid = jnp.floor((jlo + jhi) * 0.5)
                c = count(lambda s, pos: jnp.logical_and(s == thr, pos <= jmid))
                ge = c >= need
                return jnp.where(ge, jlo, jmid), jnp.where(ge, jmid, jhi)

            _, jhi = lax.fori_loop(0, n_bits, jstep,
                                   (jnp.full((1, tq), -1.0, F32),
                                    jnp.full((1, tq), float(seq_len - 1), F32)))
            j_ref[...] = jnp.where(tie > 0.0, jhi, -1.0)

    def mask_tile(kt, carry):
        s = sc_ref[kt]
        thr = t_ref[...]
        off = (kt * tk).astype(F32)
        sel = jnp.logical_or(s > thr, jnp.logical_and(s == thr, (key_loc + off) <= j_ref[...]))
        sc_ref[kt] = jnp.where(sel, 0.0, NEG_INF)
        return carry

    lax.fori_loop(0, nk, mask_tile, 0)

    for h in range(N_HEADS):
        feat = _alibi_query_features(_alibi_slope(h, N_HEADS), i, tq)
        qcat_ref[h] = jnp.concatenate([q_ref[h], feat.astype(BF16)], axis=1)

    def pass_a(kt, mx):
        ks = pl.ds(pl.multiple_of(kt * tk, tk), tk)
        kaug = kaug_ref[ks, :]
        bias = sc_ref[kt]
        out = []
        for h in range(N_HEADS):
            s = _dot_nt(jnp.concatenate([k_ref[h, ks, :], kaug], axis=1), qcat_ref[h]) + bias
            s_ref[h, kt] = s
            out.append(jnp.maximum(mx[h], _fold_rows(s, jnp.maximum)))
        return tuple(out)

    mx = lax.fori_loop(0, nk, pass_a, tuple(jnp.full((8, tq), NEG_INF, F32) for _ in range(N_HEADS)))
    m = [jnp.max(mx[h], axis=0, keepdims=True) for h in range(N_HEADS)]

    acc_ref[...] = jnp.zeros(acc_ref.shape, F32)

    def pass_b(kt, ls):
        out = []
        for h in range(N_HEADS):
            p = jnp.exp(s_ref[h, kt] - m[h])
            out.append(ls[h] + _fold_rows(p, jnp.add))
            acc_ref[h] += jnp.dot(vt_ref[h, kt], p.astype(BF16), preferred_element_type=F32)
        return tuple(out)

    ls = lax.fori_loop(0, nk, pass_b, tuple(jnp.zeros((8, tq), F32) for _ in range(N_HEADS)))

    for h in range(N_HEADS):
        l = jnp.sum(ls[h], axis=0, keepdims=True)
        z = z_ref[h].astype(F32)
        o = (acc_ref[h] / l).T * (z * jax.nn.sigmoid(z))
        o_ref[:, h * HEAD_DIM:(h + 1) * HEAD_DIM] = o.astype(BF16)


def _dsa_attention(p, kk, wi, kaug, bsz, seq_len):
    tq, tk = DSA_TQ, DSA_TK
    nq = seq_len // tq
    nkt = seq_len // tk
    m = bsz * seq_len
    top_k = min(DSA_TOPK_MAX, seq_len // 4)
    seg = lambda s: (lambda b, i: (s, b * nq + i, 0))
    seg_full = lambda s: (lambda b, i: (s, b, 0))
    kern = functools.partial(_dsa_kernel, top_k=top_k, seq_len=seq_len)
    return pl.pallas_call(
        kern,
        grid=(bsz, nq),
        in_specs=[
            pl.BlockSpec((BLOCKS_PER_SEG, tq, LANES), seg(SEG_QA)),
            pl.BlockSpec((BLOCKS_PER_SEG, seq_len, LANES), seg_full(SEG_KA)),
            pl.BlockSpec((BLOCKS_PER_SEG, seq_len, LANES), seg_full(SEG_VA)),
            pl.BlockSpec((BLOCKS_PER_SEG, tq, LANES), seg(SEG_ZA)),
            pl.BlockSpec((BLOCKS_PER_SEG, tq, LANES), seg(SEG_QIDX)),
            pl.BlockSpec((seq_len, LANES), lambda b, i: (b, 0)),
            pl.BlockSpec((tq, LANES), lambda b, i: (b * nq + i, 0)),
            pl.BlockSpec((seq_len, LANES), lambda b, i: (0, 0)),
        ],
        out_specs=pl.BlockSpec((tq, D_GROUP), lambda b, i: (b * nq + i, 0)),
        out_shape=jax.ShapeDtypeStruct((m, D_GROUP), BF16),
        scratch_shapes=[
            pltpu.VMEM((N_HEADS, nkt, HEAD_DIM, tk), BF16),
            pltpu.VMEM((IDX_HEADS, tq, LANES), BF16),
            pltpu.VMEM((nkt, tk, tq), F32),
            pltpu.VMEM((1, tq), F32),
            pltpu.VMEM((1, tq), F32),
            pltpu.VMEM((N_HEADS, tq, 2 * LANES), BF16),
            pltpu.VMEM((N_HEADS, nkt, tk, tq), F32),
            pltpu.VMEM((N_HEADS, HEAD_DIM, tq), F32),
        ],
        compiler_params=pltpu.CompilerParams(
            dimension_semantics=("arbitrary", "arbitrary"), vmem_limit_bytes=VMEM_LIMIT_BYTES),
        name="dsa_attn",
    )(p, p, p, p, p, kk, wi, kaug)


def _moba_kernel(q_ref, k_ref, v_ref, z_ref, kaug_ref, o_ref,
                 kmean_ref, vt_ref, selb_ref, qcat_ref, s_ref, acc_ref, *, n_blocks):
    bs = MOBA_BLOCK
    n = pl.program_id(1)
    top_k = min(MOBA_TOPK_MAX, n_blocks - 1)
    nb_pad = kmean_ref.shape[1]

    @pl.when(n == 0)
    def _():
        kmean_ref[...] = jnp.zeros(kmean_ref.shape, F32)
        for h in range(N_HEADS):
            for j in range(n_blocks):
                kb = k_ref[h, j * bs:(j + 1) * bs, :].astype(F32)
                kmean_ref[h, j:j + 1, :] = jnp.mean(kb, axis=0, keepdims=True)
        _transpose_values(v_ref, vt_ref, n_blocks, bs)

    blk = lax.broadcasted_iota(jnp.int32, (nb_pad, bs), 0)
    past = blk < n

    selb_ref[...] = jnp.zeros(selb_ref.shape, F32)
    for h in range(N_HEADS):
        q = q_ref[h]
        g = _dot_nt(kmean_ref[h].astype(BF16), q)
        for j in range(n_blocks):
            gj = g[j:j + 1, :]
            beats = jnp.logical_or(g > gj, jnp.logical_and(g == gj, blk < j))
            rank = jnp.sum(jnp.where(jnp.logical_and(beats, past), 1.0, 0.0), axis=0, keepdims=True)
            dropped = jnp.logical_and(rank >= float(top_k), j < n)
            selb_ref[AUG_ONEHOT0 + j:AUG_ONEHOT0 + j + 1, :] = jnp.where(dropped, NEG_INF, 0.0)
        feat = _alibi_query_features(_alibi_slope(h, N_HEADS), n, bs) + selb_ref[...].T
        qcat_ref[h] = jnp.concatenate([q, feat.astype(BF16)], axis=1)

    key_loc = lax.broadcasted_iota(jnp.int32, (bs, bs), 0)
    qry_loc = lax.broadcasted_iota(jnp.int32, (bs, bs), 1)
    causal = key_loc <= qry_loc
    own = pl.ds(pl.multiple_of(n * bs, bs), bs)
    kaug_own = kaug_ref[own, :]
    mx = []
    for h in range(N_HEADS):
        s = _dot_nt(jnp.concatenate([k_ref[h, own, :], kaug_own], axis=1), qcat_ref[h])
        s = jnp.where(causal, s, NEG_INF)
        s_ref[h, n] = s
        mx.append(_fold_rows(s, jnp.maximum))

    def pass_a(j, mx):
        ks = pl.ds(pl.multiple_of(j * bs, bs), bs)
        kaug = kaug_ref[ks, :]
        out = []
        for h in range(N_HEADS):
            s = _dot_nt(jnp.concatenate([k_ref[h, ks, :], kaug], axis=1), qcat_ref[h])
            s_ref[h, j] = s
            out.append(jnp.maximum(mx[h], _fold_rows(s, jnp.maximum)))
        return tuple(out)

    mx = lax.fori_loop(0, n, pass_a, tuple(mx))
    m = [jnp.max(mx[h], axis=0, keepdims=True) for h in range(N_HEADS)]

    acc_ref[...] = jnp.zeros(acc_ref.shape, F32)

    def pass_b(j, ls):
        out = []
        for h in range(N_HEADS):
            p = jnp.exp(s_ref[h, j] - m[h])
            out.append(ls[h] + _fold_rows(p, jnp.add))
            acc_ref[h] += jnp.dot(vt_ref[h, j], p.astype(BF16), preferred_element_type=F32)
        return tuple(out)

    ls = lax.fori_loop(0, n + 1, pass_b, tuple(jnp.zeros((8, bs), F32) for _ in range(N_HEADS)))

    for h in range(N_HEADS):
        l = jnp.sum(ls[h], axis=0, keepdims=True)
        z = z_ref[h].astype(F32)
        o = (acc_ref[h] / l).T * (z * jax.nn.sigmoid(z))
        o_ref[:, h * HEAD_DIM:(h + 1) * HEAD_DIM] = o.astype(BF16)


def _moba_attention(p, kaug, bsz, seq_len):
    bs = MOBA_BLOCK
    nb = seq_len // bs
    m = bsz * seq_len
    seg = lambda s: (lambda b, i: (s, b * nb + i, 0))
    seg_full = lambda s: (lambda b, i: (s, b, 0))
    kern = functools.partial(_moba_kernel, n_blocks=nb)
    nb_pad = max(16, nb)
    assert AUG_ONEHOT0 + nb <= LANES
    return pl.pallas_call(
        kern,
        grid=(bsz, nb),
        in_specs=[
            pl.BlockSpec((BLOCKS_PER_SEG, bs, LANES), seg(SEG_QB)),
            pl.BlockSpec((BLOCKS_PER_SEG, seq_len, LANES), seg_full(SEG_KB)),
            pl.BlockSpec((BLOCKS_PER_SEG, seq_len, LANES), seg_full(SEG_VB)),
            pl.BlockSpec((BLOCKS_PER_SEG, bs, LANES), seg(SEG_ZB)),
            pl.BlockSpec((seq_len, LANES), lambda b, i: (0, 0)),
        ],
        out_specs=pl.BlockSpec((bs, D_GROUP), lambda b, i: (b * nb + i, 0)),
        out_shape=jax.ShapeDtypeStruct((m, D_GROUP), BF16),
        scratch_shapes=[
            pltpu.VMEM((N_HEADS, nb_pad, HEAD_DIM), F32),
            pltpu.VMEM((N_HEADS, nb, HEAD_DIM, bs), BF16),
            pltpu.VMEM((LANES, bs), F32),
            pltpu.VMEM((N_HEADS, bs, 2 * LANES), BF16),
            pltpu.VMEM((N_HEADS, nb, bs, bs), F32),
            pltpu.VMEM((N_HEADS, HEAD_DIM, bs), F32),
        ],
        compiler_params=pltpu.CompilerParams(
            dimension_semantics=("arbitrary", "arbitrary"), vmem_limit_bytes=VMEM_LIMIT_BYTES),
        name="moba_attn",
    )(p, p, p, p, kaug)


OUT_CHUNK = 512


def _out_kernel(ya_ref, yb_ref, wa_ref, wb_ref, x_ref, mod_ref, o_ref):
    ya = ya_ref[...]
    yb = yb_ref[...]
    for c in range(o_ref.shape[1] // OUT_CHUNK):
        cs = slice(c * OUT_CHUNK, (c + 1) * OUT_CHUNK)
        y = jnp.dot(ya, wa_ref[:, cs], preferred_element_type=F32)
        y = y + jnp.dot(yb, wb_ref[:, cs], preferred_element_type=F32)
        o_ref[:, cs] = x_ref[:, cs] + mod_ref[0, 2:3, cs] * y


def _out_projection(ya, yb, w_out_bf, x2, mod3, seq_len):
    m, d = x2.shape
    tm = 512
    tiles_per_seq = seq_len // tm
    return pl.pallas_call(
        _out_kernel,
        grid=(m // tm,),
        in_specs=[
            pl.BlockSpec((tm, D_GROUP), lambda i: (i, 0)),
            pl.BlockSpec((tm, D_GROUP), lambda i: (i, 0)),
            pl.BlockSpec((D_GROUP, d), lambda i: (0, 0)),
            pl.BlockSpec((D_GROUP, d), lambda i: (1, 0)),
            pl.BlockSpec((tm, d), lambda i: (i, 0)),
            pl.BlockSpec((1, 3, d), lambda i: (i // tiles_per_seq, 0, 0)),
        ],
        out_specs=pl.BlockSpec((tm, d), lambda i: (i, 0)),
        out_shape=jax.ShapeDtypeStruct((m, d), F32),
        compiler_params=pltpu.CompilerParams(
            dimension_semantics=("arbitrary",), vmem_limit_bytes=VMEM_LIMIT_BYTES),
        name="out_proj",
    )(ya, yb, w_out_bf, w_out_bf, x2, mod3)


def _layer(x, c, w_ada, b_ada, g_norm, w_in, q_norm_a, k_norm_a, k_norm_idx, q_norm_b, k_norm_b, w_out):
    bsz, seq_len, d = x.shape
    assert seq_len % MOBA_BLOCK == 0 and seq_len % 1024 == 0 and d % 1024 == 0
    x2 = x.reshape(bsz * seq_len, d)

    mod3 = _modulation(c, w_ada, b_ada).reshape(bsz, 3, d)

    n_a = 4 * D_GROUP + IDX_HEADS * IDX_DIM
    n_mid = IDX_DIM + IDX_HEADS
    assert w_in.shape[1] == n_a + n_mid + 4 * D_GROUP and n_a == 5 * D_GROUP
    w_main = jnp.concatenate([w_in[:, :n_a], w_in[:, n_a + n_mid:]], axis=1).astype(BF16)
    kidx_cols = w_in[:, n_a:n_a + IDX_DIM]
    w_tail = jnp.concatenate(
        [kidx_cols, kidx_cols, w_in[:, n_a + IDX_DIM:n_a + n_mid],
         jnp.zeros((d, LANES - IDX_HEADS), w_in.dtype)], axis=1).astype(BF16)
    ones = jnp.ones((HEAD_DIM,), F32)
    sm_scale = HEAD_DIM ** -0.5
    gains = jnp.stack([q_norm_a * sm_scale, k_norm_a, ones, ones, ones,
                       q_norm_b * sm_scale, k_norm_b, ones, ones]).reshape(N_SEGS, 1, HEAD_DIM)
    for h in range(N_HEADS):
        slope = _alibi_slope(h, N_HEADS)
        assert float(np.float32(slope).astype(jnp.bfloat16)) == slope, "ALiBi slopes must be exact in bf16"
    assert seq_len // POS_TILE <= 256 and POS_TILE <= 256, "positions must split into bf16-exact parts"
    kaug = _key_features(seq_len)
    gk = jnp.concatenate([k_norm_idx, k_norm_idx]).reshape(1, LANES)

    p, kk, wi = _projection(x2, mod3, g_norm.reshape(1, d), w_main, w_tail, gains, gk, seq_len)
    ya = _dsa_attention(p, kk, wi, kaug, bsz, seq_len)
    yb = _moba_attention(p, kaug, bsz, seq_len)
    out = _out_projection(ya, yb, w_out.astype(BF16), x2, mod3, seq_len)
    return out.reshape(bsz, seq_len, d)


def kernel(x, c, w_ada, b_ada, g_norm, w_in, q_norm_a, k_norm_a, k_norm_idx, q_norm_b, k_norm_b, w_out):
    for i in range(w_ada.shape[0]):
        x = _layer(x, c, w_ada[i], b_ada[i], g_norm[i], w_in[i], q_norm_a[i], k_norm_a[i],
                   k_norm_idx[i], q_norm_b[i], k_norm_b[i], w_out[i])
    return x
```

```python
import functools

import jax
import jax.numpy as jnp
import numpy as np
from jax import lax
from jax.experimental import pallas as pl
from jax.experimental.pallas import tpu as pltpu

F32 = jnp.float32
BF16 = jnp.bfloat16

HEAD_DIM = 128
N_HEADS = 8
D_GROUP = N_HEADS * HEAD_DIM
IDX_HEADS = 16
IDX_DIM = 64
DSA_TOPK_MAX = 256
MOBA_BLOCK = 256
MOBA_TOPK_MAX = 3
RMS_EPS = 1e-6
NEG_INF = -1e30
BIG = 1e30

LANES = 128
VMEM_LIMIT_BYTES = 56 * 1024 * 1024

SEG_QA, SEG_KA, SEG_VA, SEG_ZA, SEG_QIDX, SEG_QB, SEG_KB, SEG_VB, SEG_ZB = range(9)
N_SEGS = 9
BLOCKS_PER_SEG = D_GROUP // LANES


def _alibi_slope(h, n):
    return float(2.0 ** (-8.0 * (h + 1) / n))


def _dot_nt(a, b):
    return lax.dot_general(a, b, (((1,), (1,)), ((), ())), preferred_element_type=F32)


def _mod_kernel(c_ref, w_ref, b_ref, o_ref):
    c = c_ref[...]
    s = c * jax.nn.sigmoid(c)
    o_ref[...] = jnp.dot(s, w_ref[...], preferred_element_type=F32) + b_ref[...]


def _modulation(c, w_ada, b_ada):
    bsz, d = c.shape
    n = w_ada.shape[1]
    tn = 1024
    return pl.pallas_call(
        _mod_kernel,
        grid=(n // tn,),
        in_specs=[
            pl.BlockSpec((bsz, d), lambda j: (0, 0)),
            pl.BlockSpec((d, tn), lambda j: (0, j)),
            pl.BlockSpec((1, tn), lambda j: (0, j)),
        ],
        out_specs=pl.BlockSpec((bsz, tn), lambda j: (0, j)),
        out_shape=jax.ShapeDtypeStruct((bsz, n), F32),
        compiler_params=pltpu.CompilerParams(
            dimension_semantics=("arbitrary",), vmem_limit_bytes=VMEM_LIMIT_BYTES),
        name="adaln_mod",
    )(c, w_ada, b_ada.reshape(1, n))


PROJ_CHUNK = 2 * LANES


def _proj_kernel(x_ref, mod_ref, g_ref, w_ref, wt_ref, gain_ref, gk_ref,
                 p_ref, kk_ref, wi_ref, h_ref):
    j = pl.program_id(1)

    @pl.when(j == 0)
    def _():
        x = x_ref[...]
        ms = jnp.mean(x * x, axis=-1, keepdims=True)
        shift = mod_ref[0, 0:1, :]
        gs = g_ref[...] * (1.0 + mod_ref[0, 1:2, :])
        hb = (x * lax.rsqrt(ms + RMS_EPS) * gs + shift).astype(BF16)
        h_ref[...] = hb
        t = jnp.dot(hb, wt_ref[...], preferred_element_type=F32)
        tk = t[:, :LANES]
        kms = jnp.mean(tk * tk, axis=-1, keepdims=True)
        kk_ref[...] = (tk * lax.rsqrt(kms + RMS_EPS) * gk_ref[...]).astype(BF16)
        wi_ref[...] = t[:, LANES:]

    is_norm = (j == SEG_QA) | (j == SEG_KA) | (j == SEG_QB) | (j == SEG_KB)
    g = gain_ref[0]
    h = h_ref[...]
    for cc in range(D_GROUP // PROJ_CHUNK):
        acc = jnp.dot(h, w_ref[:, cc * PROJ_CHUNK:(cc + 1) * PROJ_CHUNK], preferred_element_type=F32)
        for c in range(PROJ_CHUNK // LANES):
            a = acc[:, c * LANES:(c + 1) * LANES]
            ms = jnp.mean(a * a, axis=-1, keepdims=True)
            f = jnp.where(is_norm, lax.rsqrt(ms + RMS_EPS), 1.0)
            p_ref[cc * (PROJ_CHUNK // LANES) + c] = (a * f * g).astype(BF16)


def _projection(x2, mod3, g_norm, w_main, w_tail, gains, gk, seq_len):
    m, d = x2.shape
    tm = 1024
    tiles_per_seq = seq_len // tm
    return pl.pallas_call(
        _proj_kernel,
        grid=(m // tm, N_SEGS),
        in_specs=[
            pl.BlockSpec((tm, d), lambda i, j: (i, 0)),
            pl.BlockSpec((1, 3, d), lambda i, j: (i // tiles_per_seq, 0, 0)),
            pl.BlockSpec((1, d), lambda i, j: (0, 0)),
            pl.BlockSpec((d, D_GROUP), lambda i, j: (0, j)),
            pl.BlockSpec((d, 2 * LANES), lambda i, j: (0, 0)),
            pl.BlockSpec((1, 1, LANES), lambda i, j: (j, 0, 0)),
            pl.BlockSpec((1, LANES), lambda i, j: (0, 0)),
        ],
        out_specs=[
            pl.BlockSpec((BLOCKS_PER_SEG, tm, LANES), lambda i, j: (j, i, 0)),
            pl.BlockSpec((tm, LANES), lambda i, j: (i, 0)),
            pl.BlockSpec((tm, LANES), lambda i, j: (i, 0)),
        ],
        out_shape=[
            jax.ShapeDtypeStruct((N_SEGS * BLOCKS_PER_SEG, m, LANES), BF16),
            jax.ShapeDtypeStruct((m, LANES), BF16),
            jax.ShapeDtypeStruct((m, LANES), F32),
        ],
        scratch_shapes=[pltpu.VMEM((tm, d), BF16)],
        compiler_params=pltpu.CompilerParams(
            dimension_semantics=("arbitrary", "arbitrary"), vmem_limit_bytes=VMEM_LIMIT_BYTES),
        name="in_proj",
    )(x2, mod3, g_norm, w_main, w_tail, gains, gk)


POS_TILE = MOBA_BLOCK
AUG_KLOC, AUG_KBLK, AUG_ONE_LOC, AUG_ONE_BLK, AUG_ONEHOT0 = 0, 1, 2, 3, 8


def _key_features(seq_len):
    pos = np.arange(seq_len)
    f = np.zeros((seq_len, LANES), np.float32)
    f[:, AUG_KLOC] = pos % POS_TILE
    f[:, AUG_KBLK] = pos // POS_TILE
    f[:, AUG_ONE_LOC] = 1.0
    f[:, AUG_ONE_BLK] = 1.0
    f[pos, AUG_ONEHOT0 + pos // POS_TILE] = 1.0
    return jnp.asarray(f, BF16)


def _alibi_query_features(slope, q_tile, tq):
    lane = lax.broadcasted_iota(jnp.int32, (tq, LANES), 1)
    row = lax.broadcasted_iota(jnp.int32, (tq, LANES), 0).astype(F32)
    f = jnp.where(lane == AUG_KLOC, slope, 0.0)
    f = jnp.where(lane == AUG_KBLK, slope * POS_TILE, f)
    f = jnp.where(lane == AUG_ONE_LOC, -slope * row, f)
    return jnp.where(lane == AUG_ONE_BLK, (-slope * POS_TILE) * q_tile.astype(F32), f)


def _fold_rows(x, op):
    r = x.reshape(x.shape[0] // 8, 8, x.shape[1])
    while r.shape[0] > 1:
        half = r.shape[0] // 2
        r = op(r[:half], r[half:])
    return r[0]


TILE_CHUNKS = (4, 2, 1)


def _for_tile_chunks(n_tiles, fn, chunks=TILE_CHUNKS):
    big = chunks[0]
    shift = big.bit_length() - 1

    def many(c, carry):
        fn(c * big, big)
        return carry

    n_big = lax.shift_right_logical(n_tiles, shift)
    lax.fori_loop(0, n_big, many, 0)
    done = n_big * big
    for size in chunks[1:]:
        has = (n_tiles & size) != 0

        @pl.when(has)
        def _(done=done, size=size):
            fn(done, size)

        done = done + (n_tiles & size)


def _softmax_pass_a(n_tiles, tile, score_rows, s_ref, mx_ref):
    tq = s_ref.shape[-1]

    def pass_a(first, count):
        for h in range(N_HEADS):
            s = score_rows(h, first, count)
            s_ref[h, pl.ds(first, count)] = s.reshape(count, tile, tq)
            mx_ref[h] = jnp.maximum(mx_ref[h], _fold_rows(s, jnp.maximum))

    _for_tile_chunks(n_tiles, pass_a)
    return [jnp.max(mx_ref[h], axis=0, keepdims=True) for h in range(N_HEADS)]


def _softmax_pass_b(n_tiles, tile, m, value_cols, s_ref, ls_ref, acc_ref):
    tq = acc_ref.shape[-1]
    ls_ref[...] = jnp.zeros(ls_ref.shape, F32)
    acc_ref[...] = jnp.zeros(acc_ref.shape, F32)

    def pass_b(first, count):
        for h in range(N_HEADS):
            p = jnp.exp(s_ref[h, pl.ds(first, count)].reshape(count * tile, tq) - m[h])
            ls_ref[h] += _fold_rows(p, jnp.add)
            vt = jnp.concatenate([value_cols(h, first + u) for u in range(count)], axis=1)
            acc_ref[h] += jnp.dot(vt, p.astype(BF16), preferred_element_type=F32)

    _for_tile_chunks(n_tiles, pass_b)
    return [jnp.sum(ls_ref[h], axis=0, keepdims=True) for h in range(N_HEADS)]


def _gated_output(h, z_ref, l, acc_ref, o_ref):
    z = z_ref[h].astype(F32)
    o = (acc_ref[h] / l).T * (z * jax.nn.sigmoid(z))
    o_ref[:, h * HEAD_DIM:(h + 1) * HEAD_DIM] = o.astype(BF16)


def _transpose_values(v_ref, vt_ref, n_tiles, tile):
    for h in range(N_HEADS):
        for j in range(n_tiles):
            vt_ref[h, j] = v_ref[h, j * tile:(j + 1) * tile, :].astype(F32).T.astype(BF16)


DSA_TQ = 256
DSA_TK = POS_TILE
SEARCH_STEPS = 4


def _dsa_kernel(q_ref, k_ref, v_ref, z_ref, qi_ref, kk_ref, wi_ref, kaug_ref, o_ref,
                vt_ref, qs_ref, sc_ref, t_ref, j_ref, qcat_ref, s_ref, mx_ref, ls_ref, acc_ref, *, top_k, seq_len):
    tq, tk = DSA_TQ, DSA_TK
    i = pl.program_id(1)
    t0 = i * tq
    nk = i + 1
    idx_scale = (IDX_DIM ** -0.5) * (IDX_HEADS ** -0.5)

    @pl.when(i == 0)
    def _():
        _transpose_values(v_ref, vt_ref, seq_len // tk, tk)

    lane = lax.broadcasted_iota(jnp.int32, (tq, LANES), 1)
    key_loc = lax.broadcasted_iota(jnp.int32, (tk, tq), 0).astype(F32)
    qry_loc = lax.broadcasted_iota(jnp.int32, (tk, tq), 1).astype(F32)

    for c in range(BLOCKS_PER_SEG):
        qp = qi_ref[c].astype(F32)
        qs_ref[2 * c] = jnp.where(lane < IDX_DIM, qp, 0.0).astype(BF16)
        qs_ref[2 * c + 1] = jnp.where(lane >= IDX_DIM, qp, 0.0).astype(BF16)
    w_t = (wi_ref[...] * idx_scale).T
    w_rows = [w_t[h:h + 1, :] for h in range(IDX_HEADS)]

    def score_tile(kt, carry):
        kk = kk_ref[pl.ds(pl.multiple_of(kt * tk, tk), tk), :]
        acc = jnp.zeros((tk, tq), F32)
        for h in range(IDX_HEADS):
            acc = acc + w_rows[h] * jnp.maximum(_dot_nt(kk, qs_ref[h]), 0.0)
        off = (kt * tk - t0).astype(F32)
        sc_ref[kt] = jnp.where((key_loc + off) <= qry_loc, acc, -jnp.inf)
        return carry

    lax.fori_loop(0, nk, score_tile, 0)

    t_ref[...] = jnp.full((1, tq), NEG_INF, F32)
    j_ref[...] = jnp.full((1, tq), -1.0, F32)

    def count(pred):
        def body(kt, acc):
            off = (kt * tk).astype(F32)
            return acc + _fold_rows(jnp.where(pred(sc_ref[kt], key_loc + off), 1.0, 0.0), jnp.add)
        return jnp.sum(lax.fori_loop(0, nk, body, jnp.zeros((8, tq), F32)), axis=0, keepdims=True)

    @pl.when(t0 + tq > top_k)
    def _():
        kf = float(top_k)
        qpos = lax.broadcasted_iota(jnp.int32, (1, tq), 1) + t0
        done0 = jnp.where(qpos + 1 <= top_k, 1.0, 0.0)

        def minmax(kt, carry):
            mn, mx = carry
            s = sc_ref[kt]
            mx = jnp.maximum(mx, _fold_rows(s, jnp.maximum))
            mn = jnp.minimum(mn, _fold_rows(jnp.where(s > 0.5 * NEG_INF, s, BIG), jnp.minimum))
            return mn, mx

        mn, mx = lax.fori_loop(0, nk, minmax,
                               (jnp.full((8, tq), BIG, F32), jnp.full((8, tq), NEG_INF, F32)))
        lo0 = jnp.min(mn, axis=0, keepdims=True)
        hi0 = jnp.max(mx, axis=0, keepdims=True)

        def step(x, degen, state):
            lo, hi, thr, done, tie = state
            c = count(lambda s, pos: s > x)
            active = done == 0.0
            live = jnp.logical_and(active, jnp.logical_not(degen))
            found = jnp.logical_and(live, c == kf)
            new_tie = jnp.logical_and(active, degen)
            thr = jnp.where(found, x, jnp.where(new_tie, hi, thr))
            tie = jnp.where(new_tie, 1.0, tie)
            done = jnp.where(jnp.logical_or(found, new_tie), 1.0, done)
            lo = jnp.where(jnp.logical_and(live, c > kf), x, lo)
            hi = jnp.where(jnp.logical_and(live, c < kf), x, hi)
            return lo, hi, thr, done, tie

        state0 = (lo0, hi0, jnp.full((1, tq), NEG_INF, F32), done0, jnp.zeros((1, tq), F32))
        state0 = step(lo0, jnp.zeros((1, tq), jnp.bool_), state0)

        def cond(carry):
            return carry[1] > 0.0

        def body(carry):
            state, _ = carry
            for _ in range(SEARCH_STEPS):
                lo, hi = state[0], state[1]
                mid = 0.5 * lo + 0.5 * hi
                degen = jnp.logical_or(mid <= lo, mid >= hi)
                state = step(mid, degen, state)
            return state, jnp.max(1.0 - state[3])

        (lo, hi, thr, done, tie), _ = lax.while_loop(cond, body, (state0, jnp.max(1.0 - state0[3])))
        t_ref[...] = thr

        @pl.when(jnp.max(tie) > 0.0)
        def _():
            need = kf - count(lambda s, pos: s > thr)
            n_bits = int(np.ceil(np.log2(seq_len + 1)))

            def jstep(_, carry):
                jlo, jhi = carry
                jmid = jnp.floor((jlo + jhi) * 0.5)
                c = count(lambda s, pos: jnp.logical_and(s == thr, pos <= jmid))
                ge = c >= need
                return jnp.where(ge, jlo, jmid), jnp.where(ge, jmid, jhi)

            _, jhi = lax.fori_loop(0, n_bits, jstep,
                                   (jnp.full((1, tq), -1.0, F32),
                                    jnp.full((1, tq), float(seq_len - 1), F32)))
            j_ref[...] = jnp.where(tie > 0.0, jhi, -1.0)

    def mask_tile(kt, carry):
        s = sc_ref[kt]
        thr = t_ref[...]
        off = (kt * tk).astype(F32)
        sel = jnp.logical_or(s > thr, jnp.logical_and(s == thr, (key_loc + off) <= j_ref[...]))
        sc_ref[kt] = jnp.where(sel, 0.0, NEG_INF)
        return carry

    lax.fori_loop(0, nk, mask_tile, 0)

    for h in range(N_HEADS):
        feat = _alibi_query_features(_alibi_slope(h, N_HEADS), i, tq)
        qcat_ref[h] = jnp.concatenate([q_ref[h], feat.astype(BF16)], axis=1)

    def score_rows(h, first, count):
        ks = pl.ds(pl.multiple_of(first * tk, tk), count * tk)
        kcat = jnp.concatenate([k_ref[h, ks, :], kaug_ref[ks, :]], axis=1)
        return _dot_nt(kcat, qcat_ref[h]) + sc_ref[pl.ds(first, count)].reshape(count * tk, tq)

    mx_ref[...] = jnp.full(mx_ref.shape, NEG_INF, F32)
    value_cols = lambda h, kt: vt_ref[h, kt]
    m = _softmax_pass_a(nk, tk, score_rows, s_ref, mx_ref)
    lsum = _softmax_pass_b(nk, tk, m, value_cols, s_ref, ls_ref, acc_ref)
    for h in range(N_HEADS):
        _gated_output(h, z_ref, lsum[h], acc_ref, o_ref)


def _dsa_attention(p, kk, wi, kaug, bsz, seq_len):
    tq, tk = DSA_TQ, DSA_TK
    nq = seq_len // tq
    nkt = seq_len // tk
    m = bsz * seq_len
    top_k = min(DSA_TOPK_MAX, seq_len // 4)
    seg = lambda s: (lambda b, i: (s, b * nq + i, 0))
    seg_full = lambda s: (lambda b, i: (s, b, 0))
    kern = functools.partial(_dsa_kernel, top_k=top_k, seq_len=seq_len)
    return pl.pallas_call(
        kern,
        grid=(bsz, nq),
        in_specs=[
            pl.BlockSpec((BLOCKS_PER_SEG, tq, LANES), seg(SEG_QA)),
            pl.BlockSpec((BLOCKS_PER_SEG, seq_len, LANES), seg_full(SEG_KA)),
            pl.BlockSpec((BLOCKS_PER_SEG, seq_len, LANES), seg_full(SEG_VA)),
            pl.BlockSpec((BLOCKS_PER_SEG, tq, LANES), seg(SEG_ZA)),
            pl.BlockSpec((BLOCKS_PER_SEG, tq, LANES), seg(SEG_QIDX)),
            pl.BlockSpec((seq_len, LANES), lambda b, i: (b, 0)),
            pl.BlockSpec((tq, LANES), lambda b, i: (b * nq + i, 0)),
            pl.BlockSpec((seq_len, LANES), lambda b, i: (0, 0)),
        ],
        out_specs=pl.BlockSpec((tq, D_GROUP), lambda b, i: (b * nq + i, 0)),
        out_shape=jax.ShapeDtypeStruct((m, D_GROUP), BF16),
        scratch_shapes=[
            pltpu.VMEM((N_HEADS, nkt, HEAD_DIM, tk), BF16),
            pltpu.VMEM((IDX_HEADS, tq, LANES), BF16),
            pltpu.VMEM((nkt, tk, tq), F32),
            pltpu.VMEM((1, tq), F32),
            pltpu.VMEM((1, tq), F32),
            pltpu.VMEM((N_HEADS, tq, 2 * LANES), BF16),
            pltpu.VMEM((N_HEADS, nkt, tk, tq), F32),
            pltpu.VMEM((N_HEADS, 8, tq), F32),
            pltpu.VMEM((N_HEADS, 8, tq), F32),
            pltpu.VMEM((N_HEADS, HEAD_DIM, tq), F32),
        ],
        compiler_params=pltpu.CompilerParams(
            dimension_semantics=("arbitrary", "arbitrary"), vmem_limit_bytes=VMEM_LIMIT_BYTES),
        name="dsa_attn",
    )(p, p, p, p, p, kk, wi, kaug)


def _moba_kernel(q_ref, k_ref, v_ref, z_ref, kaug_ref, o_ref,
                 kmean_ref, vt_ref, selb_ref, qcat_ref, s_ref, mx_ref, ls_ref, acc_ref, *, n_blocks):
    bs = MOBA_BLOCK
    n = pl.program_id(1)
    top_k = min(MOBA_TOPK_MAX, n_blocks - 1)
    nb_pad = kmean_ref.shape[1]

    @pl.when(n == 0)
    def _():
        kmean_ref[...] = jnp.zeros(kmean_ref.shape, F32)
        for h in range(N_HEADS):
            for j in range(n_blocks):
                kb = k_ref[h, j * bs:(j + 1) * bs, :].astype(F32)
                kmean_ref[h, j:j + 1, :] = jnp.mean(kb, axis=0, keepdims=True)
        _transpose_values(v_ref, vt_ref, n_blocks, bs)

    blk = lax.broadcasted_iota(jnp.int32, (nb_pad, bs), 0)
    past = blk < n

    selb_ref[...] = jnp.zeros(selb_ref.shape, F32)
    for h in range(N_HEADS):
        q = q_ref[h]
        g = _dot_nt(kmean_ref[h].astype(BF16), q)
        for j in range(n_blocks):
            gj = g[j:j + 1, :]
            beats = jnp.logical_or(g > gj, jnp.logical_and(g == gj, blk < j))
            rank = jnp.sum(jnp.where(jnp.logical_and(beats, past), 1.0, 0.0), axis=0, keepdims=True)
            dropped = jnp.logical_and(rank >= float(top_k), j < n)
            selb_ref[AUG_ONEHOT0 + j:AUG_ONEHOT0 + j + 1, :] = jnp.where(dropped, NEG_INF, 0.0)
        feat = _alibi_query_features(_alibi_slope(h, N_HEADS), n, bs) + selb_ref[...].T
        qcat_ref[h] = jnp.concatenate([q, feat.astype(BF16)], axis=1)

    key_loc = lax.broadcasted_iota(jnp.int32, (bs, bs), 0)
    qry_loc = lax.broadcasted_iota(jnp.int32, (bs, bs), 1)
    causal = key_loc <= qry_loc

    def score_rows(h, first, count):
        ks = pl.ds(pl.multiple_of(first * bs, bs), count * bs)
        kcat = jnp.concatenate([k_ref[h, ks, :], kaug_ref[ks, :]], axis=1)
        return _dot_nt(kcat, qcat_ref[h])

    for h in range(N_HEADS):
        s = jnp.where(causal, score_rows(h, n, 1), NEG_INF)
        s_ref[h, n] = s
        mx_ref[h] = _fold_rows(s, jnp.maximum)
    value_cols = lambda h, j: vt_ref[h, j]
    m = _softmax_pass_a(n, bs, score_rows, s_ref, mx_ref)
    lsum = _softmax_pass_b(n + 1, bs, m, value_cols, s_ref, ls_ref, acc_ref)
    for h in range(N_HEADS):
        _gated_output(h, z_ref, lsum[h], acc_ref, o_ref)


def _moba_attention(p, kaug, bsz, seq_len):
    bs = MOBA_BLOCK
    nb = seq_len // bs
    m = bsz * seq_len
    seg = lambda s: (lambda b, i: (s, b * nb + i, 0))
    seg_full = lambda s: (lambda b, i: (s, b, 0))
    kern = functools.partial(_moba_kernel, n_blocks=nb)
    nb_pad = max(16, nb)
    assert AUG_ONEHOT0 + nb <= LANES
    return pl.pallas_call(
        kern,
        grid=(bsz, nb),
        in_specs=[
            pl.BlockSpec((BLOCKS_PER_SEG, bs, LANES), seg(SEG_QB)),
            pl.BlockSpec((BLOCKS_PER_SEG, seq_len, LANES), seg_full(SEG_KB)),
            pl.BlockSpec((BLOCKS_PER_SEG, seq_len, LANES), seg_full(SEG_VB)),
            pl.BlockSpec((BLOCKS_PER_SEG, bs, LANES), seg(SEG_ZB)),
            pl.BlockSpec((seq_len, LANES), lambda b, i: (0, 0)),
        ],
        out_specs=pl.BlockSpec((bs, D_GROUP), lambda b, i: (b * nb + i, 0)),
        out_shape=jax.ShapeDtypeStruct((m, D_GROUP), BF16),
        scratch_shapes=[
            pltpu.VMEM((N_HEADS, nb_pad, HEAD_DIM), F32),
            pltpu.VMEM((N_HEADS, nb, HEAD_DIM, bs), BF16),
            pltpu.VMEM((LANES, bs), F32),
            pltpu.VMEM((N_HEADS, bs, 2 * LANES), BF16),
            pltpu.VMEM((N_HEADS, nb, bs, bs), F32),
            pltpu.VMEM((N_HEADS, 8, bs), F32),
            pltpu.VMEM((N_HEADS, 8, bs), F32),
            pltpu.VMEM((N_HEADS, HEAD_DIM, bs), F32),
        ],
        compiler_params=pltpu.CompilerParams(
            dimension_semantics=("arbitrary", "arbitrary"), vmem_limit_bytes=VMEM_LIMIT_BYTES),
        name="moba_attn",
    )(p, p, p, p, kaug)


OUT_CHUNK = 512


def _out_kernel(ya_ref, yb_ref, wa_ref, wb_ref, x_ref, mod_ref, o_ref):
    ya = ya_ref[...]
    yb = yb_ref[...]
    for c in range(o_ref.shape[1] // OUT_CHUNK):
        cs = slice(c * OUT_CHUNK, (c + 1) * OUT_CHUNK)
        y = jnp.dot(ya, wa_ref[:, cs], preferred_element_type=F32)
        y = y + jnp.dot(yb, wb_ref[:, cs], preferred_element_type=F32)
        o_ref[:, cs] = x_ref[:, cs] + mod_ref[0, 2:3, cs] * y


def _out_projection(ya, yb, w_out_bf, x2, mod3, seq_len):
    m, d = x2.shape
    tm = 512
    tiles_per_seq = seq_len // tm
    return pl.pallas_call(
        _out_kernel,
        grid=(m // tm,),
        in_specs=[
            pl.BlockSpec((tm, D_GROUP), lambda i: (i, 0)),
            pl.BlockSpec((tm, D_GROUP), lambda i: (i, 0)),
            pl.BlockSpec((D_GROUP, d), lambda i: (0, 0)),
            pl.BlockSpec((D_GROUP, d), lambda i: (1, 0)),
            pl.BlockSpec((tm, d), lambda i: (i, 0)),
            pl.BlockSpec((1, 3, d), lambda i: (i // tiles_per_seq, 0, 0)),
        ],
        out_specs=pl.BlockSpec((tm, d), lambda i: (i, 0)),
        out_shape=jax.ShapeDtypeStruct((m, d), F32),
        compiler_params=pltpu.CompilerParams(
            dimension_semantics=("arbitrary",), vmem_limit_bytes=VMEM_LIMIT_BYTES),
        name="out_proj",
    )(ya, yb, w_out_bf, w_out_bf, x2, mod3)


def _layer(x, c, w_ada, b_ada, g_norm, w_in, q_norm_a, k_norm_a, k_norm_idx, q_norm_b, k_norm_b, w_out):
    bsz, seq_len, d = x.shape
    assert seq_len % MOBA_BLOCK == 0 and seq_len % 1024 == 0 and d % 1024 == 0
    x2 = x.reshape(bsz * seq_len, d)

    mod3 = _modulation(c, w_ada, b_ada).reshape(bsz, 3, d)

    n_a = 4 * D_GROUP + IDX_HEADS * IDX_DIM
    n_mid = IDX_DIM + IDX_HEADS
    assert w_in.shape[1] == n_a + n_mid + 4 * D_GROUP and n_a == 5 * D_GROUP
    w_main = jnp.concatenate([w_in[:, :n_a], w_in[:, n_a + n_mid:]], axis=1).astype(BF16)
    kidx_cols = w_in[:, n_a:n_a + IDX_DIM]
    w_tail = jnp.concatenate(
        [kidx_cols, kidx_cols, w_in[:, n_a + IDX_DIM:n_a + n_mid],
         jnp.zeros((d, LANES - IDX_HEADS), w_in.dtype)], axis=1).astype(BF16)
    ones = jnp.ones((HEAD_DIM,), F32)
    sm_scale = HEAD_DIM ** -0.5
    gains = jnp.stack([q_norm_a * sm_scale, k_norm_a, ones, ones, ones,
                       q_norm_b * sm_scale, k_norm_b, ones, ones]).reshape(N_SEGS, 1, HEAD_DIM)
    for h in range(N_HEADS):
        slope = _alibi_slope(h, N_HEADS)
        assert float(np.float32(slope).astype(jnp.bfloat16)) == slope, "ALiBi slopes must be exact in bf16"
    assert seq_len // POS_TILE <= 256 and POS_TILE <= 256, "positions must split into bf16-exact parts"
    kaug = _key_features(seq_len)
    gk = jnp.concatenate([k_norm_idx, k_norm_idx]).reshape(1, LANES)

    p, kk, wi = _projection(x2, mod3, g_norm.reshape(1, d), w_main, w_tail, gains, gk, seq_len)
    ya = _dsa_attention(p, kk, wi, kaug, bsz, seq_len)
    yb = _moba_attention(p, kaug, bsz, seq_len)
    out = _out_projection(ya, yb, w_out.astype(BF16), x2, mod3, seq_len)
    return out.reshape(bsz, seq_len, d)


def kernel(x, c, w_ada, b_ada, g_norm, w_in, q_norm_a, k_norm_a, k_norm_idx, q_norm_b, k_norm_b, w_out):
    for i in range(w_ada.shape[0]):
        x = _layer(x, c, w_ada[i], b_ada[i], g_norm[i], w_in[i], q_norm_a[i], k_norm_a[i],
                   k_norm_idx[i], q_norm_b[i], k_norm_b[i], w_out[i])
    return x
```

```python
import functools

import jax
import jax.numpy as jnp
import numpy as np
from jax import lax
from jax.experimental import pallas as pl
from jax.experimental.pallas import tpu as pltpu

F32 = jnp.float32
BF16 = jnp.bfloat16

HEAD_DIM = 128
N_HEADS = 8
D_GROUP = N_HEADS * HEAD_DIM
IDX_HEADS = 16
IDX_DIM = 64
DSA_TOPK_MAX = 256
MOBA_BLOCK = 256
MOBA_TOPK_MAX = 3
RMS_EPS = 1e-6
NEG_INF = -1e30
BIG = 1e30

LANES = 128
VMEM_LIMIT_BYTES = 56 * 1024 * 1024

SEG_QA, SEG_KA, SEG_VA, SEG_ZA, SEG_QIDX, SEG_QB, SEG_KB, SEG_VB, SEG_ZB = range(9)
N_SEGS = 9
BLOCKS_PER_SEG = D_GROUP // LANES


def _alibi_slope(h, n):
    return float(2.0 ** (-8.0 * (h + 1) / n))


def _dot_nt(a, b):
    return lax.dot_general(a, b, (((1,), (1,)), ((), ())), preferred_element_type=F32)


def _mod_kernel(c_ref, w_ref, b_ref, o_ref):
    c = c_ref[...]
    s = c * jax.nn.sigmoid(c)
    o_ref[...] = jnp.dot(s, w_ref[...], preferred_element_type=F32) + b_ref[...]


def _modulation(c, w_ada, b_ada):
    bsz, d = c.shape
    n = w_ada.shape[1]
    tn = 1024
    return pl.pallas_call(
        _mod_kernel,
        grid=(n // tn,),
        in_specs=[
            pl.BlockSpec((bsz, d), lambda j: (0, 0)),
            pl.BlockSpec((d, tn), lambda j: (0, j)),
            pl.BlockSpec((1, tn), lambda j: (0, j)),
        ],
        out_specs=pl.BlockSpec((bsz, tn), lambda j: (0, j)),
        out_shape=jax.ShapeDtypeStruct((bsz, n), F32),
        compiler_params=pltpu.CompilerParams(
            dimension_semantics=("arbitrary",), vmem_limit_bytes=VMEM_LIMIT_BYTES),
        name="adaln_mod",
    )(c, w_ada, b_ada.reshape(1, n))


PROJ_CHUNK = 2 * LANES


def _proj_kernel(x_ref, mod_ref, g_ref, w_ref, wt_ref, gain_ref, gk_ref,
                 p_ref, kk_ref, wi_ref, h_ref):
    j = pl.program_id(1)

    @pl.when(j == 0)
    def _():
        x = x_ref[...]
        ms = jnp.mean(x * x, axis=-1, keepdims=True)
        shift = mod_ref[0, 0:1, :]
        gs = g_ref[...] * (1.0 + mod_ref[0, 1:2, :])
        hb = (x * lax.rsqrt(ms + RMS_EPS) * gs + shift).astype(BF16)
        h_ref[...] = hb
        t = jnp.dot(hb, wt_ref[...], preferred_element_type=F32)
        tk = t[:, :LANES]
        kms = jnp.mean(tk * tk, axis=-1, keepdims=True)
        kk_ref[...] = (tk * lax.rsqrt(kms + RMS_EPS) * gk_ref[...]).astype(BF16)
        wi_ref[...] = t[:, LANES:]

    is_norm = (j == SEG_QA) | (j == SEG_KA) | (j == SEG_QB) | (j == SEG_KB)
    g = gain_ref[0]
    h = h_ref[...]
    for cc in range(D_GROUP // PROJ_CHUNK):
        acc = jnp.dot(h, w_ref[:, cc * PROJ_CHUNK:(cc + 1) * PROJ_CHUNK], preferred_element_type=F32)
        for c in range(PROJ_CHUNK // LANES):
            a = acc[:, c * LANES:(c + 1) * LANES]
            ms = jnp.mean(a * a, axis=-1, keepdims=True)
            f = jnp.where(is_norm, lax.rsqrt(ms + RMS_EPS), 1.0)
            p_ref[cc * (PROJ_CHUNK // LANES) + c] = (a * f * g).astype(BF16)


def _projection(x2, mod3, g_norm, w_main, w_tail, gains, gk, seq_len):
    m, d = x2.shape
    tm = 1024
    tiles_per_seq = seq_len // tm
    return pl.pallas_call(
        _proj_kernel,
        grid=(m // tm, N_SEGS),
        in_specs=[
            pl.BlockSpec((tm, d), lambda i, j: (i, 0)),
            pl.BlockSpec((1, 3, d), lambda i, j: (i // tiles_per_seq, 0, 0)),
            pl.BlockSpec((1, d), lambda i, j: (0, 0)),
            pl.BlockSpec((d, D_GROUP), lambda i, j: (0, j)),
            pl.BlockSpec((d, 2 * LANES), lambda i, j: (0, 0)),
            pl.BlockSpec((1, 1, LANES), lambda i, j: (j, 0, 0)),
            pl.BlockSpec((1, LANES), lambda i, j: (0, 0)),
        ],
        out_specs=[
            pl.BlockSpec((BLOCKS_PER_SEG, tm, LANES), lambda i, j: (j, i, 0)),
            pl.BlockSpec((tm, LANES), lambda i, j: (i, 0)),
            pl.BlockSpec((tm, LANES), lambda i, j: (i, 0)),
        ],
        out_shape=[
            jax.ShapeDtypeStruct((N_SEGS * BLOCKS_PER_SEG, m, LANES), BF16),
            jax.ShapeDtypeStruct((m, LANES), BF16),
            jax.ShapeDtypeStruct((m, LANES), F32),
        ],
        scratch_shapes=[pltpu.VMEM((tm, d), BF16)],
        compiler_params=pltpu.CompilerParams(
            dimension_semantics=("arbitrary", "arbitrary"), vmem_limit_bytes=VMEM_LIMIT_BYTES),
        name="in_proj",
    )(x2, mod3, g_norm, w_main, w_tail, gains, gk)


POS_TILE = MOBA_BLOCK
AUG_KLOC, AUG_KBLK, AUG_ONE_LOC, AUG_ONE_BLK, AUG_ONEHOT0 = 0, 1, 2, 3, 8


def _key_features(seq_len):
    pos = np.arange(seq_len)
    f = np.zeros((seq_len, LANES), np.float32)
    f[:, AUG_KLOC] = pos % POS_TILE
    f[:, AUG_KBLK] = pos // POS_TILE
    f[:, AUG_ONE_LOC] = 1.0
    f[:, AUG_ONE_BLK] = 1.0
    f[pos, AUG_ONEHOT0 + pos // POS_TILE] = 1.0
    return jnp.asarray(f, BF16)


def _alibi_query_features(slope, q_tile, tq):
    lane = lax.broadcasted_iota(jnp.int32, (tq, LANES), 1)
    row = lax.broadcasted_iota(jnp.int32, (tq, LANES), 0).astype(F32)
    f = jnp.where(lane == AUG_KLOC, slope, 0.0)
    f = jnp.where(lane == AUG_KBLK, slope * POS_TILE, f)
    f = jnp.where(lane == AUG_ONE_LOC, -slope * row, f)
    return jnp.where(lane == AUG_ONE_BLK, (-slope * POS_TILE) * q_tile.astype(F32), f)


def _fold_rows(x, op):
    r = x.reshape(x.shape[0] // 8, 8, x.shape[1])
    while r.shape[0] > 1:
        half = r.shape[0] // 2
        r = op(r[:half], r[half:])
    return r[0]


TILE_CHUNKS = (4, 2, 1)


def _for_tile_chunks(n_tiles, fn, chunks=TILE_CHUNKS):
    big = chunks[0]
    shift = big.bit_length() - 1

    def many(c, carry):
        fn(c * big, big)
        return carry

    n_big = lax.shift_right_logical(n_tiles, shift)
    lax.fori_loop(0, n_big, many, 0)
    done = n_big * big
    for size in chunks[1:]:
        has = (n_tiles & size) != 0

        @pl.when(has)
        def _(done=done, size=size):
            fn(done, size)

        done = done + (n_tiles & size)


def _softmax_pass_a(n_tiles, tile, score_rows, s_ref, mx_ref):
    tq = s_ref.shape[-1]

    def pass_a(first, count):
        for h in range(N_HEADS):
            s = score_rows(h, first, count)
            s_ref[h, pl.ds(first, count)] = s.reshape(count, tile, tq)
            mx_ref[h] = jnp.maximum(mx_ref[h], _fold_rows(s, jnp.maximum))

    _for_tile_chunks(n_tiles, pass_a)
    return [jnp.max(mx_ref[h], axis=0, keepdims=True) for h in range(N_HEADS)]


def _softmax_pass_b(n_tiles, tile, m, value_cols, s_ref, ls_ref, acc_ref):
    tq = acc_ref.shape[-1]
    ls_ref[...] = jnp.zeros(ls_ref.shape, F32)
    acc_ref[...] = jnp.zeros(acc_ref.shape, F32)

    def pass_b(first, count):
        for h in range(N_HEADS):
            p = jnp.exp(s_ref[h, pl.ds(first, count)].reshape(count * tile, tq) - m[h])
            ls_ref[h] += _fold_rows(p, jnp.add)
            vt = jnp.concatenate([value_cols(h, first + u) for u in range(count)], axis=1)
            acc_ref[h] += jnp.dot(vt, p.astype(BF16), preferred_element_type=F32)

    _for_tile_chunks(n_tiles, pass_b)
    return [jnp.sum(ls_ref[h], axis=0, keepdims=True) for h in range(N_HEADS)]


def _gated_output(h, z_ref, l, acc_ref, o_ref):
    z = z_ref[h].astype(F32)
    o = (acc_ref[h] / l).T * (z * jax.nn.sigmoid(z))
    o_ref[:, h * HEAD_DIM:(h + 1) * HEAD_DIM] = o.astype(BF16)


def _transpose_values(v_ref, vt_ref, n_tiles, tile):
    for h in range(N_HEADS):
        for j in range(n_tiles):
            vt_ref[h, j] = v_ref[h, j * tile:(j + 1) * tile, :].astype(F32).T.astype(BF16)


DSA_TQ = 256
DSA_TK = POS_TILE
SEARCH_STEPS = 4


def _dsa_kernel(q_ref, k_ref, v_ref, z_ref, qi_ref, kk_ref, wi_ref, kaug_ref, o_ref,
                vt_ref, qs_ref, sc_ref, t_ref, j_ref, qcat_ref, s_ref, mx_ref, ls_ref, acc_ref, *, top_k, seq_len):
    tq, tk = DSA_TQ, DSA_TK
    i = pl.program_id(1)
    t0 = i * tq
    nk = i + 1
    idx_scale = (IDX_DIM ** -0.5) * (IDX_HEADS ** -0.5)

    @pl.when(i == 0)
    def _():
        _transpose_values(v_ref, vt_ref, seq_len // tk, tk)

    lane = lax.broadcasted_iota(jnp.int32, (tq, LANES), 1)
    key_loc = lax.broadcasted_iota(jnp.int32, (tk, tq), 0).astype(F32)
    qry_loc = lax.broadcasted_iota(jnp.int32, (tk, tq), 1).astype(F32)

    for c in range(BLOCKS_PER_SEG):
        qp = qi_ref[c].astype(F32)
        qs_ref[2 * c] = jnp.where(lane < IDX_DIM, qp, 0.0).astype(BF16)
        qs_ref[2 * c + 1] = jnp.where(lane >= IDX_DIM, qp, 0.0).astype(BF16)
    w_t = (wi_ref[...] * idx_scale).T
    w_rows = [w_t[h:h + 1, :] for h in range(IDX_HEADS)]

    def score_tiles(first, count):
        for u in range(count):
            kt = first + u
            kk = kk_ref[pl.ds(pl.multiple_of(kt * tk, tk), tk), :]
            acc = jnp.zeros((tk, tq), F32)
            for h in range(IDX_HEADS):
                acc = acc + w_rows[h] * jnp.maximum(_dot_nt(kk, qs_ref[h]), 0.0)
            off = (kt * tk - t0).astype(F32)
            sc_ref[kt] = jnp.where((key_loc + off) <= qry_loc, acc, -jnp.inf)

    _for_tile_chunks(nk, score_tiles)

    t_ref[...] = jnp.full((1, tq), NEG_INF, F32)
    j_ref[...] = jnp.full((1, tq), -1.0, F32)

    def count(pred):
        def body(kt, acc):
            off = (kt * tk).astype(F32)
            return acc + _fold_rows(jnp.where(pred(sc_ref[kt], key_loc + off), 1.0, 0.0), jnp.add)
        return jnp.sum(lax.fori_loop(0, nk, body, jnp.zeros((8, tq), F32)), axis=0, keepdims=True)

    @pl.when(t0 + tq > top_k)
    def _():
        kf = float(top_k)
        qpos = lax.broadcasted_iota(jnp.int32, (1, tq), 1) + t0
        done0 = jnp.where(qpos + 1 <= top_k, 1.0, 0.0)

        def minmax(kt, carry):
            mn, mx = carry
            s = sc_ref[kt]
            mx = jnp.maximum(mx, _fold_rows(s, jnp.maximum))
            mn = jnp.minimum(mn, _fold_rows(jnp.where(s > 0.5 * NEG_INF, s, BIG), jnp.minimum))
            return mn, mx

        mn, mx = lax.fori_loop(0, nk, minmax,
                               (jnp.full((8, tq), BIG, F32), jnp.full((8, tq), NEG_INF, F32)))
        lo0 = jnp.min(mn, axis=0, keepdims=True)
        hi0 = jnp.max(mx, axis=0, keepdims=True)

        def step(x, degen, state):
            lo, hi, thr, done, tie = state
            c = count(lambda s, pos: s > x)
            active = done == 0.0
            live = jnp.logical_and(active, jnp.logical_not(degen))
            found = jnp.logical_and(live, c == kf)
            new_tie = jnp.logical_and(active, degen)
            thr = jnp.where(found, x, jnp.where(new_tie, hi, thr))
            tie = jnp.where(new_tie, 1.0, tie)
            done = jnp.where(jnp.logical_or(found, new_tie), 1.0, done)
            lo = jnp.where(jnp.logical_and(live, c > kf), x, lo)
            hi = jnp.where(jnp.logical_and(live, c < kf), x, hi)
            return lo, hi, thr, done, tie

        state0 = (lo0, hi0, jnp.full((1, tq), NEG_INF, F32), done0, jnp.zeros((1, tq), F32))
        state0 = step(lo0, jnp.zeros((1, tq), jnp.bool_), state0)

        def cond(carry):
            return carry[1] > 0.0

        def body(carry):
            state, _ = carry
            for _ in range(SEARCH_STEPS):
                lo, hi = state[0], state[1]
                mid = 0.5 * lo + 0.5 * hi
                degen = jnp.logical_or(mid <= lo, mid >= hi)
                state = step(mid, degen, state)
            return state, jnp.max(1.0 - state[3])

        (lo, hi, thr, done, tie), _ = lax.while_loop(cond, body, (state0, jnp.max(1.0 - state0[3])))
        t_ref[...] = thr

        @pl.when(jnp.max(tie) > 0.0)
        def _():
            need = kf - count(lambda s, pos: s > thr)
            n_bits = int(np.ceil(np.log2(seq_len + 1)))

            def jstep(_, carry):
                jlo, jhi = carry
                jmid = jnp.floor((jlo + jhi) * 0.5)
                c = count(lambda s, pos: jnp.logical_and(s == thr, pos <= jmid))
                ge = c >= need
                return jnp.where(ge, jlo, jmid), jnp.where(ge, jmid, jhi)

            _, jhi = lax.fori_loop(0, n_bits, jstep,
                                   (jnp.full((1, tq), -1.0, F32),
                                    jnp.full((1, tq), float(seq_len - 1), F32)))
            j_ref[...] = jnp.where(tie > 0.0, jhi, -1.0)

    def mask_tile(kt, carry):
        s = sc_ref[kt]
        thr = t_ref[...]
        off = (kt * tk).astype(F32)
        sel = jnp.logical_or(s > thr, jnp.logical_and(s == thr, (key_loc + off) <= j_ref[...]))
        sc_ref[kt] = jnp.where(sel, 0.0, NEG_INF)
        return carry

    lax.fori_loop(0, nk, mask_tile, 0)

    for h in range(N_HEADS):
        feat = _alibi_query_features(_alibi_slope(h, N_HEADS), i, tq)
        qcat_ref[h] = jnp.concatenate([q_ref[h], feat.astype(BF16)], axis=1)

    def score_rows(h, first, count):
        ks = pl.ds(pl.multiple_of(first * tk, tk), count * tk)
        kcat = jnp.concatenate([k_ref[h, ks, :], kaug_ref[ks, :]], axis=1)
        return _dot_nt(kcat, qcat_ref[h]) + sc_ref[pl.ds(first, count)].reshape(count * tk, tq)

    mx_ref[...] = jnp.full(mx_ref.shape, NEG_INF, F32)
    value_cols = lambda h, kt: vt_ref[h, kt]
    m = _softmax_pass_a(nk, tk, score_rows, s_ref, mx_ref)
    lsum = _softmax_pass_b(nk, tk, m, value_cols, s_ref, ls_ref, acc_ref)
    for h in range(N_HEADS):
        _gated_output(h, z_ref, lsum[h], acc_ref, o_ref)


def _dsa_attention(p, kk, wi, kaug, bsz, seq_len):
    tq, tk = DSA_TQ, DSA_TK
    nq = seq_len // tq
    nkt = seq_len // tk
    m = bsz * seq_len
    top_k = min(DSA_TOPK_MAX, seq_len // 4)
    seg = lambda s: (lambda b, i: (s, b * nq + i, 0))
    seg_full = lambda s: (lambda b, i: (s, b, 0))
    kern = functools.partial(_dsa_kernel, top_k=top_k, seq_len=seq_len)
    return pl.pallas_call(
        kern,
        grid=(bsz, nq),
        in_specs=[
            pl.BlockSpec((BLOCKS_PER_SEG, tq, LANES), seg(SEG_QA)),
            pl.BlockSpec((BLOCKS_PER_SEG, seq_len, LANES), seg_full(SEG_KA)),
            pl.BlockSpec((BLOCKS_PER_SEG, seq_len, LANES), seg_full(SEG_VA)),
            pl.BlockSpec((BLOCKS_PER_SEG, tq, LANES), seg(SEG_ZA)),
            pl.BlockSpec((BLOCKS_PER_SEG, tq, LANES), seg(SEG_QIDX)),
            pl.BlockSpec((seq_len, LANES), lambda b, i: (b, 0)),
            pl.BlockSpec((tq, LANES), lambda b, i: (b * nq + i, 0)),
            pl.BlockSpec((seq_len, LANES), lambda b, i: (0, 0)),
        ],
        out_specs=pl.BlockSpec((tq, D_GROUP), lambda b, i: (b * nq + i, 0)),
        out_shape=jax.ShapeDtypeStruct((m, D_GROUP), BF16),
        scratch_shapes=[
            pltpu.VMEM((N_HEADS, nkt, HEAD_DIM, tk), BF16),
            pltpu.VMEM((IDX_HEADS, tq, LANES), BF16),
            pltpu.VMEM((nkt, tk, tq), F32),
            pltpu.VMEM((1, tq), F32),
            pltpu.VMEM((1, tq), F32),
            pltpu.VMEM((N_HEADS, tq, 2 * LANES), BF16),
            pltpu.VMEM((N_HEADS, nkt, tk, tq), F32),
            pltpu.VMEM((N_HEADS, 8, tq), F32),
            pltpu.VMEM((N_HEADS, 8, tq), F32),
            pltpu.VMEM((N_HEADS, HEAD_DIM, tq), F32),
        ],
        compiler_params=pltpu.CompilerParams(
            dimension_semantics=("arbitrary", "arbitrary"), vmem_limit_bytes=VMEM_LIMIT_BYTES),
        name="dsa_attn",
    )(p, p, p, p, p, kk, wi, kaug)


def _moba_kernel(q_ref, k_ref, v_ref, z_ref, kaug_ref, o_ref,
                 kmean_ref, vt_ref, selb_ref, qcat_ref, s_ref, mx_ref, ls_ref, acc_ref, *, n_blocks):
    bs = MOBA_BLOCK
    n = pl.program_id(1)
    top_k = min(MOBA_TOPK_MAX, n_blocks - 1)
    nb_pad = kmean_ref.shape[1]

    @pl.when(n == 0)
    def _():
        kmean_ref[...] = jnp.zeros(kmean_ref.shape, F32)
        for h in range(N_HEADS):
            for j in range(n_blocks):
                kb = k_ref[h, j * bs:(j + 1) * bs, :].astype(F32)
                kmean_ref[h, j:j + 1, :] = jnp.mean(kb, axis=0, keepdims=True)
        _transpose_values(v_ref, vt_ref, n_blocks, bs)

    blk = lax.broadcasted_iota(jnp.int32, (nb_pad, bs), 0)
    past = blk < n

    selb_ref[...] = jnp.zeros(selb_ref.shape, F32)
    for h in range(N_HEADS):
        q = q_ref[h]
        g = _dot_nt(kmean_ref[h].astype(BF16), q)
        for j in range(n_blocks):
            gj = g[j:j + 1, :]
            beats = jnp.logical_or(g > gj, jnp.logical_and(g == gj, blk < j))
            rank = jnp.sum(jnp.where(jnp.logical_and(beats, past), 1.0, 0.0), axis=0, keepdims=True)
            dropped = jnp.logical_and(rank >= float(top_k), j < n)
            selb_ref[AUG_ONEHOT0 + j:AUG_ONEHOT0 + j + 1, :] = jnp.where(dropped, NEG_INF, 0.0)
        feat = _alibi_query_features(_alibi_slope(h, N_HEADS), n, bs) + selb_ref[...].T
        qcat_ref[h] = jnp.concatenate([q, feat.astype(BF16)], axis=1)

    key_loc = lax.broadcasted_iota(jnp.int32, (bs, bs), 0)
    qry_loc = lax.broadcasted_iota(jnp.int32, (bs, bs), 1)
    causal = key_loc <= qry_loc

    def score_rows(h, first, count):
        ks = pl.ds(pl.multiple_of(first * bs, bs), count * bs)
        kcat = jnp.concatenate([k_ref[h, ks, :], kaug_ref[ks, :]], axis=1)
        return _dot_nt(kcat, qcat_ref[h])

    for h in range(N_HEADS):
        s = jnp.where(causal, score_rows(h, n, 1), NEG_INF)
        s_ref[h, n] = s
        mx_ref[h] = _fold_rows(s, jnp.maximum)
    value_cols = lambda h, j: vt_ref[h, j]
    m = _softmax_pass_a(n, bs, score_rows, s_ref, mx_ref)
    lsum = _softmax_pass_b(n + 1, bs, m, value_cols, s_ref, ls_ref, acc_ref)
    for h in range(N_HEADS):
        _gated_output(h, z_ref, lsum[h], acc_ref, o_ref)


def _moba_attention(p, kaug, bsz, seq_len):
    bs = MOBA_BLOCK
    nb = seq_len // bs
    m = bsz * seq_len
    seg = lambda s: (lambda b, i: (s, b * nb + i, 0))
    seg_full = lambda s: (lambda b, i: (s, b, 0))
    kern = functools.partial(_moba_kernel, n_blocks=nb)
    nb_pad = max(16, nb)
    assert AUG_ONEHOT0 + nb <= LANES
    return pl.pallas_call(
        kern,
        grid=(bsz, nb),
        in_specs=[
            pl.BlockSpec((BLOCKS_PER_SEG, bs, LANES), seg(SEG_QB)),
            pl.BlockSpec((BLOCKS_PER_SEG, seq_len, LANES), seg_full(SEG_KB)),
            pl.BlockSpec((BLOCKS_PER_SEG, seq_len, LANES), seg_full(SEG_VB)),
            pl.BlockSpec((BLOCKS_PER_SEG, bs, LANES), seg(SEG_ZB)),
            pl.BlockSpec((seq_len, LANES), lambda b, i: (0, 0)),
        ],
        out_specs=pl.BlockSpec((bs, D_GROUP), lambda b, i: (b * nb + i, 0)),
        out_shape=jax.ShapeDtypeStruct((m, D_GROUP), BF16),
        scratch_shapes=[
            pltpu.VMEM((N_HEADS, nb_pad, HEAD_DIM), F32),
            pltpu.VMEM((N_HEADS, nb, HEAD_DIM, bs), BF16),
            pltpu.VMEM((LANES, bs), F32),
            pltpu.VMEM((N_HEADS, bs, 2 * LANES), BF16),
            pltpu.VMEM((N_HEADS, nb, bs, bs), F32),
            pltpu.VMEM((N_HEADS, 8, bs), F32),
            pltpu.VMEM((N_HEADS, 8, bs), F32),
            pltpu.VMEM((N_HEADS, HEAD_DIM, bs), F32),
        ],
        compiler_params=pltpu.CompilerParams(
            dimension_semantics=("arbitrary", "arbitrary"), vmem_limit_bytes=VMEM_LIMIT_BYTES),
        name="moba_attn",
    )(p, p, p, p, kaug)


OUT_CHUNK = 512


def _out_kernel(ya_ref, yb_ref, wa_ref, wb_ref, x_ref, mod_ref, o_ref):
    ya = ya_ref[...]
    yb = yb_ref[...]
    for c in range(o_ref.shape[1] // OUT_CHUNK):
        cs = slice(c * OUT_CHUNK, (c + 1) * OUT_CHUNK)
        y = jnp.dot(ya, wa_ref[:, cs], preferred_element_type=F32)
        y = y + jnp.dot(yb, wb_ref[:, cs], preferred_element_type=F32)
        o_ref[:, cs] = x_ref[:, cs] + mod_ref[0, 2:3, cs] * y


def _out_projection(ya, yb, w_out_bf, x2, mod3, seq_len):
    m, d = x2.shape
    tm = 512
    tiles_per_seq = seq_len // tm
    return pl.pallas_call(
        _out_kernel,
        grid=(m // tm,),
        in_specs=[
            pl.BlockSpec((tm, D_GROUP), lambda i: (i, 0)),
            pl.BlockSpec((tm, D_GROUP), lambda i: (i, 0)),
            pl.BlockSpec((D_GROUP, d), lambda i: (0, 0)),
            pl.BlockSpec((D_GROUP, d), lambda i: (1, 0)),
            pl.BlockSpec((tm, d), lambda i: (i, 0)),
            pl.BlockSpec((1, 3, d), lambda i: (i // tiles_per_seq, 0, 0)),
        ],
        out_specs=pl.BlockSpec((tm, d), lambda i: (i, 0)),
        out_shape=jax.ShapeDtypeStruct((m, d), F32),
        compiler_params=pltpu.CompilerParams(
            dimension_semantics=("arbitrary",), vmem_limit_bytes=VMEM_LIMIT_BYTES),
        name="out_proj",
    )(ya, yb, w_out_bf, w_out_bf, x2, mod3)


def _relayout_kernel(w_ref, main_ref, tail_ref, *, n_a, n_mid):
    rows = w_ref.shape[0]
    main_ref[:, :n_a] = w_ref[:, :n_a].astype(BF16)
    main_ref[:, n_a:] = w_ref[:, n_a + n_mid:].astype(BF16)
    kidx = w_ref[:, n_a:n_a + IDX_DIM]
    widx = w_ref[:, n_a + IDX_DIM:n_a + n_mid]
    tail = jnp.concatenate([kidx, kidx, widx, jnp.zeros((rows, LANES - IDX_HEADS), F32)], axis=1)
    tail_ref[...] = tail.astype(BF16)


def _relayout_weights(w_in):
    d, n_in = w_in.shape
    n_a = 4 * D_GROUP + IDX_HEADS * IDX_DIM
    n_mid = IDX_DIM + IDX_HEADS
    assert n_in == n_a + n_mid + 4 * D_GROUP and n_a == 5 * D_GROUP
    n_main = n_in - n_mid
    tr = 128
    return pl.pallas_call(
        functools.partial(_relayout_kernel, n_a=n_a, n_mid=n_mid),
        grid=(d // tr,),
        in_specs=[pl.BlockSpec((tr, n_in), lambda r: (r, 0))],
        out_specs=[
            pl.BlockSpec((tr, n_main), lambda r: (r, 0)),
            pl.BlockSpec((tr, 2 * LANES), lambda r: (r, 0)),
        ],
        out_shape=[
            jax.ShapeDtypeStruct((d, n_main), BF16),
            jax.ShapeDtypeStruct((d, 2 * LANES), BF16),
        ],
        compiler_params=pltpu.CompilerParams(
            dimension_semantics=("arbitrary",), vmem_limit_bytes=VMEM_LIMIT_BYTES),
        name="w_relayout",
    )(w_in)


def _layer(x, c, w_ada, b_ada, g_norm, w_in, q_norm_a, k_norm_a, k_norm_idx, q_norm_b, k_norm_b, w_out):
    bsz, seq_len, d = x.shape
    assert seq_len % MOBA_BLOCK == 0 and seq_len % 1024 == 0 and d % 1024 == 0
    x2 = x.reshape(bsz * seq_len, d)

    mod3 = _modulation(c, w_ada, b_ada).reshape(bsz, 3, d)

    w_main, w_tail = _relayout_weights(w_in)
    ones = jnp.ones((HEAD_DIM,), F32)
    sm_scale = HEAD_DIM ** -0.5
    gains = jnp.stack([q_norm_a * sm_scale, k_norm_a, ones, ones, ones,
                       q_norm_b * sm_scale, k_norm_b, ones, ones]).reshape(N_SEGS, 1, HEAD_DIM)
    for h in range(N_HEADS):
        slope = _alibi_slope(h, N_HEADS)
        assert float(np.float32(slope).astype(jnp.bfloat16)) == slope, "ALiBi slopes must be exact in bf16"
    assert seq_len // POS_TILE <= 256 and POS_TILE <= 256, "positions must split into bf16-exact parts"
    kaug = _key_features(seq_len)
    gk = jnp.concatenate([k_norm_idx, k_norm_idx]).reshape(1, LANES)

    p, kk, wi = _projection(x2, mod3, g_norm.reshape(1, d), w_main, w_tail, gains, gk, seq_len)
    ya = _dsa_attention(p, kk, wi, kaug, bsz, seq_len)
    yb = _moba_attention(p, kaug, bsz, seq_len)
    out = _out_projection(ya, yb, w_out.astype(BF16), x2, mod3, seq_len)
    return out.reshape(bsz, seq_len, d)


def kernel(x, c, w_ada, b_ada, g_norm, w_in, q_norm_a, k_norm_a, k_norm_idx, q_norm_b, k_norm_b, w_out):
    for i in range(w_ada.shape[0]):
        x = _layer(x, c, w_ada[i], b_ada[i], g_norm[i], w_in[i], q_norm_a[i], k_norm_a[i],
                   k_norm_idx[i], q_norm_b[i], k_norm_b[i], w_out[i])
    return x
```

```python
import functools

import jax
import jax.numpy as jnp
import numpy as np
from jax import lax
from jax.experimental import pallas as pl
from jax.experimental.pallas import tpu as pltpu

F32 = jnp.float32
BF16 = jnp.bfloat16

HEAD_DIM = 128
N_HEADS = 8
D_GROUP = N_HEADS * HEAD_DIM
IDX_HEADS = 16
IDX_DIM = 64
DSA_TOPK_MAX = 256
MOBA_BLOCK = 256
MOBA_TOPK_MAX = 3
RMS_EPS = 1e-6
NEG_INF = -1e30
BIG = 1e30

LANES = 128
VMEM_LIMIT_BYTES = 56 * 1024 * 1024

SEG_QA, SEG_KA, SEG_VA, SEG_ZA, SEG_QIDX, SEG_QB, SEG_KB, SEG_VB, SEG_ZB = range(9)
N_SEGS = 9
BLOCKS_PER_SEG = D_GROUP // LANES


def _alibi_slope(h, n):
    return float(2.0 ** (-8.0 * (h + 1) / n))


def _dot_nt(a, b):
    return lax.dot_general(a, b, (((1,), (1,)), ((), ())), preferred_element_type=F32)


def _mod_kernel(c_ref, w_ref, b_ref, o_ref):
    c = c_ref[...]
    s = c * jax.nn.sigmoid(c)
    o_ref[...] = jnp.dot(s, w_ref[...], preferred_element_type=F32) + b_ref[...]


def _modulation(c, w_ada, b_ada):
    bsz, d = c.shape
    n = w_ada.shape[1]
    tn = 1024
    return pl.pallas_call(
        _mod_kernel,
        grid=(n // tn,),
        in_specs=[
            pl.BlockSpec((bsz, d), lambda j: (0, 0)),
            pl.BlockSpec((d, tn), lambda j: (0, j)),
            pl.BlockSpec((1, tn), lambda j: (0, j)),
        ],
        out_specs=pl.BlockSpec((bsz, tn), lambda j: (0, j)),
        out_shape=jax.ShapeDtypeStruct((bsz, n), F32),
        compiler_params=pltpu.CompilerParams(
            dimension_semantics=("arbitrary",), vmem_limit_bytes=VMEM_LIMIT_BYTES),
        name="adaln_mod",
    )(c, w_ada, b_ada.reshape(1, n))


PROJ_CHUNK = 2 * LANES


def _proj_kernel(x_ref, mod_ref, g_ref, w_ref, wt_ref, gain_ref, gk_ref,
                 p_ref, kk_ref, wi_ref, h_ref):
    j = pl.program_id(1)

    @pl.when(j == 0)
    def _():
        x = x_ref[...]
        ms = jnp.mean(x * x, axis=-1, keepdims=True)
        shift = mod_ref[0, 0:1, :]
        gs = g_ref[...] * (1.0 + mod_ref[0, 1:2, :])
        hb = (x * lax.rsqrt(ms + RMS_EPS) * gs + shift).astype(BF16)
        h_ref[...] = hb
        t = _dot_nt(hb, wt_ref[...])
        tk = t[:, :LANES]
        kms = jnp.mean(tk * tk, axis=-1, keepdims=True)
        kk_ref[...] = (tk * lax.rsqrt(kms + RMS_EPS) * gk_ref[...]).astype(BF16)
        wi_ref[...] = t[:, LANES:]

    is_norm = (j == SEG_QA) | (j == SEG_KA) | (j == SEG_QB) | (j == SEG_KB)
    g = gain_ref[0]
    h = h_ref[...]
    for cc in range(D_GROUP // PROJ_CHUNK):
        acc = _dot_nt(h, w_ref[cc * PROJ_CHUNK:(cc + 1) * PROJ_CHUNK, :])
        for c in range(PROJ_CHUNK // LANES):
            a = acc[:, c * LANES:(c + 1) * LANES]
            ms = jnp.mean(a * a, axis=-1, keepdims=True)
            f = jnp.where(is_norm, lax.rsqrt(ms + RMS_EPS), 1.0)
            p_ref[cc * (PROJ_CHUNK // LANES) + c] = (a * f * g).astype(BF16)


def _projection(x2, mod3, g_norm, w_main, w_tail, gains, gk, seq_len):
    m, d = x2.shape
    tm = 1024
    tiles_per_seq = seq_len // tm
    return pl.pallas_call(
        _proj_kernel,
        grid=(m // tm, N_SEGS),
        in_specs=[
            pl.BlockSpec((tm, d), lambda i, j: (i, 0)),
            pl.BlockSpec((1, 3, d), lambda i, j: (i // tiles_per_seq, 0, 0)),
            pl.BlockSpec((1, d), lambda i, j: (0, 0)),
            pl.BlockSpec((D_GROUP, d), lambda i, j: (j, 0)),
            pl.BlockSpec((2 * LANES, d), lambda i, j: (0, 0)),
            pl.BlockSpec((1, 1, LANES), lambda i, j: (j, 0, 0)),
            pl.BlockSpec((1, LANES), lambda i, j: (0, 0)),
        ],
        out_specs=[
            pl.BlockSpec((BLOCKS_PER_SEG, tm, LANES), lambda i, j: (j, i, 0)),
            pl.BlockSpec((tm, LANES), lambda i, j: (i, 0)),
            pl.BlockSpec((tm, LANES), lambda i, j: (i, 0)),
        ],
        out_shape=[
            jax.ShapeDtypeStruct((N_SEGS * BLOCKS_PER_SEG, m, LANES), BF16),
            jax.ShapeDtypeStruct((m, LANES), BF16),
            jax.ShapeDtypeStruct((m, LANES), F32),
        ],
        scratch_shapes=[pltpu.VMEM((tm, d), BF16)],
        compiler_params=pltpu.CompilerParams(
            dimension_semantics=("arbitrary", "arbitrary"), vmem_limit_bytes=VMEM_LIMIT_BYTES),
        name="in_proj",
    )(x2, mod3, g_norm, w_main, w_tail, gains, gk)


POS_TILE = MOBA_BLOCK
AUG_KLOC, AUG_KBLK, AUG_ONE_LOC, AUG_ONE_BLK, AUG_ONEHOT0 = 0, 1, 2, 3, 8


def _key_features(seq_len):
    pos = np.arange(seq_len)
    f = np.zeros((seq_len, LANES), np.float32)
    f[:, AUG_KLOC] = pos % POS_TILE
    f[:, AUG_KBLK] = pos // POS_TILE
    f[:, AUG_ONE_LOC] = 1.0
    f[:, AUG_ONE_BLK] = 1.0
    f[pos, AUG_ONEHOT0 + pos // POS_TILE] = 1.0
    return jnp.asarray(f, BF16)


def _alibi_query_features(slope, q_tile, tq):
    lane = lax.broadcasted_iota(jnp.int32, (tq, LANES), 1)
    row = lax.broadcasted_iota(jnp.int32, (tq, LANES), 0).astype(F32)
    f = jnp.where(lane == AUG_KLOC, slope, 0.0)
    f = jnp.where(lane == AUG_KBLK, slope * POS_TILE, f)
    f = jnp.where(lane == AUG_ONE_LOC, -slope * row, f)
    return jnp.where(lane == AUG_ONE_BLK, (-slope * POS_TILE) * q_tile.astype(F32), f)


def _fold_rows(x, op):
    r = x.reshape(x.shape[0] // 8, 8, x.shape[1])
    while r.shape[0] > 1:
        half = r.shape[0] // 2
        r = op(r[:half], r[half:])
    return r[0]


TILE_CHUNKS = (4, 2, 1)


def _for_tile_chunks(n_tiles, fn, chunks=TILE_CHUNKS):
    big = chunks[0]
    shift = big.bit_length() - 1

    def many(c, carry):
        fn(c * big, big)
        return carry

    n_big = lax.shift_right_logical(n_tiles, shift)
    lax.fori_loop(0, n_big, many, 0)
    done = n_big * big
    for size in chunks[1:]:
        has = (n_tiles & size) != 0

        @pl.when(has)
        def _(done=done, size=size):
            fn(done, size)

        done = done + (n_tiles & size)


def _softmax_pass_a(n_tiles, tile, score_rows, s_ref, mx_ref):
    tq = s_ref.shape[-1]

    def pass_a(first, count):
        for h in range(N_HEADS):
            s = score_rows(h, first, count)
            s_ref[h, pl.ds(first, count)] = s.reshape(count, tile, tq)
            mx_ref[h] = jnp.maximum(mx_ref[h], _fold_rows(s, jnp.maximum))

    _for_tile_chunks(n_tiles, pass_a)
    return [jnp.max(mx_ref[h], axis=0, keepdims=True) for h in range(N_HEADS)]


def _softmax_pass_b(n_tiles, tile, m, value_cols, s_ref, ls_ref, acc_ref):
    tq = acc_ref.shape[-1]
    ls_ref[...] = jnp.zeros(ls_ref.shape, F32)
    acc_ref[...] = jnp.zeros(acc_ref.shape, F32)

    def pass_b(first, count):
        for h in range(N_HEADS):
            p = jnp.exp(s_ref[h, pl.ds(first, count)].reshape(count * tile, tq) - m[h])
            ls_ref[h] += _fold_rows(p, jnp.add)
            vt = jnp.concatenate([value_cols(h, first + u) for u in range(count)], axis=1)
            acc_ref[h] += jnp.dot(vt, p.astype(BF16), preferred_element_type=F32)

    _for_tile_chunks(n_tiles, pass_b)
    return [jnp.sum(ls_ref[h], axis=0, keepdims=True) for h in range(N_HEADS)]


def _gated_output(h, z_ref, l, acc_ref, o_ref):
    z = z_ref[h].astype(F32)
    o = (acc_ref[h] / l).T * (z * jax.nn.sigmoid(z))
    o_ref[:, h * HEAD_DIM:(h + 1) * HEAD_DIM] = o.astype(BF16)


def _transpose_values(v_ref, vt_ref, n_tiles, tile):
    for h in range(N_HEADS):
        for j in range(n_tiles):
            vt_ref[h, j] = v_ref[h, j * tile:(j + 1) * tile, :].astype(F32).T.astype(BF16)


DSA_TQ = 256
DSA_TK = POS_TILE
SEARCH_STEPS = 4


def _dsa_kernel(q_ref, k_ref, v_ref, z_ref, qi_ref, kk_ref, wi_ref, kaug_ref, o_ref,
                vt_ref, qs_ref, sc_ref, t_ref, j_ref, qcat_ref, s_ref, mx_ref, ls_ref, acc_ref, *, top_k, seq_len):
    tq, tk = DSA_TQ, DSA_TK
    i = pl.program_id(1)
    t0 = i * tq
    nk = i + 1
    idx_scale = (IDX_DIM ** -0.5) * (IDX_HEADS ** -0.5)

    @pl.when(i == 0)
    def _():
        _transpose_values(v_ref, vt_ref, seq_len // tk, tk)

    lane = lax.broadcasted_iota(jnp.int32, (tq, LANES), 1)
    key_loc = lax.broadcasted_iota(jnp.int32, (tk, tq), 0).astype(F32)
    qry_loc = lax.broadcasted_iota(jnp.int32, (tk, tq), 1).astype(F32)

    for c in range(BLOCKS_PER_SEG):
        qp = qi_ref[c].astype(F32)
        qs_ref[2 * c] = jnp.where(lane < IDX_DIM, qp, 0.0).astype(BF16)
        qs_ref[2 * c + 1] = jnp.where(lane >= IDX_DIM, qp, 0.0).astype(BF16)
    w_t = (wi_ref[...] * idx_scale).T
    w_rows = [w_t[h:h + 1, :] for h in range(IDX_HEADS)]

    def score_tiles(first, count):
        for u in range(count):
            kt = first + u
            kk = kk_ref[pl.ds(pl.multiple_of(kt * tk, tk), tk), :]
            acc = jnp.zeros((tk, tq), F32)
            for h in range(IDX_HEADS):
                acc = acc + w_rows[h] * jnp.maximum(_dot_nt(kk, qs_ref[h]), 0.0)
            off = (kt * tk - t0).astype(F32)
            sc_ref[kt] = jnp.where((key_loc + off) <= qry_loc, acc, -jnp.inf)

    _for_tile_chunks(nk, score_tiles)

    t_ref[...] = jnp.full((1, tq), NEG_INF, F32)
    j_ref[...] = jnp.full((1, tq), -1.0, F32)

    def count(pred):
        def body(kt, acc):
            off = (kt * tk).astype(F32)
            return acc + _fold_rows(jnp.where(pred(sc_ref[kt], key_loc + off), 1.0, 0.0), jnp.add)
        return jnp.sum(lax.fori_loop(0, nk, body, jnp.zeros((8, tq), F32)), axis=0, keepdims=True)

    @pl.when(t0 + tq > top_k)
    def _():
        kf = float(top_k)
        qpos = lax.broadcasted_iota(jnp.int32, (1, tq), 1) + t0
        done0 = jnp.where(qpos + 1 <= top_k, 1.0, 0.0)

        def minmax(kt, carry):
            mn, mx = carry
            s = sc_ref[kt]
            mx = jnp.maximum(mx, _fold_rows(s, jnp.maximum))
            mn = jnp.minimum(mn, _fold_rows(jnp.where(s > 0.5 * NEG_INF, s, BIG), jnp.minimum))
            return mn, mx

        mn, mx = lax.fori_loop(0, nk, minmax,
                               (jnp.full((8, tq), BIG, F32), jnp.full((8, tq), NEG_INF, F32)))
        lo0 = jnp.min(mn, axis=0, keepdims=True)
        hi0 = jnp.max(mx, axis=0, keepdims=True)

        def step(x, degen, state):
            lo, hi, thr, done, tie = state
            c = count(lambda s, pos: s > x)
            active = done == 0.0
            live = jnp.logical_and(active, jnp.logical_not(degen))
            found = jnp.logical_and(live, c == kf)
            new_tie = jnp.logical_and(active, degen)
            thr = jnp.where(found, x, jnp.where(new_tie, hi, thr))
            tie = jnp.where(new_tie, 1.0, tie)
            done = jnp.where(jnp.logical_or(found, new_tie), 1.0, done)
            lo = jnp.where(jnp.logical_and(live, c > kf), x, lo)
            hi = jnp.where(jnp.logical_and(live, c < kf), x, hi)
            return lo, hi, thr, done, tie

        state0 = (lo0, hi0, jnp.full((1, tq), NEG_INF, F32), done0, jnp.zeros((1, tq), F32))
        state0 = step(lo0, jnp.zeros((1, tq), jnp.bool_), state0)

        def cond(carry):
            return carry[1] > 0.0

        def body(carry):
            state, _ = carry
            for _ in range(SEARCH_STEPS):
                lo, hi = state[0], state[1]
                mid = 0.5 * lo + 0.5 * hi
                degen = jnp.logical_or(mid <= lo, mid >= hi)
                state = step(mid, degen, state)
            return state, jnp.max(1.0 - state[3])

        (lo, hi, thr, done, tie), _ = lax.while_loop(cond, body, (state0, jnp.max(1.0 - state0[3])))
        t_ref[...] = thr

        @pl.when(jnp.max(tie) > 0.0)
        def _():
            need = kf - count(lambda s, pos: s > thr)
            n_bits = int(np.ceil(np.log2(seq_len + 1)))

            def jstep(_, carry):
                jlo, jhi = carry
                jmid = jnp.floor((jlo + jhi) * 0.5)
                c = count(lambda s, pos: jnp.logical_and(s == thr, pos <= jmid))
                ge = c >= need
                return jnp.where(ge, jlo, jmid), jnp.where(ge, jmid, jhi)

            _, jhi = lax.fori_loop(0, n_bits, jstep,
                                   (jnp.full((1, tq), -1.0, F32),
                                    jnp.full((1, tq), float(seq_len - 1), F32)))
            j_ref[...] = jnp.where(tie > 0.0, jhi, -1.0)

    def mask_tile(kt, carry):
        s = sc_ref[kt]
        thr = t_ref[...]
        off = (kt * tk).astype(F32)
        sel = jnp.logical_or(s > thr, jnp.logical_and(s == thr, (key_loc + off) <= j_ref[...]))
        sc_ref[kt] = jnp.where(sel, 0.0, NEG_INF)
        return carry

    lax.fori_loop(0, nk, mask_tile, 0)

    for h in range(N_HEADS):
        feat = _alibi_query_features(_alibi_slope(h, N_HEADS), i, tq)
        qcat_ref[h] = jnp.concatenate([q_ref[h], feat.astype(BF16)], axis=1)

    def score_rows(h, first, count):
        ks = pl.ds(pl.multiple_of(first * tk, tk), count * tk)
        kcat = jnp.concatenate([k_ref[h, ks, :], kaug_ref[ks, :]], axis=1)
        return _dot_nt(kcat, qcat_ref[h]) + sc_ref[pl.ds(first, count)].reshape(count * tk, tq)

    mx_ref[...] = jnp.full(mx_ref.shape, NEG_INF, F32)
    value_cols = lambda h, kt: vt_ref[h, kt]
    m = _softmax_pass_a(nk, tk, score_rows, s_ref, mx_ref)
    lsum = _softmax_pass_b(nk, tk, m, value_cols, s_ref, ls_ref, acc_ref)
    for h in range(N_HEADS):
        _gated_output(h, z_ref, lsum[h], acc_ref, o_ref)


def _dsa_attention(p, kk, wi, kaug, bsz, seq_len):
    tq, tk = DSA_TQ, DSA_TK
    nq = seq_len // tq
    nkt = seq_len // tk
    m = bsz * seq_len
    top_k = min(DSA_TOPK_MAX, seq_len // 4)
    seg = lambda s: (lambda b, i: (s, b * nq + i, 0))
    seg_full = lambda s: (lambda b, i: (s, b, 0))
    kern = functools.partial(_dsa_kernel, top_k=top_k, seq_len=seq_len)
    return pl.pallas_call(
        kern,
        grid=(bsz, nq),
        in_specs=[
            pl.BlockSpec((BLOCKS_PER_SEG, tq, LANES), seg(SEG_QA)),
            pl.BlockSpec((BLOCKS_PER_SEG, seq_len, LANES), seg_full(SEG_KA)),
            pl.BlockSpec((BLOCKS_PER_SEG, seq_len, LANES), seg_full(SEG_VA)),
            pl.BlockSpec((BLOCKS_PER_SEG, tq, LANES), seg(SEG_ZA)),
            pl.BlockSpec((BLOCKS_PER_SEG, tq, LANES), seg(SEG_QIDX)),
            pl.BlockSpec((seq_len, LANES), lambda b, i: (b, 0)),
            pl.BlockSpec((tq, LANES), lambda b, i: (b * nq + i, 0)),
            pl.BlockSpec((seq_len, LANES), lambda b, i: (0, 0)),
        ],
        out_specs=pl.BlockSpec((tq, D_GROUP), lambda b, i: (b * nq + i, 0)),
        out_shape=jax.ShapeDtypeStruct((m, D_GROUP), BF16),
        scratch_shapes=[
            pltpu.VMEM((N_HEADS, nkt, HEAD_DIM, tk), BF16),
            pltpu.VMEM((IDX_HEADS, tq, LANES), BF16),
            pltpu.VMEM((nkt, tk, tq), F32),
            pltpu.VMEM((1, tq), F32),
            pltpu.VMEM((1, tq), F32),
            pltpu.VMEM((N_HEADS, tq, 2 * LANES), BF16),
            pltpu.VMEM((N_HEADS, nkt, tk, tq), F32),
            pltpu.VMEM((N_HEADS, 8, tq), F32),
            pltpu.VMEM((N_HEADS, 8, tq), F32),
            pltpu.VMEM((N_HEADS, HEAD_DIM, tq), F32),
        ],
        compiler_params=pltpu.CompilerParams(
            dimension_semantics=("arbitrary", "arbitrary"), vmem_limit_bytes=VMEM_LIMIT_BYTES),
        name="dsa_attn",
    )(p, p, p, p, p, kk, wi, kaug)


def _moba_kernel(q_ref, k_ref, v_ref, z_ref, kaug_ref, o_ref,
                 kmean_ref, vt_ref, selb_ref, qcat_ref, s_ref, mx_ref, ls_ref, acc_ref, *, n_blocks):
    bs = MOBA_BLOCK
    n = pl.program_id(1)
    top_k = min(MOBA_TOPK_MAX, n_blocks - 1)
    nb_pad = kmean_ref.shape[1]

    @pl.when(n == 0)
    def _():
        kmean_ref[...] = jnp.zeros(kmean_ref.shape, F32)
        for h in range(N_HEADS):
            for j in range(n_blocks):
                kb = k_ref[h, j * bs:(j + 1) * bs, :].astype(F32)
                kmean_ref[h, j:j + 1, :] = jnp.mean(kb, axis=0, keepdims=True)
        _transpose_values(v_ref, vt_ref, n_blocks, bs)

    blk = lax.broadcasted_iota(jnp.int32, (nb_pad, bs), 0)
    past = blk < n

    selb_ref[...] = jnp.zeros(selb_ref.shape, F32)
    for h in range(N_HEADS):
        q = q_ref[h]
        g = _dot_nt(kmean_ref[h].astype(BF16), q)
        for j in range(n_blocks):
            gj = g[j:j + 1, :]
            beats = jnp.logical_or(g > gj, jnp.logical_and(g == gj, blk < j))
            rank = jnp.sum(jnp.where(jnp.logical_and(beats, past), 1.0, 0.0), axis=0, keepdims=True)
            dropped = jnp.logical_and(rank >= float(top_k), j < n)
            selb_ref[AUG_ONEHOT0 + j:AUG_ONEHOT0 + j + 1, :] = jnp.where(dropped, NEG_INF, 0.0)
        feat = _alibi_query_features(_alibi_slope(h, N_HEADS), n, bs) + selb_ref[...].T
        qcat_ref[h] = jnp.concatenate([q, feat.astype(BF16)], axis=1)

    key_loc = lax.broadcasted_iota(jnp.int32, (bs, bs), 0)
    qry_loc = lax.broadcasted_iota(jnp.int32, (bs, bs), 1)
    causal = key_loc <= qry_loc

    def score_rows(h, first, count):
        ks = pl.ds(pl.multiple_of(first * bs, bs), count * bs)
        kcat = jnp.concatenate([k_ref[h, ks, :], kaug_ref[ks, :]], axis=1)
        return _dot_nt(kcat, qcat_ref[h])

    for h in range(N_HEADS):
        s = jnp.where(causal, score_rows(h, n, 1), NEG_INF)
        s_ref[h, n] = s
        mx_ref[h] = _fold_rows(s, jnp.maximum)
    value_cols = lambda h, j: vt_ref[h, j]
    m = _softmax_pass_a(n, bs, score_rows, s_ref, mx_ref)
    lsum = _softmax_pass_b(n + 1, bs, m, value_cols, s_ref, ls_ref, acc_ref)
    for h in range(N_HEADS):
        _gated_output(h, z_ref, lsum[h], acc_ref, o_ref)


def _moba_attention(p, kaug, bsz, seq_len):
    bs = MOBA_BLOCK
    nb = seq_len // bs
    m = bsz * seq_len
    seg = lambda s: (lambda b, i: (s, b * nb + i, 0))
    seg_full = lambda s: (lambda b, i: (s, b, 0))
    kern = functools.partial(_moba_kernel, n_blocks=nb)
    nb_pad = max(16, nb)
    assert AUG_ONEHOT0 + nb <= LANES
    return pl.pallas_call(
        kern,
        grid=(bsz, nb),
        in_specs=[
            pl.BlockSpec((BLOCKS_PER_SEG, bs, LANES), seg(SEG_QB)),
            pl.BlockSpec((BLOCKS_PER_SEG, seq_len, LANES), seg_full(SEG_KB)),
            pl.BlockSpec((BLOCKS_PER_SEG, seq_len, LANES), seg_full(SEG_VB)),
            pl.BlockSpec((BLOCKS_PER_SEG, bs, LANES), seg(SEG_ZB)),
            pl.BlockSpec((seq_len, LANES), lambda b, i: (0, 0)),
        ],
        out_specs=pl.BlockSpec((bs, D_GROUP), lambda b, i: (b * nb + i, 0)),
        out_shape=jax.ShapeDtypeStruct((m, D_GROUP), BF16),
        scratch_shapes=[
            pltpu.VMEM((N_HEADS, nb_pad, HEAD_DIM), F32),
            pltpu.VMEM((N_HEADS, nb, HEAD_DIM, bs), BF16),
            pltpu.VMEM((LANES, bs), F32),
            pltpu.VMEM((N_HEADS, bs, 2 * LANES), BF16),
            pltpu.VMEM((N_HEADS, nb, bs, bs), F32),
            pltpu.VMEM((N_HEADS, 8, bs), F32),
            pltpu.VMEM((N_HEADS, 8, bs), F32),
            pltpu.VMEM((N_HEADS, HEAD_DIM, bs), F32),
        ],
        compiler_params=pltpu.CompilerParams(
            dimension_semantics=("arbitrary", "arbitrary"), vmem_limit_bytes=VMEM_LIMIT_BYTES),
        name="moba_attn",
    )(p, p, p, p, kaug)


OUT_CHUNK = 512


def _out_kernel(ya_ref, yb_ref, wa_ref, wb_ref, x_ref, mod_ref, o_ref):
    ya = ya_ref[...]
    yb = yb_ref[...]
    for c in range(o_ref.shape[1] // OUT_CHUNK):
        cs = slice(c * OUT_CHUNK, (c + 1) * OUT_CHUNK)
        y = jnp.dot(ya, wa_ref[:, cs], preferred_element_type=F32)
        y = y + jnp.dot(yb, wb_ref[:, cs], preferred_element_type=F32)
        o_ref[:, cs] = x_ref[:, cs] + mod_ref[0, 2:3, cs] * y


def _out_projection(ya, yb, w_out_bf, x2, mod3, seq_len):
    m, d = x2.shape
    tm = 512
    tiles_per_seq = seq_len // tm
    return pl.pallas_call(
        _out_kernel,
        grid=(m // tm,),
        in_specs=[
            pl.BlockSpec((tm, D_GROUP), lambda i: (i, 0)),
            pl.BlockSpec((tm, D_GROUP), lambda i: (i, 0)),
            pl.BlockSpec((D_GROUP, d), lambda i: (0, 0)),
            pl.BlockSpec((D_GROUP, d), lambda i: (1, 0)),
            pl.BlockSpec((tm, d), lambda i: (i, 0)),
            pl.BlockSpec((1, 3, d), lambda i: (i // tiles_per_seq, 0, 0)),
        ],
        out_specs=pl.BlockSpec((tm, d), lambda i: (i, 0)),
        out_shape=jax.ShapeDtypeStruct((m, d), F32),
        compiler_params=pltpu.CompilerParams(
            dimension_semantics=("arbitrary",), vmem_limit_bytes=VMEM_LIMIT_BYTES),
        name="out_proj",
    )(ya, yb, w_out_bf, w_out_bf, x2, mod3)


RELAYOUT_ROWS = 512


def _relayout_kernel(a_ref, b_ref, mid_ref, main_ref, tail_ref, *, n_a_tiles, n_mid):
    r = pl.program_id(0)

    @pl.when(r < n_a_tiles)
    def _():
        main_ref[...] = a_ref[...].astype(BF16)

    @pl.when(r >= n_a_tiles)
    def _():
        main_ref[:RELAYOUT_ROWS - n_mid, :] = a_ref[n_mid:, :].astype(BF16)
        main_ref[RELAYOUT_ROWS - n_mid:, :] = b_ref[:n_mid, :].astype(BF16)

    kidx = mid_ref[:IDX_DIM, :].astype(BF16)
    tail_ref[0:IDX_DIM, :] = kidx
    tail_ref[IDX_DIM:2 * IDX_DIM, :] = kidx
    tail_ref[2 * IDX_DIM:2 * IDX_DIM + IDX_HEADS, :] = mid_ref[IDX_DIM:IDX_DIM + IDX_HEADS, :].astype(BF16)
    tail_ref[2 * IDX_DIM + IDX_HEADS:, :] = jnp.zeros(
        (tail_ref.shape[0] - 2 * IDX_DIM - IDX_HEADS, tail_ref.shape[1]), BF16)


def _relayout_weights(w_t):
    n_in, d = w_t.shape
    n_a = 4 * D_GROUP + IDX_HEADS * IDX_DIM
    n_mid = IDX_DIM + IDX_HEADS
    tr = RELAYOUT_ROWS
    assert n_in == n_a + n_mid + 4 * D_GROUP and n_a % tr == 0 and (4 * D_GROUP) % tr == 0
    assert n_mid % 16 == 0 and n_mid < tr and n_a % n_mid == 0
    n_main = n_in - n_mid
    n_a_tiles = n_a // tr
    return pl.pallas_call(
        functools.partial(_relayout_kernel, n_a_tiles=n_a_tiles, n_mid=n_mid),
        grid=(n_main // tr,),
        in_specs=[
            pl.BlockSpec((tr, d), lambda r: (r, 0)),
            pl.BlockSpec((tr, d), lambda r: (jnp.maximum(r, n_a_tiles - 1) + 1, 0)),
            pl.BlockSpec((n_mid, d), lambda r: (n_a // n_mid, 0)),
        ],
        out_specs=[
            pl.BlockSpec((tr, d), lambda r: (r, 0)),
            pl.BlockSpec((2 * LANES, d), lambda r: (0, 0)),
        ],
        out_shape=[
            jax.ShapeDtypeStruct((n_main, d), BF16),
            jax.ShapeDtypeStruct((2 * LANES, d), BF16),
        ],
        compiler_params=pltpu.CompilerParams(
            dimension_semantics=("arbitrary",), vmem_limit_bytes=VMEM_LIMIT_BYTES),
        name="w_relayout",
    )(w_t, w_t, w_t)


def _layer(x, c, w_ada, b_ada, g_norm, w_in, q_norm_a, k_norm_a, k_norm_idx, q_norm_b, k_norm_b, w_out):
    bsz, seq_len, d = x.shape
    assert seq_len % MOBA_BLOCK == 0 and seq_len % 1024 == 0 and d % 1024 == 0
    x2 = x.reshape(bsz * seq_len, d)

    mod3 = _modulation(c, w_ada, b_ada).reshape(bsz, 3, d)

    w_main, w_tail = _relayout_weights(jnp.swapaxes(w_in, 0, 1))
    ones = jnp.ones((HEAD_DIM,), F32)
    sm_scale = HEAD_DIM ** -0.5
    gains = jnp.stack([q_norm_a * sm_scale, k_norm_a, ones, ones, ones,
                       q_norm_b * sm_scale, k_norm_b, ones, ones]).reshape(N_SEGS, 1, HEAD_DIM)
    for h in range(N_HEADS):
        slope = _alibi_slope(h, N_HEADS)
        assert float(np.float32(slope).astype(jnp.bfloat16)) == slope, "ALiBi slopes must be exact in bf16"
    assert seq_len // POS_TILE <= 256 and POS_TILE <= 256, "positions must split into bf16-exact parts"
    kaug = _key_features(seq_len)
    gk = jnp.concatenate([k_norm_idx, k_norm_idx]).reshape(1, LANES)

    p, kk, wi = _projection(x2, mod3, g_norm.reshape(1, d), w_main, w_tail, gains, gk, seq_len)
    ya = _dsa_attention(p, kk, wi, kaug, bsz, seq_len)
    yb = _moba_attention(p, kaug, bsz, seq_len)
    out = _out_projection(ya, yb, w_out.astype(BF16), x2, mod3, seq_len)
    return out.reshape(bsz, seq_len, d)


def kernel(x, c, w_ada, b_ada, g_norm, w_in, q_norm_a, k_norm_a, k_norm_idx, q_norm_b, k_norm_b, w_out):
    for i in range(w_ada.shape[0]):
        x = _layer(x, c, w_ada[i], b_ada[i], g_norm[i], w_in[i], q_norm_a[i], k_norm_a[i],
                   k_norm_idx[i], q_norm_b[i], k_norm_b[i], w_out[i])
    return x
```

```python
import functools

import jax
import jax.numpy as jnp
import numpy as np
from jax import lax
from jax.experimental import pallas as pl
from jax.experimental.pallas import tpu as pltpu

F32 = jnp.float32
BF16 = jnp.bfloat16

HEAD_DIM = 128
N_HEADS = 8
D_GROUP = N_HEADS * HEAD_DIM
IDX_HEADS = 16
IDX_DIM = 64
DSA_TOPK_MAX = 256
MOBA_BLOCK = 256
MOBA_TOPK_MAX = 3
RMS_EPS = 1e-6
NEG_INF = -1e30
BIG = 1e30

LANES = 128
VMEM_LIMIT_BYTES = 56 * 1024 * 1024

SEG_QA, SEG_KA, SEG_VA, SEG_ZA, SEG_QIDX, SEG_QB, SEG_KB, SEG_VB, SEG_ZB = range(9)
N_SEGS = 9
BLOCKS_PER_SEG = D_GROUP // LANES


def _alibi_slope(h, n):
    return float(2.0 ** (-8.0 * (h + 1) / n))


def _dot_nt(a, b):
    return lax.dot_general(a, b, (((1,), (1,)), ((), ())), preferred_element_type=F32)


def _mod_kernel(c_ref, w_ref, b_ref, o_ref):
    c = c_ref[...]
    s = c * jax.nn.sigmoid(c)
    o_ref[...] = jnp.dot(s, w_ref[...], preferred_element_type=F32) + b_ref[...]


def _modulation(c, w_ada, b_ada):
    bsz, d = c.shape
    n = w_ada.shape[1]
    tn = 1024
    return pl.pallas_call(
        _mod_kernel,
        grid=(n // tn,),
        in_specs=[
            pl.BlockSpec((bsz, d), lambda j: (0, 0)),
            pl.BlockSpec((d, tn), lambda j: (0, j)),
            pl.BlockSpec((1, tn), lambda j: (0, j)),
        ],
        out_specs=pl.BlockSpec((bsz, tn), lambda j: (0, j)),
        out_shape=jax.ShapeDtypeStruct((bsz, n), F32),
        compiler_params=pltpu.CompilerParams(
            dimension_semantics=("arbitrary",), vmem_limit_bytes=VMEM_LIMIT_BYTES),
        name="adaln_mod",
    )(c, w_ada, b_ada.reshape(1, n))


PROJ_CHUNK = 2 * LANES


def _proj_kernel(x_ref, mod_ref, g_ref, w_ref, wt_ref, gain_ref, gk_ref,
                 p_ref, kk_ref, wi_ref, h_ref):
    j = pl.program_id(1)

    @pl.when(j == 0)
    def _():
        x = x_ref[...]
        ms = jnp.mean(x * x, axis=-1, keepdims=True)
        shift = mod_ref[0, 0:1, :]
        gs = g_ref[...] * (1.0 + mod_ref[0, 1:2, :])
        hb = (x * lax.rsqrt(ms + RMS_EPS) * gs + shift).astype(BF16)
        h_ref[...] = hb
        t = _dot_nt(hb, wt_ref[...])
        tk = t[:, :LANES]
        kms = jnp.mean(tk * tk, axis=-1, keepdims=True)
        kk_ref[...] = (tk * lax.rsqrt(kms + RMS_EPS) * gk_ref[...]).astype(BF16)
        wi_ref[...] = t[:, LANES:]

    def segment(with_norm):
        g = gain_ref[0]
        h = h_ref[...]
        for cc in range(D_GROUP // PROJ_CHUNK):
            acc = _dot_nt(h, w_ref[cc * PROJ_CHUNK:(cc + 1) * PROJ_CHUNK, :])
            for c in range(PROJ_CHUNK // LANES):
                a = acc[:, c * LANES:(c + 1) * LANES]
                if with_norm:
                    ms = jnp.mean(a * a, axis=-1, keepdims=True)
                    a = a * lax.rsqrt(ms + RMS_EPS) * g
                p_ref[cc * (PROJ_CHUNK // LANES) + c] = a.astype(BF16)

    is_norm = (j == SEG_QA) | (j == SEG_KA) | (j == SEG_QB) | (j == SEG_KB)
    pl.when(is_norm)(functools.partial(segment, True))
    pl.when(jnp.logical_not(is_norm))(functools.partial(segment, False))


def _projection(x2, mod3, g_norm, w_main, w_tail, gains, gk, seq_len):
    m, d = x2.shape
    tm = 1024
    tiles_per_seq = seq_len // tm
    return pl.pallas_call(
        _proj_kernel,
        grid=(m // tm, N_SEGS),
        in_specs=[
            pl.BlockSpec((tm, d), lambda i, j: (i, 0)),
            pl.BlockSpec((1, 3, d), lambda i, j: (i // tiles_per_seq, 0, 0)),
            pl.BlockSpec((1, d), lambda i, j: (0, 0)),
            pl.BlockSpec((D_GROUP, d), lambda i, j: (j, 0)),
            pl.BlockSpec((2 * LANES, d), lambda i, j: (0, 0)),
            pl.BlockSpec((1, 1, LANES), lambda i, j: (j, 0, 0)),
            pl.BlockSpec((1, LANES), lambda i, j: (0, 0)),
        ],
        out_specs=[
            pl.BlockSpec((BLOCKS_PER_SEG, tm, LANES), lambda i, j: (j, i, 0)),
            pl.BlockSpec((tm, LANES), lambda i, j: (i, 0)),
            pl.BlockSpec((tm, LANES), lambda i, j: (i, 0)),
        ],
        out_shape=[
            jax.ShapeDtypeStruct((N_SEGS * BLOCKS_PER_SEG, m, LANES), BF16),
            jax.ShapeDtypeStruct((m, LANES), BF16),
            jax.ShapeDtypeStruct((m, LANES), F32),
        ],
        scratch_shapes=[pltpu.VMEM((tm, d), BF16)],
        compiler_params=pltpu.CompilerParams(
            dimension_semantics=("arbitrary", "arbitrary"), vmem_limit_bytes=VMEM_LIMIT_BYTES),
        name="in_proj",
    )(x2, mod3, g_norm, w_main, w_tail, gains, gk)


POS_TILE = MOBA_BLOCK
AUG_KLOC, AUG_KBLK, AUG_ONE_LOC, AUG_ONE_BLK, AUG_ONEHOT0 = 0, 1, 2, 3, 8


def _key_features(seq_len):
    pos = np.arange(seq_len)
    f = np.zeros((seq_len, LANES), np.float32)
    f[:, AUG_KLOC] = pos % POS_TILE
    f[:, AUG_KBLK] = pos // POS_TILE
    f[:, AUG_ONE_LOC] = 1.0
    f[:, AUG_ONE_BLK] = 1.0
    f[pos, AUG_ONEHOT0 + pos // POS_TILE] = 1.0
    return jnp.asarray(f, BF16)


def _alibi_query_features(slope, q_tile, tq):
    lane = lax.broadcasted_iota(jnp.int32, (tq, LANES), 1)
    row = lax.broadcasted_iota(jnp.int32, (tq, LANES), 0).astype(F32)
    f = jnp.where(lane == AUG_KLOC, slope, 0.0)
    f = jnp.where(lane == AUG_KBLK, slope * POS_TILE, f)
    f = jnp.where(lane == AUG_ONE_LOC, -slope * row, f)
    return jnp.where(lane == AUG_ONE_BLK, (-slope * POS_TILE) * q_tile.astype(F32), f)


def _fold_rows(x, op):
    r = x.reshape(x.shape[0] // 8, 8, x.shape[1])
    while r.shape[0] > 1:
        half = r.shape[0] // 2
        r = op(r[:half], r[half:])
    return r[0]


TILE_CHUNKS = (4, 2, 1)


def _for_tile_chunks(n_tiles, fn, chunks=TILE_CHUNKS):
    big = chunks[0]
    shift = big.bit_length() - 1

    def many(c, carry):
        fn(c * big, big)
        return carry

    n_big = lax.shift_right_logical(n_tiles, shift)
    lax.fori_loop(0, n_big, many, 0)
    done = n_big * big
    for size in chunks[1:]:
        has = (n_tiles & size) != 0

        @pl.when(has)
        def _(done=done, size=size):
            fn(done, size)

        done = done + (n_tiles & size)


def _softmax_pass_a(n_tiles, tile, score_rows, s_ref, mx_ref):
    tq = s_ref.shape[-1]

    def pass_a(first, count):
        for h in range(N_HEADS):
            s = score_rows(h, first, count)
            s_ref[h, pl.ds(first, count)] = s.reshape(count, tile, tq)
            mx_ref[h] = jnp.maximum(mx_ref[h], _fold_rows(s, jnp.maximum))

    _for_tile_chunks(n_tiles, pass_a)
    return [jnp.max(mx_ref[h], axis=0, keepdims=True) for h in range(N_HEADS)]


def _softmax_pass_b(n_tiles, tile, m, value_cols, s_ref, ls_ref, acc_ref):
    tq = acc_ref.shape[-1]
    ls_ref[...] = jnp.zeros(ls_ref.shape, F32)
    acc_ref[...] = jnp.zeros(acc_ref.shape, F32)

    def pass_b(first, count):
        for h in range(N_HEADS):
            p = jnp.exp(s_ref[h, pl.ds(first, count)].reshape(count * tile, tq) - m[h])
            ls_ref[h] += _fold_rows(p, jnp.add)
            vt = jnp.concatenate([value_cols(h, first + u) for u in range(count)], axis=1)
            acc_ref[h] += jnp.dot(vt, p.astype(BF16), preferred_element_type=F32)

    _for_tile_chunks(n_tiles, pass_b)
    return [jnp.sum(ls_ref[h], axis=0, keepdims=True) for h in range(N_HEADS)]


def _gated_output(h, z_ref, l, acc_ref, o_ref):
    z = z_ref[h].astype(F32)
    o = (acc_ref[h] / l).T * (z * jax.nn.sigmoid(z))
    o_ref[:, h * HEAD_DIM:(h + 1) * HEAD_DIM] = o.astype(BF16)


def _transpose_values(v_ref, vt_ref, n_tiles, tile):
    for h in range(N_HEADS):
        for j in range(n_tiles):
            vt_ref[h, j] = v_ref[h, j * tile:(j + 1) * tile, :].astype(F32).T.astype(BF16)


DSA_TQ = 256
DSA_TK = POS_TILE
SEARCH_STEPS = 4


def _dsa_kernel(q_ref, k_ref, v_ref, z_ref, qi_ref, kk_ref, wi_ref, kaug_ref, o_ref,
                vt_ref, qs_ref, sc_ref, t_ref, j_ref, qcat_ref, s_ref, mx_ref, ls_ref, acc_ref, *, top_k, seq_len):
    tq, tk = DSA_TQ, DSA_TK
    i = pl.program_id(1)
    t0 = i * tq
    nk = i + 1
    idx_scale = (IDX_DIM ** -0.5) * (IDX_HEADS ** -0.5)

    @pl.when(i == 0)
    def _():
        _transpose_values(v_ref, vt_ref, seq_len // tk, tk)

    lane = lax.broadcasted_iota(jnp.int32, (tq, LANES), 1)
    key_loc = lax.broadcasted_iota(jnp.int32, (tk, tq), 0).astype(F32)
    qry_loc = lax.broadcasted_iota(jnp.int32, (tk, tq), 1).astype(F32)

    for c in range(BLOCKS_PER_SEG):
        qp = qi_ref[c].astype(F32)
        qs_ref[2 * c] = jnp.where(lane < IDX_DIM, qp, 0.0).astype(BF16)
        qs_ref[2 * c + 1] = jnp.where(lane >= IDX_DIM, qp, 0.0).astype(BF16)
    w_t = (wi_ref[...] * idx_scale).T
    w_rows = [w_t[h:h + 1, :] for h in range(IDX_HEADS)]

    def score_tiles(first, count):
        for u in range(count):
            kt = first + u
            kk = kk_ref[pl.ds(pl.multiple_of(kt * tk, tk), tk), :]
            acc = jnp.zeros((tk, tq), F32)
            for h in range(IDX_HEADS):
                acc = acc + w_rows[h] * jnp.maximum(_dot_nt(kk, qs_ref[h]), 0.0)
            off = (kt * tk - t0).astype(F32)
            sc_ref[kt] = jnp.where((key_loc + off) <= qry_loc, acc, -jnp.inf)

    _for_tile_chunks(nk, score_tiles)

    t_ref[...] = jnp.full((1, tq), NEG_INF, F32)
    j_ref[...] = jnp.full((1, tq), -1.0, F32)

    def count(pred):
        def body(kt, acc):
            off = (kt * tk).astype(F32)
            return acc + _fold_rows(jnp.where(pred(sc_ref[kt], key_loc + off), 1.0, 0.0), jnp.add)
        return jnp.sum(lax.fori_loop(0, nk, body, jnp.zeros((8, tq), F32)), axis=0, keepdims=True)

    @pl.when(t0 + tq > top_k)
    def _():
        kf = float(top_k)
        qpos = lax.broadcasted_iota(jnp.int32, (1, tq), 1) + t0
        done0 = jnp.where(qpos + 1 <= top_k, 1.0, 0.0)

        def minmax(kt, carry):
            mn, mx = carry
            s = sc_ref[kt]
            mx = jnp.maximum(mx, _fold_rows(s, jnp.maximum))
            mn = jnp.minimum(mn, _fold_rows(jnp.where(s > 0.5 * NEG_INF, s, BIG), jnp.minimum))
            return mn, mx

        mn, mx = lax.fori_loop(0, nk, minmax,
                               (jnp.full((8, tq), BIG, F32), jnp.full((8, tq), NEG_INF, F32)))
        lo0 = jnp.min(mn, axis=0, keepdims=True)
        hi0 = jnp.max(mx, axis=0, keepdims=True)

        def step(x, degen, state):
            lo, hi, thr, done, tie = state
            c = count(lambda s, pos: s > x)
            active = done == 0.0
            live = jnp.logical_and(active, jnp.logical_not(degen))
            found = jnp.logical_and(live, c == kf)
            new_tie = jnp.logical_and(active, degen)
            thr = jnp.where(found, x, jnp.where(new_tie, hi, thr))
            tie = jnp.where(new_tie, 1.0, tie)
            done = jnp.where(jnp.logical_or(found, new_tie), 1.0, done)
            lo = jnp.where(jnp.logical_and(live, c > kf), x, lo)
            hi = jnp.where(jnp.logical_and(live, c < kf), x, hi)
            return lo, hi, thr, done, tie

        state0 = (lo0, hi0, jnp.full((1, tq), NEG_INF, F32), done0, jnp.zeros((1, tq), F32))
        state0 = step(lo0, jnp.zeros((1, tq), jnp.bool_), state0)

        def cond(carry):
            return carry[1] > 0.0

        def body(carry):
            state, _ = carry
            for _ in range(SEARCH_STEPS):
                lo, hi = state[0], state[1]
                mid = 0.5 * lo + 0.5 * hi
                degen = jnp.logical_or(mid <= lo, mid >= hi)
                state = step(mid, degen, state)
            return state, jnp.max(1.0 - state[3])

        (lo, hi, thr, done, tie), _ = lax.while_loop(cond, body, (state0, jnp.max(1.0 - state0[3])))
        t_ref[...] = thr

        @pl.when(jnp.max(tie) > 0.0)
        def _():
            need = kf - count(lambda s, pos: s > thr)
            n_bits = int(np.ceil(np.log2(seq_len + 1)))

            def jstep(_, carry):
                jlo, jhi = carry
                jmid = jnp.floor((jlo + jhi) * 0.5)
                c = count(lambda s, pos: jnp.logical_and(s == thr, pos <= jmid))
                ge = c >= need
                return jnp.where(ge, jlo, jmid), jnp.where(ge, jmid, jhi)

            _, jhi = lax.fori_loop(0, n_bits, jstep,
                                   (jnp.full((1, tq), -1.0, F32),
                                    jnp.full((1, tq), float(seq_len - 1), F32)))
            j_ref[...] = jnp.where(tie > 0.0, jhi, -1.0)

    def mask_tile(kt, carry):
        s = sc_ref[kt]
        thr = t_ref[...]
        off = (kt * tk).astype(F32)
        sel = jnp.logical_or(s > thr, jnp.logical_and(s == thr, (key_loc + off) <= j_ref[...]))
        sc_ref[kt] = jnp.where(sel, 0.0, NEG_INF)
        return carry

    lax.fori_loop(0, nk, mask_tile, 0)

    for h in range(N_HEADS):
        feat = _alibi_query_features(_alibi_slope(h, N_HEADS), i, tq)
        qcat_ref[h] = jnp.concatenate([q_ref[h], feat.astype(BF16)], axis=1)

    def score_rows(h, first, count):
        ks = pl.ds(pl.multiple_of(first * tk, tk), count * tk)
        kcat = jnp.concatenate([k_ref[h, ks, :], kaug_ref[ks, :]], axis=1)
        return _dot_nt(kcat, qcat_ref[h]) + sc_ref[pl.ds(first, count)].reshape(count * tk, tq)

    mx_ref[...] = jnp.full(mx_ref.shape, NEG_INF, F32)
    value_cols = lambda h, kt: vt_ref[h, kt]
    m = _softmax_pass_a(nk, tk, score_rows, s_ref, mx_ref)
    lsum = _softmax_pass_b(nk, tk, m, value_cols, s_ref, ls_ref, acc_ref)
    for h in range(N_HEADS):
        _gated_output(h, z_ref, lsum[h], acc_ref, o_ref)


def _dsa_attention(p, kk, wi, kaug, bsz, seq_len):
    tq, tk = DSA_TQ, DSA_TK
    nq = seq_len // tq
    nkt = seq_len // tk
    m = bsz * seq_len
    top_k = min(DSA_TOPK_MAX, seq_len // 4)
    seg = lambda s: (lambda b, i: (s, b * nq + i, 0))
    seg_full = lambda s: (lambda b, i: (s, b, 0))
    kern = functools.partial(_dsa_kernel, top_k=top_k, seq_len=seq_len)
    return pl.pallas_call(
        kern,
        grid=(bsz, nq),
        in_specs=[
            pl.BlockSpec((BLOCKS_PER_SEG, tq, LANES), seg(SEG_QA)),
            pl.BlockSpec((BLOCKS_PER_SEG, seq_len, LANES), seg_full(SEG_KA)),
            pl.BlockSpec((BLOCKS_PER_SEG, seq_len, LANES), seg_full(SEG_VA)),
            pl.BlockSpec((BLOCKS_PER_SEG, tq, LANES), seg(SEG_ZA)),
            pl.BlockSpec((BLOCKS_PER_SEG, tq, LANES), seg(SEG_QIDX)),
            pl.BlockSpec((seq_len, LANES), lambda b, i: (b, 0)),
            pl.BlockSpec((tq, LANES), lambda b, i: (b * nq + i, 0)),
            pl.BlockSpec((seq_len, LANES), lambda b, i: (0, 0)),
        ],
        out_specs=pl.BlockSpec((tq, D_GROUP), lambda b, i: (b * nq + i, 0)),
        out_shape=jax.ShapeDtypeStruct((m, D_GROUP), BF16),
        scratch_shapes=[
            pltpu.VMEM((N_HEADS, nkt, HEAD_DIM, tk), BF16),
            pltpu.VMEM((IDX_HEADS, tq, LANES), BF16),
            pltpu.VMEM((nkt, tk, tq), F32),
            pltpu.VMEM((1, tq), F32),
            pltpu.VMEM((1, tq), F32),
            pltpu.VMEM((N_HEADS, tq, 2 * LANES), BF16),
            pltpu.VMEM((N_HEADS, nkt, tk, tq), F32),
            pltpu.VMEM((N_HEADS, 8, tq), F32),
            pltpu.VMEM((N_HEADS, 8, tq), F32),
            pltpu.VMEM((N_HEADS, HEAD_DIM, tq), F32),
        ],
        compiler_params=pltpu.CompilerParams(
            dimension_semantics=("arbitrary", "arbitrary"), vmem_limit_bytes=VMEM_LIMIT_BYTES),
        name="dsa_attn",
    )(p, p, p, p, p, kk, wi, kaug)


def _moba_kernel(q_ref, k_ref, v_ref, z_ref, kaug_ref, o_ref,
                 kmean_ref, vt_ref, selb_ref, qcat_ref, s_ref, mx_ref, ls_ref, acc_ref, *, n_blocks):
    bs = MOBA_BLOCK
    n = pl.program_id(1)
    top_k = min(MOBA_TOPK_MAX, n_blocks - 1)

    @pl.when(n == 0)
    def _():
        kmean_ref[...] = jnp.zeros(kmean_ref.shape, F32)
        for h in range(N_HEADS):
            for j in range(n_blocks):
                kb = k_ref[h, j * bs:(j + 1) * bs, :].astype(F32)
                kmean_ref[h, j:j + 1, :] = jnp.mean(kb, axis=0, keepdims=True)
        _transpose_values(v_ref, vt_ref, n_blocks, bs)

    nb_rows = -(-n_blocks // 8) * 8
    blk = lax.broadcasted_iota(jnp.int32, (nb_rows, bs), 0)
    past = blk < n

    selb_ref[...] = jnp.zeros(selb_ref.shape, F32)
    for h in range(N_HEADS):
        q = q_ref[h]
        g = _dot_nt(kmean_ref[h].astype(BF16), q)[:nb_rows]
        for j in range(n_blocks):
            gj = g[j:j + 1, :]
            beats = jnp.logical_or(g > gj, jnp.logical_and(g == gj, blk < j))
            rank = jnp.sum(jnp.where(jnp.logical_and(beats, past), 1.0, 0.0), axis=0, keepdims=True)
            dropped = jnp.logical_and(rank >= float(top_k), j < n)
            selb_ref[AUG_ONEHOT0 + j:AUG_ONEHOT0 + j + 1, :] = jnp.where(dropped, NEG_INF, 0.0)
        feat = _alibi_query_features(_alibi_slope(h, N_HEADS), n, bs) + selb_ref[...].T
        qcat_ref[h] = jnp.concatenate([q, feat.astype(BF16)], axis=1)

    key_loc = lax.broadcasted_iota(jnp.int32, (bs, bs), 0)
    qry_loc = lax.broadcasted_iota(jnp.int32, (bs, bs), 1)
    causal = key_loc <= qry_loc

    def score_rows(h, first, count):
        ks = pl.ds(pl.multiple_of(first * bs, bs), count * bs)
        kcat = jnp.concatenate([k_ref[h, ks, :], kaug_ref[ks, :]], axis=1)
        return _dot_nt(kcat, qcat_ref[h])

    for h in range(N_HEADS):
        s = jnp.where(causal, score_rows(h, n, 1), NEG_INF)
        s_ref[h, n] = s
        mx_ref[h] = _fold_rows(s, jnp.maximum)
    value_cols = lambda h, j: vt_ref[h, j]
    m = _softmax_pass_a(n, bs, score_rows, s_ref, mx_ref)
    lsum = _softmax_pass_b(n + 1, bs, m, value_cols, s_ref, ls_ref, acc_ref)
    for h in range(N_HEADS):
        _gated_output(h, z_ref, lsum[h], acc_ref, o_ref)


def _moba_attention(p, kaug, bsz, seq_len):
    bs = MOBA_BLOCK
    nb = seq_len // bs
    m = bsz * seq_len
    seg = lambda s: (lambda b, i: (s, b * nb + i, 0))
    seg_full = lambda s: (lambda b, i: (s, b, 0))
    kern = functools.partial(_moba_kernel, n_blocks=nb)
    nb_pad = max(16, nb)
    assert AUG_ONEHOT0 + nb <= LANES
    return pl.pallas_call(
        kern,
        grid=(bsz, nb),
        in_specs=[
            pl.BlockSpec((BLOCKS_PER_SEG, bs, LANES), seg(SEG_QB)),
            pl.BlockSpec((BLOCKS_PER_SEG, seq_len, LANES), seg_full(SEG_KB)),
            pl.BlockSpec((BLOCKS_PER_SEG, seq_len, LANES), seg_full(SEG_VB)),
            pl.BlockSpec((BLOCKS_PER_SEG, bs, LANES), seg(SEG_ZB)),
            pl.BlockSpec((seq_len, LANES), lambda b, i: (0, 0)),
        ],
        out_specs=pl.BlockSpec((bs, D_GROUP), lambda b, i: (b * nb + i, 0)),
        out_shape=jax.ShapeDtypeStruct((m, D_GROUP), BF16),
        scratch_shapes=[
            pltpu.VMEM((N_HEADS, nb_pad, HEAD_DIM), F32),
            pltpu.VMEM((N_HEADS, nb, HEAD_DIM, bs), BF16),
            pltpu.VMEM((LANES, bs), F32),
            pltpu.VMEM((N_HEADS, bs, 2 * LANES), BF16),
            pltpu.VMEM((N_HEADS, nb, bs, bs), F32),
            pltpu.VMEM((N_HEADS, 8, bs), F32),
            pltpu.VMEM((N_HEADS, 8, bs), F32),
            pltpu.VMEM((N_HEADS, HEAD_DIM, bs), F32),
        ],
        compiler_params=pltpu.CompilerParams(
            dimension_semantics=("arbitrary", "arbitrary"), vmem_limit_bytes=VMEM_LIMIT_BYTES),
        name="moba_attn",
    )(p, p, p, p, kaug)


OUT_CHUNK = 512


def _out_kernel(ya_ref, yb_ref, wa_ref, wb_ref, x_ref, mod_ref, o_ref):
    ya = ya_ref[...]
    yb = yb_ref[...]
    for c in range(o_ref.shape[1] // OUT_CHUNK):
        cs = slice(c * OUT_CHUNK, (c + 1) * OUT_CHUNK)
        y = jnp.dot(ya, wa_ref[:, cs], preferred_element_type=F32)
        y = y + jnp.dot(yb, wb_ref[:, cs], preferred_element_type=F32)
        o_ref[:, cs] = x_ref[:, cs] + mod_ref[0, 2:3, cs] * y


def _out_projection(ya, yb, w_out_bf, x2, mod3, seq_len):
    m, d = x2.shape
    tm = 512
    tiles_per_seq = seq_len // tm
    return pl.pallas_call(
        _out_kernel,
        grid=(m // tm,),
        in_specs=[
            pl.BlockSpec((tm, D_GROUP), lambda i: (i, 0)),
            pl.BlockSpec((tm, D_GROUP), lambda i: (i, 0)),
            pl.BlockSpec((D_GROUP, d), lambda i: (0, 0)),
            pl.BlockSpec((D_GROUP, d), lambda i: (1, 0)),
            pl.BlockSpec((tm, d), lambda i: (i, 0)),
            pl.BlockSpec((1, 3, d), lambda i: (i // tiles_per_seq, 0, 0)),
        ],
        out_specs=pl.BlockSpec((tm, d), lambda i: (i, 0)),
        out_shape=jax.ShapeDtypeStruct((m, d), F32),
        compiler_params=pltpu.CompilerParams(
            dimension_semantics=("arbitrary",), vmem_limit_bytes=VMEM_LIMIT_BYTES),
        name="out_proj",
    )(ya, yb, w_out_bf, w_out_bf, x2, mod3)


RELAYOUT_ROWS = 512


def _relayout_kernel(a_ref, b_ref, mid_ref, main_ref, tail_ref, *, n_a_tiles, n_mid):
    r = pl.program_id(0)

    @pl.when(r < n_a_tiles)
    def _():
        main_ref[...] = a_ref[...].astype(BF16)

    @pl.when(r >= n_a_tiles)
    def _():
        main_ref[:RELAYOUT_ROWS - n_mid, :] = a_ref[n_mid:, :].astype(BF16)
        main_ref[RELAYOUT_ROWS - n_mid:, :] = b_ref[:n_mid, :].astype(BF16)

    kidx = mid_ref[:IDX_DIM, :].astype(BF16)
    tail_ref[0:IDX_DIM, :] = kidx
    tail_ref[IDX_DIM:2 * IDX_DIM, :] = kidx
    tail_ref[2 * IDX_DIM:2 * IDX_DIM + IDX_HEADS, :] = mid_ref[IDX_DIM:IDX_DIM + IDX_HEADS, :].astype(BF16)
    tail_ref[2 * IDX_DIM + IDX_HEADS:, :] = jnp.zeros(
        (tail_ref.shape[0] - 2 * IDX_DIM - IDX_HEADS, tail_ref.shape[1]), BF16)


def _relayout_weights(w_t):
    n_in, d = w_t.shape
    n_a = 4 * D_GROUP + IDX_HEADS * IDX_DIM
    n_mid = IDX_DIM + IDX_HEADS
    tr = RELAYOUT_ROWS
    assert n_in == n_a + n_mid + 4 * D_GROUP and n_a % tr == 0 and (4 * D_GROUP) % tr == 0
    assert n_mid % 16 == 0 and n_mid < tr and n_a % n_mid == 0
    n_main = n_in - n_mid
    n_a_tiles = n_a // tr
    return pl.pallas_call(
        functools.partial(_relayout_kernel, n_a_tiles=n_a_tiles, n_mid=n_mid),
        grid=(n_main // tr,),
        in_specs=[
            pl.BlockSpec((tr, d), lambda r: (r, 0)),
            pl.BlockSpec((tr, d), lambda r: (jnp.maximum(r, n_a_tiles - 1) + 1, 0)),
            pl.BlockSpec((n_mid, d), lambda r: (n_a // n_mid, 0)),
        ],
        out_specs=[
            pl.BlockSpec((tr, d), lambda r: (r, 0)),
            pl.BlockSpec((2 * LANES, d), lambda r: (0, 0)),
        ],
        out_shape=[
            jax.ShapeDtypeStruct((n_main, d), BF16),
            jax.ShapeDtypeStruct((2 * LANES, d), BF16),
        ],
        compiler_params=pltpu.CompilerParams(
            dimension_semantics=("arbitrary",), vmem_limit_bytes=VMEM_LIMIT_BYTES),
        name="w_relayout",
    )(w_t, w_t, w_t)


def _layer(x, c, w_ada, b_ada, g_norm, w_in, q_norm_a, k_norm_a, k_norm_idx, q_norm_b, k_norm_b, w_out):
    bsz, seq_len, d = x.shape
    assert seq_len % MOBA_BLOCK == 0 and seq_len % 1024 == 0 and d % 1024 == 0
    x2 = x.reshape(bsz * seq_len, d)

    mod3 = _modulation(c, w_ada, b_ada).reshape(bsz, 3, d)

    w_main, w_tail = _relayout_weights(jnp.swapaxes(w_in, 0, 1))
    ones = jnp.ones((HEAD_DIM,), F32)
    sm_scale = HEAD_DIM ** -0.5
    gains = jnp.stack([q_norm_a * sm_scale, k_norm_a, ones, ones, ones,
                       q_norm_b * sm_scale, k_norm_b, ones, ones]).reshape(N_SEGS, 1, HEAD_DIM)
    for h in range(N_HEADS):
        slope = _alibi_slope(h, N_HEADS)
        assert float(np.float32(slope).astype(jnp.bfloat16)) == slope, "ALiBi slopes must be exact in bf16"
    assert seq_len // POS_TILE <= 256 and POS_TILE <= 256, "positions must split into bf16-exact parts"
    kaug = _key_features(seq_len)
    gk = jnp.concatenate([k_norm_idx, k_norm_idx]).reshape(1, LANES)

    p, kk, wi = _projection(x2, mod3, g_norm.reshape(1, d), w_main, w_tail, gains, gk, seq_len)
    ya = _dsa_attention(p, kk, wi, kaug, bsz, seq_len)
    yb = _moba_attention(p, kaug, bsz, seq_len)
    out = _out_projection(ya, yb, w_out.astype(BF16), x2, mod3, seq_len)
    return out.reshape(bsz, seq_len, d)


def kernel(x, c, w_ada, b_ada, g_norm, w_in, q_norm_a, k_norm_a, k_norm_idx, q_norm_b, k_norm_b, w_out):
    for i in range(w_ada.shape[0]):
        x = _layer(x, c, w_ada[i], b_ada[i], g_norm[i], w_in[i], q_norm_a[i], k_norm_a[i],
                   k_norm_idx[i], q_norm_b[i], k_norm_b[i], w_out[i])
    return x
```

```python
import functools

import jax
import jax.numpy as jnp
import numpy as np
from jax import lax
from jax.experimental import pallas as pl
from jax.experimental.pallas import tpu as pltpu

F32 = jnp.float32
BF16 = jnp.bfloat16

HEAD_DIM = 128
N_HEADS = 8
D_GROUP = N_HEADS * HEAD_DIM
IDX_HEADS = 16
IDX_DIM = 64
DSA_TOPK_MAX = 256
MOBA_BLOCK = 256
MOBA_TOPK_MAX = 3
RMS_EPS = 1e-6
NEG_INF = -1e30
BIG = 1e30

LANES = 128
VMEM_LIMIT_BYTES = 56 * 1024 * 1024

SEG_QA, SEG_KA, SEG_VA, SEG_ZA, SEG_QIDX, SEG_QB, SEG_KB, SEG_VB, SEG_ZB = range(9)
N_SEGS = 9
BLOCKS_PER_SEG = D_GROUP // LANES


def _alibi_slope(h, n):
    return float(2.0 ** (-8.0 * (h + 1) / n))


def _dot_nt(a, b):
    return lax.dot_general(a, b, (((1,), (1,)), ((), ())), preferred_element_type=F32)


def _mod_kernel(c_ref, w_ref, b_ref, o_ref):
    c = c_ref[...]
    s = c * jax.nn.sigmoid(c)
    o_ref[...] = jnp.dot(s, w_ref[...], preferred_element_type=F32) + b_ref[...]


def _modulation(c, w_ada, b_ada):
    bsz, d = c.shape
    n = w_ada.shape[1]
    tn = 1024
    return pl.pallas_call(
        _mod_kernel,
        grid=(n // tn,),
        in_specs=[
            pl.BlockSpec((bsz, d), lambda j: (0, 0)),
            pl.BlockSpec((d, tn), lambda j: (0, j)),
            pl.BlockSpec((1, tn), lambda j: (0, j)),
        ],
        out_specs=pl.BlockSpec((bsz, tn), lambda j: (0, j)),
        out_shape=jax.ShapeDtypeStruct((bsz, n), F32),
        compiler_params=pltpu.CompilerParams(
            dimension_semantics=("arbitrary",), vmem_limit_bytes=VMEM_LIMIT_BYTES),
        name="adaln_mod",
    )(c, w_ada, b_ada.reshape(1, n))


PROJ_CHUNK = 2 * LANES


def _proj_kernel(x_ref, mod_ref, g_ref, w_ref, wt_ref, gain_ref, gk_ref,
                 p_ref, kk_ref, wi_ref, h_ref):
    j = pl.program_id(1)

    @pl.when(j == 0)
    def _():
        x = x_ref[...]
        ms = jnp.mean(x * x, axis=-1, keepdims=True)
        shift = mod_ref[0, 0:1, :]
        gs = g_ref[...] * (1.0 + mod_ref[0, 1:2, :])
        hb = (x * lax.rsqrt(ms + RMS_EPS) * gs + shift).astype(BF16)
        h_ref[...] = hb
        t = _dot_nt(hb, wt_ref[...])
        tk = t[:, :LANES]
        kms = jnp.mean(tk * tk, axis=-1, keepdims=True)
        kk_ref[...] = (tk * lax.rsqrt(kms + RMS_EPS) * gk_ref[...]).astype(BF16)
        wi_ref[...] = t[:, LANES:]

    def segment(with_norm):
        g = gain_ref[0]
        h = h_ref[...]
        for cc in range(D_GROUP // PROJ_CHUNK):
            acc = _dot_nt(h, w_ref[cc * PROJ_CHUNK:(cc + 1) * PROJ_CHUNK, :])
            for c in range(PROJ_CHUNK // LANES):
                a = acc[:, c * LANES:(c + 1) * LANES]
                if with_norm:
                    ms = jnp.mean(a * a, axis=-1, keepdims=True)
                    a = a * lax.rsqrt(ms + RMS_EPS) * g
                p_ref[cc * (PROJ_CHUNK // LANES) + c] = a.astype(BF16)

    is_norm = (j == SEG_QA) | (j == SEG_KA) | (j == SEG_QB) | (j == SEG_KB)
    pl.when(is_norm)(functools.partial(segment, True))
    pl.when(jnp.logical_not(is_norm))(functools.partial(segment, False))


def _projection(x2, mod3, g_norm, w_main, w_tail, gains, gk, seq_len):
    m, d = x2.shape
    tm = 1024
    tiles_per_seq = seq_len // tm
    return pl.pallas_call(
        _proj_kernel,
        grid=(m // tm, N_SEGS),
        in_specs=[
            pl.BlockSpec((tm, d), lambda i, j: (i, 0)),
            pl.BlockSpec((1, 3, d), lambda i, j: (i // tiles_per_seq, 0, 0)),
            pl.BlockSpec((1, d), lambda i, j: (0, 0)),
            pl.BlockSpec((D_GROUP, d), lambda i, j: (j, 0)),
            pl.BlockSpec((2 * LANES, d), lambda i, j: (0, 0)),
            pl.BlockSpec((1, 1, LANES), lambda i, j: (j, 0, 0)),
            pl.BlockSpec((1, LANES), lambda i, j: (0, 0)),
        ],
        out_specs=[
            pl.BlockSpec((BLOCKS_PER_SEG, tm, LANES), lambda i, j: (j, i, 0)),
            pl.BlockSpec((tm, LANES), lambda i, j: (i, 0)),
            pl.BlockSpec((tm, LANES), lambda i, j: (i, 0)),
        ],
        out_shape=[
            jax.ShapeDtypeStruct((N_SEGS * BLOCKS_PER_SEG, m, LANES), BF16),
            jax.ShapeDtypeStruct((m, LANES), BF16),
            jax.ShapeDtypeStruct((m, LANES), F32),
        ],
        scratch_shapes=[pltpu.VMEM((tm, d), BF16)],
        compiler_params=pltpu.CompilerParams(
            dimension_semantics=("arbitrary", "arbitrary"), vmem_limit_bytes=VMEM_LIMIT_BYTES),
        name="in_proj",
    )(x2, mod3, g_norm, w_main, w_tail, gains, gk)


POS_TILE = MOBA_BLOCK
AUG_KLOC, AUG_KBLK, AUG_ONE_LOC, AUG_ONE_BLK, AUG_ONEHOT0 = 0, 1, 2, 3, 8


def _key_features(seq_len):
    pos = np.arange(seq_len)
    f = np.zeros((seq_len, LANES), np.float32)
    f[:, AUG_KLOC] = pos % POS_TILE
    f[:, AUG_KBLK] = pos // POS_TILE
    f[:, AUG_ONE_LOC] = 1.0
    f[:, AUG_ONE_BLK] = 1.0
    f[pos, AUG_ONEHOT0 + pos // POS_TILE] = 1.0
    return jnp.asarray(f, BF16)


def _alibi_query_features(slope, q_tile, tq):
    lane = lax.broadcasted_iota(jnp.int32, (tq, LANES), 1)
    row = lax.broadcasted_iota(jnp.int32, (tq, LANES), 0).astype(F32)
    f = jnp.where(lane == AUG_KLOC, slope, 0.0)
    f = jnp.where(lane == AUG_KBLK, slope * POS_TILE, f)
    f = jnp.where(lane == AUG_ONE_LOC, -slope * row, f)
    return jnp.where(lane == AUG_ONE_BLK, (-slope * POS_TILE) * q_tile.astype(F32), f)


def _fold_rows(x, op):
    r = x.reshape(x.shape[0] // 8, 8, x.shape[1])
    while r.shape[0] > 1:
        half = r.shape[0] // 2
        r = op(r[:half], r[half:])
    return r[0]


TILE_CHUNKS = (4, 2, 1)


def _for_tile_chunks(n_tiles, fn, chunks=TILE_CHUNKS):
    big = chunks[0]
    shift = big.bit_length() - 1

    def many(c, carry):
        fn(c * big, big)
        return carry

    n_big = lax.shift_right_logical(n_tiles, shift)
    lax.fori_loop(0, n_big, many, 0)
    done = n_big * big
    for size in chunks[1:]:
        has = (n_tiles & size) != 0

        @pl.when(has)
        def _(done=done, size=size):
            fn(done, size)

        done = done + (n_tiles & size)


def _softmax_pass_a(n_tiles, tile, score_rows, s_ref, mx_ref):
    tq = s_ref.shape[-1]

    def pass_a(first, count):
        for h in range(N_HEADS):
            s = score_rows(h, first, count)
            s_ref[h, pl.ds(first, count)] = s.reshape(count, tile, tq)
            mx_ref[h] = jnp.maximum(mx_ref[h], _fold_rows(s, jnp.maximum))

    _for_tile_chunks(n_tiles, pass_a)
    return [jnp.max(mx_ref[h], axis=0, keepdims=True) for h in range(N_HEADS)]


def _softmax_pass_b(n_tiles, tile, m, value_cols, s_ref, ls_ref, acc_ref):
    tq = acc_ref.shape[-1]
    ls_ref[...] = jnp.zeros(ls_ref.shape, F32)
    acc_ref[...] = jnp.zeros(acc_ref.shape, F32)

    def pass_b(first, count):
        for h in range(N_HEADS):
            p = jnp.exp(s_ref[h, pl.ds(first, count)].reshape(count * tile, tq) - m[h])
            ls_ref[h] += _fold_rows(p, jnp.add)
            vt = jnp.concatenate([value_cols(h, first + u) for u in range(count)], axis=1)
            acc_ref[h] += jnp.dot(vt, p.astype(BF16), preferred_element_type=F32)

    _for_tile_chunks(n_tiles, pass_b)
    return [jnp.sum(ls_ref[h], axis=0, keepdims=True) for h in range(N_HEADS)]


def _gated_output(h, z_ref, l, acc_ref, o_ref):
    z = z_ref[h].astype(F32)
    o = (acc_ref[h] / l).T * (z * jax.nn.sigmoid(z))
    o_ref[:, h * HEAD_DIM:(h + 1) * HEAD_DIM] = o.astype(BF16)


def _transpose_values(v_ref, vt_ref, n_tiles, tile):
    for h in range(N_HEADS):
        for j in range(n_tiles):
            vt_ref[h, j] = v_ref[h, j * tile:(j + 1) * tile, :].astype(F32).T.astype(BF16)


DSA_TQ = 256
DSA_TK = POS_TILE
SEARCH_STEPS = 4


def _dsa_kernel(q_ref, k_ref, v_ref, z_ref, qi_ref, kk_ref, wi_ref, kaug_ref, o_ref,
                vt_ref, qs_ref, sc_ref, ext_ref, t_ref, j_ref, qcat_ref, s_ref, mx_ref, ls_ref, acc_ref,
                *, top_k, seq_len):
    tq, tk = DSA_TQ, DSA_TK
    i = pl.program_id(1)
    t0 = i * tq
    nk = i + 1
    idx_scale = (IDX_DIM ** -0.5) * (IDX_HEADS ** -0.5)

    @pl.when(i == 0)
    def _():
        _transpose_values(v_ref, vt_ref, seq_len // tk, tk)

    lane = lax.broadcasted_iota(jnp.int32, (tq, LANES), 1)
    key_loc = lax.broadcasted_iota(jnp.int32, (tk, tq), 0).astype(F32)
    qry_loc = lax.broadcasted_iota(jnp.int32, (tk, tq), 1).astype(F32)

    for c in range(BLOCKS_PER_SEG):
        qp = qi_ref[c].astype(F32)
        qs_ref[2 * c] = jnp.where(lane < IDX_DIM, qp, 0.0).astype(BF16)
        qs_ref[2 * c + 1] = jnp.where(lane >= IDX_DIM, qp, 0.0).astype(BF16)
    w_t = (wi_ref[...] * idx_scale).T
    w_rows = [w_t[h:h + 1, :] for h in range(IDX_HEADS)]

    def score_tiles(first, count):
        for u in range(count):
            kt = first + u
            kk = kk_ref[pl.ds(pl.multiple_of(kt * tk, tk), tk), :]
            acc = jnp.zeros((tk, tq), F32)
            for h in range(IDX_HEADS):
                acc = acc + w_rows[h] * jnp.maximum(_dot_nt(kk, qs_ref[h]), 0.0)
            off = (kt * tk - t0).astype(F32)
            causal = (key_loc + off) <= qry_loc
            sc_ref[kt] = jnp.where(causal, acc, -jnp.inf)
            ext_ref[0] = jnp.minimum(ext_ref[0], _fold_rows(jnp.where(causal, acc, BIG), jnp.minimum))
            ext_ref[1] = jnp.maximum(ext_ref[1], _fold_rows(jnp.where(causal, acc, NEG_INF), jnp.maximum))

    ext_ref[0] = jnp.full((8, tq), BIG, F32)
    ext_ref[1] = jnp.full((8, tq), NEG_INF, F32)
    _for_tile_chunks(nk, score_tiles)

    t_ref[...] = jnp.full((1, tq), NEG_INF, F32)
    j_ref[...] = jnp.full((1, tq), -1.0, F32)

    def count(pred):
        def body(kt, acc):
            off = (kt * tk).astype(F32)
            return acc + _fold_rows(jnp.where(pred(sc_ref[kt], key_loc + off), 1.0, 0.0), jnp.add)
        return jnp.sum(lax.fori_loop(0, nk, body, jnp.zeros((8, tq), F32)), axis=0, keepdims=True)

    @pl.when(t0 + tq > top_k)
    def _():
        kf = float(top_k)
        qpos = lax.broadcasted_iota(jnp.int32, (1, tq), 1) + t0
        done0 = jnp.where(qpos + 1 <= top_k, 1.0, 0.0)

        lo0 = jnp.min(ext_ref[0], axis=0, keepdims=True)
        hi0 = jnp.max(ext_ref[1], axis=0, keepdims=True)

        def step(x, degen, state):
            lo, hi, thr, done, tie = state
            c = count(lambda s, pos: s > x)
            active = done == 0.0
            live = jnp.logical_and(active, jnp.logical_not(degen))
            found = jnp.logical_and(live, c == kf)
            new_tie = jnp.logical_and(active, degen)
            thr = jnp.where(found, x, jnp.where(new_tie, hi, thr))
            tie = jnp.where(new_tie, 1.0, tie)
            done = jnp.where(jnp.logical_or(found, new_tie), 1.0, done)
            lo = jnp.where(jnp.logical_and(live, c > kf), x, lo)
            hi = jnp.where(jnp.logical_and(live, c < kf), x, hi)
            return lo, hi, thr, done, tie

        state0 = (lo0, hi0, jnp.full((1, tq), NEG_INF, F32), done0, jnp.zeros((1, tq), F32))

        def cond(carry):
            return carry[1] > 0.0

        def body(carry):
            state, _ = carry
            for _ in range(SEARCH_STEPS):
                lo, hi = state[0], state[1]
                mid = 0.5 * lo + 0.5 * hi
                degen = jnp.logical_or(mid <= lo, mid >= hi)
                state = step(mid, degen, state)
            return state, jnp.max(1.0 - state[3])

        (lo, hi, thr_loop, done, tie), _ = lax.while_loop(cond, body, (state0, jnp.max(1.0 - state0[3])))
        t_ref[...] = thr_loop

        @pl.when(jnp.max(tie) > 0.0)
        def _():
            reach_hi = count(lambda s, pos: s >= hi)
            thr = jnp.where(jnp.logical_and(tie > 0.0, reach_hi < kf), lo, thr_loop)
            t_ref[...] = thr
            need = kf - count(lambda s, pos: s > thr)
            n_bits = int(np.ceil(np.log2(seq_len + 1)))

            def jstep(_, carry):
                jlo, jhi = carry
                jmid = jnp.floor((jlo + jhi) * 0.5)
                c = count(lambda s, pos: jnp.logical_and(s == thr, pos <= jmid))
                ge = c >= need
                return jnp.where(ge, jlo, jmid), jnp.where(ge, jmid, jhi)

            _, jhi = lax.fori_loop(0, n_bits, jstep,
                                   (jnp.full((1, tq), -1.0, F32),
                                    jnp.full((1, tq), float(seq_len - 1), F32)))
            j_ref[...] = jnp.where(tie > 0.0, jhi, -1.0)

    def mask_tile(kt, carry):
        s = sc_ref[kt]
        thr = t_ref[...]
        off = (kt * tk).astype(F32)
        sel = jnp.logical_or(s > thr, jnp.logical_and(s == thr, (key_loc + off) <= j_ref[...]))
        sc_ref[kt] = jnp.where(sel, 0.0, NEG_INF)
        return carry

    lax.fori_loop(0, nk, mask_tile, 0)

    for h in range(N_HEADS):
        feat = _alibi_query_features(_alibi_slope(h, N_HEADS), i, tq)
        qcat_ref[h] = jnp.concatenate([q_ref[h], feat.astype(BF16)], axis=1)

    def score_rows(h, first, count):
        ks = pl.ds(pl.multiple_of(first * tk, tk), count * tk)
        kcat = jnp.concatenate([k_ref[h, ks, :], kaug_ref[ks, :]], axis=1)
        return _dot_nt(kcat, qcat_ref[h]) + sc_ref[pl.ds(first, count)].reshape(count * tk, tq)

    mx_ref[...] = jnp.full(mx_ref.shape, NEG_INF, F32)
    value_cols = lambda h, kt: vt_ref[h, kt]
    m = _softmax_pass_a(nk, tk, score_rows, s_ref, mx_ref)
    lsum = _softmax_pass_b(nk, tk, m, value_cols, s_ref, ls_ref, acc_ref)
    for h in range(N_HEADS):
        _gated_output(h, z_ref, lsum[h], acc_ref, o_ref)


def _dsa_attention(p, kk, wi, kaug, bsz, seq_len):
    tq, tk = DSA_TQ, DSA_TK
    nq = seq_len // tq
    nkt = seq_len // tk
    m = bsz * seq_len
    top_k = min(DSA_TOPK_MAX, seq_len // 4)
    seg = lambda s: (lambda b, i: (s, b * nq + i, 0))
    seg_full = lambda s: (lambda b, i: (s, b, 0))
    kern = functools.partial(_dsa_kernel, top_k=top_k, seq_len=seq_len)
    return pl.pallas_call(
        kern,
        grid=(bsz, nq),
        in_specs=[
            pl.BlockSpec((BLOCKS_PER_SEG, tq, LANES), seg(SEG_QA)),
            pl.BlockSpec((BLOCKS_PER_SEG, seq_len, LANES), seg_full(SEG_KA)),
            pl.BlockSpec((BLOCKS_PER_SEG, seq_len, LANES), seg_full(SEG_VA)),
            pl.BlockSpec((BLOCKS_PER_SEG, tq, LANES), seg(SEG_ZA)),
            pl.BlockSpec((BLOCKS_PER_SEG, tq, LANES), seg(SEG_QIDX)),
            pl.BlockSpec((seq_len, LANES), lambda b, i: (b, 0)),
            pl.BlockSpec((tq, LANES), lambda b, i: (b * nq + i, 0)),
            pl.BlockSpec((seq_len, LANES), lambda b, i: (0, 0)),
        ],
        out_specs=pl.BlockSpec((tq, D_GROUP), lambda b, i: (b * nq + i, 0)),
        out_shape=jax.ShapeDtypeStruct((m, D_GROUP), BF16),
        scratch_shapes=[
            pltpu.VMEM((N_HEADS, nkt, HEAD_DIM, tk), BF16),
            pltpu.VMEM((IDX_HEADS, tq, LANES), BF16),
            pltpu.VMEM((nkt, tk, tq), F32),
            pltpu.VMEM((2, 8, tq), F32),
            pltpu.VMEM((1, tq), F32),
            pltpu.VMEM((1, tq), F32),
            pltpu.VMEM((N_HEADS, tq, 2 * LANES), BF16),
            pltpu.VMEM((N_HEADS, nkt, tk, tq), F32),
            pltpu.VMEM((N_HEADS, 8, tq), F32),
            pltpu.VMEM((N_HEADS, 8, tq), F32),
            pltpu.VMEM((N_HEADS, HEAD_DIM, tq), F32),
        ],
        compiler_params=pltpu.CompilerParams(
            dimension_semantics=("arbitrary", "arbitrary"), vmem_limit_bytes=VMEM_LIMIT_BYTES),
        name="dsa_attn",
    )(p, p, p, p, p, kk, wi, kaug)


def _moba_kernel(q_ref, k_ref, v_ref, z_ref, kaug_ref, o_ref,
                 kmean_ref, vt_ref, selb_ref, qcat_ref, s_ref, mx_ref, ls_ref, acc_ref, *, n_blocks):
    bs = MOBA_BLOCK
    n = pl.program_id(1)
    top_k = min(MOBA_TOPK_MAX, n_blocks - 1)

    @pl.when(n == 0)
    def _():
        kmean_ref[...] = jnp.zeros(kmean_ref.shape, F32)
        for h in range(N_HEADS):
            for j in range(n_blocks):
                kb = k_ref[h, j * bs:(j + 1) * bs, :].astype(F32)
                kmean_ref[h, j:j + 1, :] = jnp.mean(kb, axis=0, keepdims=True)
        _transpose_values(v_ref, vt_ref, n_blocks, bs)

    nb_rows = -(-n_blocks // 8) * 8
    blk = lax.broadcasted_iota(jnp.int32, (nb_rows, bs), 0)
    past = blk < n

    selb_ref[...] = jnp.zeros(selb_ref.shape, F32)
    for h in range(N_HEADS):
        q = q_ref[h]
        g = _dot_nt(kmean_ref[h].astype(BF16), q)[:nb_rows]
        for j in range(n_blocks):
            gj = g[j:j + 1, :]
            beats = jnp.logical_or(g > gj, jnp.logical_and(g == gj, blk < j))
            rank = jnp.sum(jnp.where(jnp.logical_and(beats, past), 1.0, 0.0), axis=0, keepdims=True)
            dropped = jnp.logical_and(rank >= float(top_k), j < n)
            selb_ref[AUG_ONEHOT0 + j:AUG_ONEHOT0 + j + 1, :] = jnp.where(dropped, NEG_INF, 0.0)
        feat = _alibi_query_features(_alibi_slope(h, N_HEADS), n, bs) + selb_ref[...].T
        qcat_ref[h] = jnp.concatenate([q, feat.astype(BF16)], axis=1)

    key_loc = lax.broadcasted_iota(jnp.int32, (bs, bs), 0)
    qry_loc = lax.broadcasted_iota(jnp.int32, (bs, bs), 1)
    causal = key_loc <= qry_loc

    def score_rows(h, first, count):
        ks = pl.ds(pl.multiple_of(first * bs, bs), count * bs)
        kcat = jnp.concatenate([k_ref[h, ks, :], kaug_ref[ks, :]], axis=1)
        return _dot_nt(kcat, qcat_ref[h])

    for h in range(N_HEADS):
        s = jnp.where(causal, score_rows(h, n, 1), NEG_INF)
        s_ref[h, n] = s
        mx_ref[h] = _fold_rows(s, jnp.maximum)
    value_cols = lambda h, j: vt_ref[h, j]
    m = _softmax_pass_a(n, bs, score_rows, s_ref, mx_ref)
    lsum = _softmax_pass_b(n + 1, bs, m, value_cols, s_ref, ls_ref, acc_ref)
    for h in range(N_HEADS):
        _gated_output(h, z_ref, lsum[h], acc_ref, o_ref)


def _moba_attention(p, kaug, bsz, seq_len):
    bs = MOBA_BLOCK
    nb = seq_len // bs
    m = bsz * seq_len
    seg = lambda s: (lambda b, i: (s, b * nb + i, 0))
    seg_full = lambda s: (lambda b, i: (s, b, 0))
    kern = functools.partial(_moba_kernel, n_blocks=nb)
    nb_pad = max(16, nb)
    assert AUG_ONEHOT0 + nb <= LANES
    return pl.pallas_call(
        kern,
        grid=(bsz, nb),
        in_specs=[
            pl.BlockSpec((BLOCKS_PER_SEG, bs, LANES), seg(SEG_QB)),
            pl.BlockSpec((BLOCKS_PER_SEG, seq_len, LANES), seg_full(SEG_KB)),
            pl.BlockSpec((BLOCKS_PER_SEG, seq_len, LANES), seg_full(SEG_VB)),
            pl.BlockSpec((BLOCKS_PER_SEG, bs, LANES), seg(SEG_ZB)),
            pl.BlockSpec((seq_len, LANES), lambda b, i: (0, 0)),
        ],
        out_specs=pl.BlockSpec((bs, D_GROUP), lambda b, i: (b * nb + i, 0)),
        out_shape=jax.ShapeDtypeStruct((m, D_GROUP), BF16),
        scratch_shapes=[
            pltpu.VMEM((N_HEADS, nb_pad, HEAD_DIM), F32),
            pltpu.VMEM((N_HEADS, nb, HEAD_DIM, bs), BF16),
            pltpu.VMEM((LANES, bs), F32),
            pltpu.VMEM((N_HEADS, bs, 2 * LANES), BF16),
            pltpu.VMEM((N_HEADS, nb, bs, bs), F32),
            pltpu.VMEM((N_HEADS, 8, bs), F32),
            pltpu.VMEM((N_HEADS, 8, bs), F32),
            pltpu.VMEM((N_HEADS, HEAD_DIM, bs), F32),
        ],
        compiler_params=pltpu.CompilerParams(
            dimension_semantics=("arbitrary", "arbitrary"), vmem_limit_bytes=VMEM_LIMIT_BYTES),
        name="moba_attn",
    )(p, p, p, p, kaug)


OUT_CHUNK = 512


def _out_kernel(ya_ref, yb_ref, wa_ref, wb_ref, x_ref, mod_ref, o_ref):
    ya = ya_ref[...]
    yb = yb_ref[...]
    for c in range(o_ref.shape[1] // OUT_CHUNK):
        cs = slice(c * OUT_CHUNK, (c + 1) * OUT_CHUNK)
        y = jnp.dot(ya, wa_ref[:, cs], preferred_element_type=F32)
        y = y + jnp.dot(yb, wb_ref[:, cs], preferred_element_type=F32)
        o_ref[:, cs] = x_ref[:, cs] + mod_ref[0, 2:3, cs] * y


def _out_projection(ya, yb, w_out_bf, x2, mod3, seq_len):
    m, d = x2.shape
    tm = 512
    tiles_per_seq = seq_len // tm
    return pl.pallas_call(
        _out_kernel,
        grid=(m // tm,),
        in_specs=[
            pl.BlockSpec((tm, D_GROUP), lambda i: (i, 0)),
            pl.BlockSpec((tm, D_GROUP), lambda i: (i, 0)),
            pl.BlockSpec((D_GROUP, d), lambda i: (0, 0)),
            pl.BlockSpec((D_GROUP, d), lambda i: (1, 0)),
            pl.BlockSpec((tm, d), lambda i: (i, 0)),
            pl.BlockSpec((1, 3, d), lambda i: (i // tiles_per_seq, 0, 0)),
        ],
        out_specs=pl.BlockSpec((tm, d), lambda i: (i, 0)),
        out_shape=jax.ShapeDtypeStruct((m, d), F32),
        compiler_params=pltpu.CompilerParams(
            dimension_semantics=("arbitrary",), vmem_limit_bytes=VMEM_LIMIT_BYTES),
        name="out_proj",
    )(ya, yb, w_out_bf, w_out_bf, x2, mod3)


RELAYOUT_ROWS = 512


def _relayout_kernel(a_ref, b_ref, mid_ref, main_ref, tail_ref, *, n_a_tiles, n_mid):
    r = pl.program_id(0)

    @pl.when(r < n_a_tiles)
    def _():
        main_ref[...] = a_ref[...].astype(BF16)

    @pl.when(r >= n_a_tiles)
    def _():
        main_ref[:RELAYOUT_ROWS - n_mid, :] = a_ref[n_mid:, :].astype(BF16)
        main_ref[RELAYOUT_ROWS - n_mid:, :] = b_ref[:n_mid, :].astype(BF16)

    kidx = mid_ref[:IDX_DIM, :].astype(BF16)
    tail_ref[0:IDX_DIM, :] = kidx
    tail_ref[IDX_DIM:2 * IDX_DIM, :] = kidx
    tail_ref[2 * IDX_DIM:2 * IDX_DIM + IDX_HEADS, :] = mid_ref[IDX_DIM:IDX_DIM + IDX_HEADS, :].astype(BF16)
    tail_ref[2 * IDX_DIM + IDX_HEADS:, :] = jnp.zeros(
        (tail_ref.shape[0] - 2 * IDX_DIM - IDX_HEADS, tail_ref.shape[1]), BF16)


def _relayout_weights(w_t):
    n_in, d = w_t.shape
    n_a = 4 * D_GROUP + IDX_HEADS * IDX_DIM
    n_mid = IDX_DIM + IDX_HEADS
    tr = RELAYOUT_ROWS
    assert n_in == n_a + n_mid + 4 * D_GROUP and n_a % tr == 0 and (4 * D_GROUP) % tr == 0
    assert n_mid % 16 == 0 and n_mid < tr and n_a % n_mid == 0
    n_main = n_in - n_mid
    n_a_tiles = n_a // tr
    return pl.pallas_call(
        functools.partial(_relayout_kernel, n_a_tiles=n_a_tiles, n_mid=n_mid),
        grid=(n_main // tr,),
        in_specs=[
            pl.BlockSpec((tr, d), lambda r: (r, 0)),
            pl.BlockSpec((tr, d), lambda r: (jnp.maximum(r, n_a_tiles - 1) + 1, 0)),
            pl.BlockSpec((n_mid, d), lambda r: (n_a // n_mid, 0)),
        ],
        out_specs=[
            pl.BlockSpec((tr, d), lambda r: (r, 0)),
            pl.BlockSpec((2 * LANES, d), lambda r: (0, 0)),
        ],
        out_shape=[
            jax.ShapeDtypeStruct((n_main, d), BF16),
            jax.ShapeDtypeStruct((2 * LANES, d), BF16),
        ],
        compiler_params=pltpu.CompilerParams(
            dimension_semantics=("arbitrary",), vmem_limit_bytes=VMEM_LIMIT_BYTES),
        name="w_relayout",
    )(w_t, w_t, w_t)


def _layer(x, c, w_ada, b_ada, g_norm, w_in, q_norm_a, k_norm_a, k_norm_idx, q_norm_b, k_norm_b, w_out):
    bsz, seq_len, d = x.shape
    assert seq_len % MOBA_BLOCK == 0 and seq_len % 1024 == 0 and d % 1024 == 0
    x2 = x.reshape(bsz * seq_len, d)

    mod3 = _modulation(c, w_ada, b_ada).reshape(bsz, 3, d)

    w_main, w_tail = _relayout_weights(jnp.swapaxes(w_in, 0, 1))
    ones = jnp.ones((HEAD_DIM,), F32)
    sm_scale = HEAD_DIM ** -0.5
    gains = jnp.stack([q_norm_a * sm_scale, k_norm_a, ones, ones, ones,
                       q_norm_b * sm_scale, k_norm_b, ones, ones]).reshape(N_SEGS, 1, HEAD_DIM)
    for h in range(N_HEADS):
        slope = _alibi_slope(h, N_HEADS)
        assert float(np.float32(slope).astype(jnp.bfloat16)) == slope, "ALiBi slopes must be exact in bf16"
    assert seq_len // POS_TILE <= 256 and POS_TILE <= 256, "positions must split into bf16-exact parts"
    kaug = _key_features(seq_len)
    gk = jnp.concatenate([k_norm_idx, k_norm_idx]).reshape(1, LANES)

    p, kk, wi = _projection(x2, mod3, g_norm.reshape(1, d), w_main, w_tail, gains, gk, seq_len)
    ya = _dsa_attention(p, kk, wi, kaug, bsz, seq_len)
    yb = _moba_attention(p, kaug, bsz, seq_len)
    out = _out_projection(ya, yb, w_out.astype(BF16), x2, mod3, seq_len)
    return out.reshape(bsz, seq_len, d)


def kernel(x, c, w_ada, b_ada, g_norm, w_in, q_norm_a, k_norm_a, k_norm_idx, q_norm_b, k_norm_b, w_out):
    for i in range(w_ada.shape[0]):
        x = _layer(x, c, w_ada[i], b_ada[i], g_norm[i], w_in[i], q_norm_a[i], k_norm_a[i],
                   k_norm_idx[i], q_norm_b[i], k_norm_b[i], w_out[i])
    return x
```

```python
import functools

import jax
import jax.numpy as jnp
import numpy as np
from jax import lax
from jax.experimental import pallas as pl
from jax.experimental.pallas import tpu as pltpu

F32 = jnp.float32
BF16 = jnp.bfloat16

HEAD_DIM = 128
N_HEADS = 8
D_GROUP = N_HEADS * HEAD_DIM
IDX_HEADS = 16
IDX_DIM = 64
DSA_TOPK_MAX = 256
MOBA_BLOCK = 256
MOBA_TOPK_MAX = 3
RMS_EPS = 1e-6
NEG_INF = -1e30
BIG = 1e30
LOG2E = 1.4426950408889634

LANES = 128
VMEM_LIMIT_BYTES = 56 * 1024 * 1024

SEG_QA, SEG_KA, SEG_VA, SEG_ZA, SEG_QIDX, SEG_QB, SEG_KB, SEG_VB, SEG_ZB = range(9)
N_SEGS = 9
BLOCKS_PER_SEG = D_GROUP // LANES


def _alibi_slope(h, n):
    return float(2.0 ** (-8.0 * (h + 1) / n))


def _dot_nt(a, b):
    return lax.dot_general(a, b, (((1,), (1,)), ((), ())), preferred_element_type=F32)


def _mod_kernel(c_ref, w_ref, b_ref, o_ref):
    c = c_ref[...]
    s = c * jax.nn.sigmoid(c)
    o_ref[...] = jnp.dot(s, w_ref[...], preferred_element_type=F32) + b_ref[...]


def _modulation(c, w_ada, b_ada):
    bsz, d = c.shape
    n = w_ada.shape[1]
    tn = 1024
    return pl.pallas_call(
        _mod_kernel,
        grid=(n // tn,),
        in_specs=[
            pl.BlockSpec((bsz, d), lambda j: (0, 0)),
            pl.BlockSpec((d, tn), lambda j: (0, j)),
            pl.BlockSpec((1, tn), lambda j: (0, j)),
        ],
        out_specs=pl.BlockSpec((bsz, tn), lambda j: (0, j)),
        out_shape=jax.ShapeDtypeStruct((bsz, n), F32),
        compiler_params=pltpu.CompilerParams(
            dimension_semantics=("arbitrary",), vmem_limit_bytes=VMEM_LIMIT_BYTES),
        name="adaln_mod",
    )(c, w_ada, b_ada.reshape(1, n))


PROJ_CHUNK = 2 * LANES


def _proj_kernel(x_ref, mod_ref, g_ref, w_ref, wt_ref, gain_ref, gk_ref,
                 p_ref, kk_ref, wi_ref, h_ref):
    j = pl.program_id(1)

    @pl.when(j == 0)
    def _():
        x = x_ref[...]
        ms = jnp.mean(x * x, axis=-1, keepdims=True)
        shift = mod_ref[0, 0:1, :]
        gs = g_ref[...] * (1.0 + mod_ref[0, 1:2, :])
        hb = (x * lax.rsqrt(ms + RMS_EPS) * gs + shift).astype(BF16)
        h_ref[...] = hb
        t = _dot_nt(hb, wt_ref[...])
        tk = t[:, :LANES]
        kms = jnp.mean(tk * tk, axis=-1, keepdims=True)
        kk_ref[...] = (tk * lax.rsqrt(kms + RMS_EPS) * gk_ref[...]).astype(BF16)
        wi_ref[...] = t[:, LANES:]

    def segment(with_norm):
        g = gain_ref[0]
        h = h_ref[...]
        for cc in range(D_GROUP // PROJ_CHUNK):
            acc = _dot_nt(h, w_ref[cc * PROJ_CHUNK:(cc + 1) * PROJ_CHUNK, :])
            for c in range(PROJ_CHUNK // LANES):
                a = acc[:, c * LANES:(c + 1) * LANES]
                if with_norm:
                    ms = jnp.mean(a * a, axis=-1, keepdims=True)
                    a = a * lax.rsqrt(ms + RMS_EPS) * g
                p_ref[cc * (PROJ_CHUNK // LANES) + c] = a.astype(BF16)

    is_norm = (j == SEG_QA) | (j == SEG_KA) | (j == SEG_QB) | (j == SEG_KB)
    pl.when(is_norm)(functools.partial(segment, True))
    pl.when(jnp.logical_not(is_norm))(functools.partial(segment, False))


def _projection(x2, mod3, g_norm, w_main, w_tail, gains, gk, seq_len):
    m, d = x2.shape
    tm = 1024
    tiles_per_seq = seq_len // tm
    return pl.pallas_call(
        _proj_kernel,
        grid=(m // tm, N_SEGS),
        in_specs=[
            pl.BlockSpec((tm, d), lambda i, j: (i, 0)),
            pl.BlockSpec((1, 3, d), lambda i, j: (i // tiles_per_seq, 0, 0)),
            pl.BlockSpec((1, d), lambda i, j: (0, 0)),
            pl.BlockSpec((D_GROUP, d), lambda i, j: (j, 0)),
            pl.BlockSpec((2 * LANES, d), lambda i, j: (0, 0)),
            pl.BlockSpec((1, 1, LANES), lambda i, j: (j, 0, 0)),
            pl.BlockSpec((1, LANES), lambda i, j: (0, 0)),
        ],
        out_specs=[
            pl.BlockSpec((BLOCKS_PER_SEG, tm, LANES), lambda i, j: (j, i, 0)),
            pl.BlockSpec((tm, LANES), lambda i, j: (i, 0)),
            pl.BlockSpec((tm, LANES), lambda i, j: (i, 0)),
        ],
        out_shape=[
            jax.ShapeDtypeStruct((N_SEGS * BLOCKS_PER_SEG, m, LANES), BF16),
            jax.ShapeDtypeStruct((m, LANES), BF16),
            jax.ShapeDtypeStruct((m, LANES), F32),
        ],
        scratch_shapes=[pltpu.VMEM((tm, d), BF16)],
        compiler_params=pltpu.CompilerParams(
            dimension_semantics=("arbitrary", "arbitrary"), vmem_limit_bytes=VMEM_LIMIT_BYTES),
        name="in_proj",
    )(x2, mod3, g_norm, w_main, w_tail, gains, gk)


POS_TILE = MOBA_BLOCK
AUG_KLOC, AUG_KBLK, AUG_ONE_LOC, AUG_ONE_BLK, AUG_ONEHOT0 = 0, 1, 2, 3, 8


def _key_features(seq_len):
    pos = np.arange(seq_len)
    f = np.zeros((seq_len, LANES), np.float32)
    f[:, AUG_KLOC] = pos % POS_TILE
    f[:, AUG_KBLK] = pos // POS_TILE
    f[:, AUG_ONE_LOC] = 1.0
    f[:, AUG_ONE_BLK] = 1.0
    f[pos, AUG_ONEHOT0 + pos // POS_TILE] = 1.0
    return jnp.asarray(f, BF16)


def _alibi_query_features(slope, q_tile, tq):
    lane = lax.broadcasted_iota(jnp.int32, (tq, LANES), 1)
    row = lax.broadcasted_iota(jnp.int32, (tq, LANES), 0).astype(F32)
    f = jnp.where(lane == AUG_KLOC, slope, 0.0)
    f = jnp.where(lane == AUG_KBLK, slope * POS_TILE, f)
    f = jnp.where(lane == AUG_ONE_LOC, -slope * row, f)
    return jnp.where(lane == AUG_ONE_BLK, (-slope * POS_TILE) * q_tile.astype(F32), f)


FOLD_CHAINS = 4


def _fold_rows(x, op):
    r = x.reshape(x.shape[0] // 8, 8, x.shape[1])
    n_acc = min(FOLD_CHAINS, r.shape[0])
    accs = [r[k] for k in range(n_acc)]
    for g in range(n_acc, r.shape[0]):
        accs[g % n_acc] = op(accs[g % n_acc], r[g])
    while len(accs) > 1:
        accs = [op(accs[k], accs[k + 1]) for k in range(0, len(accs) - 1, 2)] + accs[len(accs) - len(accs) % 2:]
    return accs[0]


TILE_CHUNKS = (4, 2, 1)


def _for_tile_chunks(n_tiles, fn, chunks=TILE_CHUNKS):
    big = chunks[0]
    shift = big.bit_length() - 1

    def many(c, carry):
        fn(c * big, big)
        return carry

    n_big = lax.shift_right_logical(n_tiles, shift)
    lax.fori_loop(0, n_big, many, 0)
    done = n_big * big
    for size in chunks[1:]:
        has = (n_tiles & size) != 0

        @pl.when(has)
        def _(done=done, size=size):
            fn(done, size)

        done = done + (n_tiles & size)


def _softmax_pass_a(n_tiles, tile, score_rows, s_ref, mx_ref):
    tq = s_ref.shape[-1]

    def pass_a(first, count):
        for h in range(N_HEADS):
            s = score_rows(h, first, count) * LOG2E
            s_ref[h, pl.ds(first, count)] = s.reshape(count, tile, tq)
            mx_ref[h] = jnp.maximum(mx_ref[h], _fold_rows(s, jnp.maximum))

    _for_tile_chunks(n_tiles, pass_a)
    return [jnp.max(mx_ref[h], axis=0, keepdims=True) for h in range(N_HEADS)]


def _softmax_pass_b(n_tiles, tile, m, value_cols, s_ref, ls_ref, acc_ref):
    tq = acc_ref.shape[-1]
    ls_ref[...] = jnp.zeros(ls_ref.shape, F32)
    acc_ref[...] = jnp.zeros(acc_ref.shape, F32)

    def pass_b(first, count):
        for h in range(N_HEADS):
            p = jnp.exp2(s_ref[h, pl.ds(first, count)].reshape(count * tile, tq) - m[h])
            ls_ref[h] += _fold_rows(p, jnp.add)
            vt = jnp.concatenate([value_cols(h, first + u) for u in range(count)], axis=1)
            acc_ref[h] += jnp.dot(vt, p.astype(BF16), preferred_element_type=F32)

    _for_tile_chunks(n_tiles, pass_b)
    return [jnp.sum(ls_ref[h], axis=0, keepdims=True) for h in range(N_HEADS)]


def _gated_output(h, z_ref, l, acc_ref, o_ref):
    z = z_ref[h].astype(F32)
    o = (acc_ref[h] / l).T * (z * jax.nn.sigmoid(z))
    o_ref[:, h * HEAD_DIM:(h + 1) * HEAD_DIM] = o.astype(BF16)


def _transpose_values(v_ref, vt_ref, n_tiles, tile):
    for h in range(N_HEADS):
        for j in range(n_tiles):
            vt_ref[h, j] = v_ref[h, j * tile:(j + 1) * tile, :].astype(F32).T.astype(BF16)


DSA_TQ = 256
DSA_TK = POS_TILE
SEARCH_STEPS = 4


def _dsa_kernel(q_ref, k_ref, v_ref, z_ref, qi_ref, kk_ref, wi_ref, kaug_ref, o_ref,
                vt_ref, qs_ref, sc_ref, ext_ref, t_ref, j_ref, qcat_ref, s_ref, mx_ref, ls_ref, acc_ref,
                *, top_k, seq_len):
    tq, tk = DSA_TQ, DSA_TK
    i = pl.program_id(1)
    t0 = i * tq
    nk = i + 1
    idx_scale = (IDX_DIM ** -0.5) * (IDX_HEADS ** -0.5)

    @pl.when(i == 0)
    def _():
        _transpose_values(v_ref, vt_ref, seq_len // tk, tk)

    lane = lax.broadcasted_iota(jnp.int32, (tq, LANES), 1)
    key_loc = lax.broadcasted_iota(jnp.int32, (tk, tq), 0).astype(F32)
    qry_loc = lax.broadcasted_iota(jnp.int32, (tk, tq), 1).astype(F32)

    for c in range(BLOCKS_PER_SEG):
        qp = qi_ref[c].astype(F32)
        qs_ref[2 * c] = jnp.where(lane < IDX_DIM, qp, 0.0).astype(BF16)
        qs_ref[2 * c + 1] = jnp.where(lane >= IDX_DIM, qp, 0.0).astype(BF16)
    w_t = (wi_ref[...] * idx_scale).T
    w_rows = [w_t[h:h + 1, :] for h in range(IDX_HEADS)]

    def score_tiles(first, count):
        for u in range(count):
            kt = first + u
            kk = kk_ref[pl.ds(pl.multiple_of(kt * tk, tk), tk), :]
            acc = jnp.zeros((tk, tq), F32)
            for h in range(IDX_HEADS):
                acc = acc + w_rows[h] * jnp.maximum(_dot_nt(kk, qs_ref[h]), 0.0)
            off = (kt * tk - t0).astype(F32)
            causal = (key_loc + off) <= qry_loc
            sc_ref[kt] = jnp.where(causal, acc, -jnp.inf)
            ext_ref[0] = jnp.minimum(ext_ref[0], _fold_rows(jnp.where(causal, acc, BIG), jnp.minimum))
            ext_ref[1] = jnp.maximum(ext_ref[1], _fold_rows(jnp.where(causal, acc, NEG_INF), jnp.maximum))

    ext_ref[0] = jnp.full((8, tq), BIG, F32)
    ext_ref[1] = jnp.full((8, tq), NEG_INF, F32)
    _for_tile_chunks(nk, score_tiles)

    t_ref[...] = jnp.full((1, tq), NEG_INF, F32)
    j_ref[...] = jnp.full((1, tq), -1.0, F32)

    def count(pred):
        def body(kt, acc):
            off = (kt * tk).astype(F32)
            return acc + _fold_rows(jnp.where(pred(sc_ref[kt], key_loc + off), 1.0, 0.0), jnp.add)
        return jnp.sum(lax.fori_loop(0, nk, body, jnp.zeros((8, tq), F32)), axis=0, keepdims=True)

    @pl.when(t0 + tq > top_k)
    def _():
        kf = float(top_k)
        qpos = lax.broadcasted_iota(jnp.int32, (1, tq), 1) + t0
        done0 = jnp.where(qpos + 1 <= top_k, 1.0, 0.0)

        lo0 = jnp.min(ext_ref[0], axis=0, keepdims=True)
        hi0 = jnp.max(ext_ref[1], axis=0, keepdims=True)

        def step(x, degen, state):
            lo, hi, thr, done, tie = state
            c = count(lambda s, pos: s > x)
            active = done == 0.0
            live = jnp.logical_and(active, jnp.logical_not(degen))
            found = jnp.logical_and(live, c == kf)
            new_tie = jnp.logical_and(active, degen)
            thr = jnp.where(found, x, jnp.where(new_tie, hi, thr))
            tie = jnp.where(new_tie, 1.0, tie)
            done = jnp.where(jnp.logical_or(found, new_tie), 1.0, done)
            lo = jnp.where(jnp.logical_and(live, c > kf), x, lo)
            hi = jnp.where(jnp.logical_and(live, c < kf), x, hi)
            return lo, hi, thr, done, tie

        state0 = (lo0, hi0, jnp.full((1, tq), NEG_INF, F32), done0, jnp.zeros((1, tq), F32))

        def cond(carry):
            return carry[1] > 0.0

        def body(carry):
            state, _ = carry
            for _ in range(SEARCH_STEPS):
                lo, hi = state[0], state[1]
                mid = 0.5 * lo + 0.5 * hi
                degen = jnp.logical_or(mid <= lo, mid >= hi)
                state = step(mid, degen, state)
            return state, jnp.max(1.0 - state[3])

        (lo, hi, thr_loop, done, tie), _ = lax.while_loop(cond, body, (state0, jnp.max(1.0 - state0[3])))
        t_ref[...] = thr_loop

        @pl.when(jnp.max(tie) > 0.0)
        def _():
            reach_hi = count(lambda s, pos: s >= hi)
            thr = jnp.where(jnp.logical_and(tie > 0.0, reach_hi < kf), lo, thr_loop)
            t_ref[...] = thr
            need = kf - count(lambda s, pos: s > thr)
            n_bits = int(np.ceil(np.log2(seq_len + 1)))

            def jstep(_, carry):
                jlo, jhi = carry
                jmid = jnp.floor((jlo + jhi) * 0.5)
                c = count(lambda s, pos: jnp.logical_and(s == thr, pos <= jmid))
                ge = c >= need
                return jnp.where(ge, jlo, jmid), jnp.where(ge, jmid, jhi)

            _, jhi = lax.fori_loop(0, n_bits, jstep,
                                   (jnp.full((1, tq), -1.0, F32),
                                    jnp.full((1, tq), float(seq_len - 1), F32)))
            j_ref[...] = jnp.where(tie > 0.0, jhi, -1.0)

    def mask_tile(kt, carry):
        s = sc_ref[kt]
        thr = t_ref[...]
        off = (kt * tk).astype(F32)
        sel = jnp.logical_or(s > thr, jnp.logical_and(s == thr, (key_loc + off) <= j_ref[...]))
        sc_ref[kt] = jnp.where(sel, 0.0, NEG_INF)
        return carry

    lax.fori_loop(0, nk, mask_tile, 0)

    for h in range(N_HEADS):
        feat = _alibi_query_features(_alibi_slope(h, N_HEADS), i, tq)
        qcat_ref[h] = jnp.concatenate([q_ref[h], feat.astype(BF16)], axis=1)

    def score_rows(h, first, count):
        ks = pl.ds(pl.multiple_of(first * tk, tk), count * tk)
        kcat = jnp.concatenate([k_ref[h, ks, :], kaug_ref[ks, :]], axis=1)
        return _dot_nt(kcat, qcat_ref[h]) + sc_ref[pl.ds(first, count)].reshape(count * tk, tq)

    mx_ref[...] = jnp.full(mx_ref.shape, NEG_INF, F32)
    value_cols = lambda h, kt: vt_ref[h, kt]
    m = _softmax_pass_a(nk, tk, score_rows, s_ref, mx_ref)
    lsum = _softmax_pass_b(nk, tk, m, value_cols, s_ref, ls_ref, acc_ref)
    for h in range(N_HEADS):
        _gated_output(h, z_ref, lsum[h], acc_ref, o_ref)


def _dsa_attention(p, kk, wi, kaug, bsz, seq_len):
    tq, tk = DSA_TQ, DSA_TK
    nq = seq_len // tq
    nkt = seq_len // tk
    m = bsz * seq_len
    top_k = min(DSA_TOPK_MAX, seq_len // 4)
    seg = lambda s: (lambda b, i: (s, b * nq + i, 0))
    seg_full = lambda s: (lambda b, i: (s, b, 0))
    kern = functools.partial(_dsa_kernel, top_k=top_k, seq_len=seq_len)
    return pl.pallas_call(
        kern,
        grid=(bsz, nq),
        in_specs=[
            pl.BlockSpec((BLOCKS_PER_SEG, tq, LANES), seg(SEG_QA)),
            pl.BlockSpec((BLOCKS_PER_SEG, seq_len, LANES), seg_full(SEG_KA)),
            pl.BlockSpec((BLOCKS_PER_SEG, seq_len, LANES), seg_full(SEG_VA)),
            pl.BlockSpec((BLOCKS_PER_SEG, tq, LANES), seg(SEG_ZA)),
            pl.BlockSpec((BLOCKS_PER_SEG, tq, LANES), seg(SEG_QIDX)),
            pl.BlockSpec((seq_len, LANES), lambda b, i: (b, 0)),
            pl.BlockSpec((tq, LANES), lambda b, i: (b * nq + i, 0)),
            pl.BlockSpec((seq_len, LANES), lambda b, i: (0, 0)),
        ],
        out_specs=pl.BlockSpec((tq, D_GROUP), lambda b, i: (b * nq + i, 0)),
        out_shape=jax.ShapeDtypeStruct((m, D_GROUP), BF16),
        scratch_shapes=[
            pltpu.VMEM((N_HEADS, nkt, HEAD_DIM, tk), BF16),
            pltpu.VMEM((IDX_HEADS, tq, LANES), BF16),
            pltpu.VMEM((nkt, tk, tq), F32),
            pltpu.VMEM((2, 8, tq), F32),
            pltpu.VMEM((1, tq), F32),
            pltpu.VMEM((1, tq), F32),
            pltpu.VMEM((N_HEADS, tq, 2 * LANES), BF16),
            pltpu.VMEM((N_HEADS, nkt, tk, tq), F32),
            pltpu.VMEM((N_HEADS, 8, tq), F32),
            pltpu.VMEM((N_HEADS, 8, tq), F32),
            pltpu.VMEM((N_HEADS, HEAD_DIM, tq), F32),
        ],
        compiler_params=pltpu.CompilerParams(
            dimension_semantics=("arbitrary", "arbitrary"), vmem_limit_bytes=VMEM_LIMIT_BYTES),
        name="dsa_attn",
    )(p, p, p, p, p, kk, wi, kaug)


def _moba_kernel(q_ref, k_ref, v_ref, z_ref, kaug_ref, o_ref,
                 kmean_ref, vt_ref, selb_ref, qcat_ref, s_ref, mx_ref, ls_ref, acc_ref, *, n_blocks):
    bs = MOBA_BLOCK
    n = pl.program_id(1)
    top_k = min(MOBA_TOPK_MAX, n_blocks - 1)

    @pl.when(n == 0)
    def _():
        kmean_ref[...] = jnp.zeros(kmean_ref.shape, F32)
        for h in range(N_HEADS):
            for j in range(n_blocks):
                kb = k_ref[h, j * bs:(j + 1) * bs, :].astype(F32)
                kmean_ref[h, j:j + 1, :] = jnp.mean(kb, axis=0, keepdims=True)
        _transpose_values(v_ref, vt_ref, n_blocks, bs)

    nb_rows = -(-n_blocks // 8) * 8
    blk = lax.broadcasted_iota(jnp.int32, (nb_rows, bs), 0)
    past = blk < n

    selb_ref[...] = jnp.zeros(selb_ref.shape, F32)
    for h in range(N_HEADS):
        q = q_ref[h]
        g = _dot_nt(kmean_ref[h].astype(BF16), q)[:nb_rows]
        for j in range(n_blocks):
            gj = g[j:j + 1, :]
            beats = jnp.logical_or(g > gj, jnp.logical_and(g == gj, blk < j))
            rank = jnp.sum(jnp.where(jnp.logical_and(beats, past), 1.0, 0.0), axis=0, keepdims=True)
            dropped = jnp.logical_and(rank >= float(top_k), j < n)
            selb_ref[AUG_ONEHOT0 + j:AUG_ONEHOT0 + j + 1, :] = jnp.where(dropped, NEG_INF, 0.0)
        feat = _alibi_query_features(_alibi_slope(h, N_HEADS), n, bs) + selb_ref[...].T
        qcat_ref[h] = jnp.concatenate([q, feat.astype(BF16)], axis=1)

    key_loc = lax.broadcasted_iota(jnp.int32, (bs, bs), 0)
    qry_loc = lax.broadcasted_iota(jnp.int32, (bs, bs), 1)
    causal = key_loc <= qry_loc

    def score_rows(h, first, count):
        ks = pl.ds(pl.multiple_of(first * bs, bs), count * bs)
        kcat = jnp.concatenate([k_ref[h, ks, :], kaug_ref[ks, :]], axis=1)
        return _dot_nt(kcat, qcat_ref[h])

    for h in range(N_HEADS):
        s = jnp.where(causal, score_rows(h, n, 1), NEG_INF) * LOG2E
        s_ref[h, n] = s
        mx_ref[h] = _fold_rows(s, jnp.maximum)
    value_cols = lambda h, j: vt_ref[h, j]
    m = _softmax_pass_a(n, bs, score_rows, s_ref, mx_ref)
    lsum = _softmax_pass_b(n + 1, bs, m, value_cols, s_ref, ls_ref, acc_ref)
    for h in range(N_HEADS):
        _gated_output(h, z_ref, lsum[h], acc_ref, o_ref)


def _moba_attention(p, kaug, bsz, seq_len):
    bs = MOBA_BLOCK
    nb = seq_len // bs
    m = bsz * seq_len
    seg = lambda s: (lambda b, i: (s, b * nb + i, 0))
    seg_full = lambda s: (lambda b, i: (s, b, 0))
    kern = functools.partial(_moba_kernel, n_blocks=nb)
    nb_pad = max(16, nb)
    assert AUG_ONEHOT0 + nb <= LANES
    return pl.pallas_call(
        kern,
        grid=(bsz, nb),
        in_specs=[
            pl.BlockSpec((BLOCKS_PER_SEG, bs, LANES), seg(SEG_QB)),
            pl.BlockSpec((BLOCKS_PER_SEG, seq_len, LANES), seg_full(SEG_KB)),
            pl.BlockSpec((BLOCKS_PER_SEG, seq_len, LANES), seg_full(SEG_VB)),
            pl.BlockSpec((BLOCKS_PER_SEG, bs, LANES), seg(SEG_ZB)),
            pl.BlockSpec((seq_len, LANES), lambda b, i: (0, 0)),
        ],
        out_specs=pl.BlockSpec((bs, D_GROUP), lambda b, i: (b * nb + i, 0)),
        out_shape=jax.ShapeDtypeStruct((m, D_GROUP), BF16),
        scratch_shapes=[
            pltpu.VMEM((N_HEADS, nb_pad, HEAD_DIM), F32),
            pltpu.VMEM((N_HEADS, nb, HEAD_DIM, bs), BF16),
            pltpu.VMEM((LANES, bs), F32),
            pltpu.VMEM((N_HEADS, bs, 2 * LANES), BF16),
            pltpu.VMEM((N_HEADS, nb, bs, bs), F32),
            pltpu.VMEM((N_HEADS, 8, bs), F32),
            pltpu.VMEM((N_HEADS, 8, bs), F32),
            pltpu.VMEM((N_HEADS, HEAD_DIM, bs), F32),
        ],
        compiler_params=pltpu.CompilerParams(
            dimension_semantics=("arbitrary", "arbitrary"), vmem_limit_bytes=VMEM_LIMIT_BYTES),
        name="moba_attn",
    )(p, p, p, p, kaug)


OUT_CHUNK = 512


def _out_kernel(ya_ref, yb_ref, wa_ref, wb_ref, x_ref, mod_ref, o_ref):
    ya = ya_ref[...]
    yb = yb_ref[...]
    for c in range(o_ref.shape[1] // OUT_CHUNK):
        cs = slice(c * OUT_CHUNK, (c + 1) * OUT_CHUNK)
        y = jnp.dot(ya, wa_ref[:, cs], preferred_element_type=F32)
        y = y + jnp.dot(yb, wb_ref[:, cs], preferred_element_type=F32)
        o_ref[:, cs] = x_ref[:, cs] + mod_ref[0, 2:3, cs] * y


def _out_projection(ya, yb, w_out_bf, x2, mod3, seq_len):
    m, d = x2.shape
    tm = 512
    tiles_per_seq = seq_len // tm
    return pl.pallas_call(
        _out_kernel,
        grid=(m // tm,),
        in_specs=[
            pl.BlockSpec((tm, D_GROUP), lambda i: (i, 0)),
            pl.BlockSpec((tm, D_GROUP), lambda i: (i, 0)),
            pl.BlockSpec((D_GROUP, d), lambda i: (0, 0)),
            pl.BlockSpec((D_GROUP, d), lambda i: (1, 0)),
            pl.BlockSpec((tm, d), lambda i: (i, 0)),
            pl.BlockSpec((1, 3, d), lambda i: (i // tiles_per_seq, 0, 0)),
        ],
        out_specs=pl.BlockSpec((tm, d), lambda i: (i, 0)),
        out_shape=jax.ShapeDtypeStruct((m, d), F32),
        compiler_params=pltpu.CompilerParams(
            dimension_semantics=("arbitrary",), vmem_limit_bytes=VMEM_LIMIT_BYTES),
        name="out_proj",
    )(ya, yb, w_out_bf, w_out_bf, x2, mod3)


RELAYOUT_ROWS = 512


def _relayout_kernel(a_ref, b_ref, mid_ref, main_ref, tail_ref, *, n_a_tiles, n_mid):
    r = pl.program_id(0)

    @pl.when(r < n_a_tiles)
    def _():
        main_ref[...] = a_ref[...].astype(BF16)

    @pl.when(r >= n_a_tiles)
    def _():
        main_ref[:RELAYOUT_ROWS - n_mid, :] = a_ref[n_mid:, :].astype(BF16)
        main_ref[RELAYOUT_ROWS - n_mid:, :] = b_ref[:n_mid, :].astype(BF16)

    kidx = mid_ref[:IDX_DIM, :].astype(BF16)
    tail_ref[0:IDX_DIM, :] = kidx
    tail_ref[IDX_DIM:2 * IDX_DIM, :] = kidx
    tail_ref[2 * IDX_DIM:2 * IDX_DIM + IDX_HEADS, :] = mid_ref[IDX_DIM:IDX_DIM + IDX_HEADS, :].astype(BF16)
    tail_ref[2 * IDX_DIM + IDX_HEADS:, :] = jnp.zeros(
        (tail_ref.shape[0] - 2 * IDX_DIM - IDX_HEADS, tail_ref.shape[1]), BF16)


def _relayout_weights(w_t):
    n_in, d = w_t.shape
    n_a = 4 * D_GROUP + IDX_HEADS * IDX_DIM
    n_mid = IDX_DIM + IDX_HEADS
    tr = RELAYOUT_ROWS
    assert n_in == n_a + n_mid + 4 * D_GROUP and n_a % tr == 0 and (4 * D_GROUP) % tr == 0
    assert n_mid % 16 == 0 and n_mid < tr and n_a % n_mid == 0
    n_main = n_in - n_mid
    n_a_tiles = n_a // tr
    return pl.pallas_call(
        functools.partial(_relayout_kernel, n_a_tiles=n_a_tiles, n_mid=n_mid),
        grid=(n_main // tr,),
        in_specs=[
            pl.BlockSpec((tr, d), lambda r: (r, 0)),
            pl.BlockSpec((tr, d), lambda r: (jnp.maximum(r, n_a_tiles - 1) + 1, 0)),
            pl.BlockSpec((n_mid, d), lambda r: (n_a // n_mid, 0)),
        ],
        out_specs=[
            pl.BlockSpec((tr, d), lambda r: (r, 0)),
            pl.BlockSpec((2 * LANES, d), lambda r: (0, 0)),
        ],
        out_shape=[
            jax.ShapeDtypeStruct((n_main, d), BF16),
            jax.ShapeDtypeStruct((2 * LANES, d), BF16),
        ],
        compiler_params=pltpu.CompilerParams(
            dimension_semantics=("arbitrary",), vmem_limit_bytes=VMEM_LIMIT_BYTES),
        name="w_relayout",
    )(w_t, w_t, w_t)


def _layer(x, c, w_ada, b_ada, g_norm, w_in, q_norm_a, k_norm_a, k_norm_idx, q_norm_b, k_norm_b, w_out):
    bsz, seq_len, d = x.shape
    assert seq_len % MOBA_BLOCK == 0 and seq_len % 1024 == 0 and d % 1024 == 0
    x2 = x.reshape(bsz * seq_len, d)

    mod3 = _modulation(c, w_ada, b_ada).reshape(bsz, 3, d)

    w_main, w_tail = _relayout_weights(jnp.swapaxes(w_in, 0, 1))
    ones = jnp.ones((HEAD_DIM,), F32)
    sm_scale = HEAD_DIM ** -0.5
    gains = jnp.stack([q_norm_a * sm_scale, k_norm_a, ones, ones, ones,
                       q_norm_b * sm_scale, k_norm_b, ones, ones]).reshape(N_SEGS, 1, HEAD_DIM)
    for h in range(N_HEADS):
        slope = _alibi_slope(h, N_HEADS)
        assert float(np.float32(slope).astype(jnp.bfloat16)) == slope, "ALiBi slopes must be exact in bf16"
    assert seq_len // POS_TILE <= 256 and POS_TILE <= 256, "positions must split into bf16-exact parts"
    kaug = _key_features(seq_len)
    gk = jnp.concatenate([k_norm_idx, k_norm_idx]).reshape(1, LANES)

    p, kk, wi = _projection(x2, mod3, g_norm.reshape(1, d), w_main, w_tail, gains, gk, seq_len)
    ya = _dsa_attention(p, kk, wi, kaug, bsz, seq_len)
    yb = _moba_attention(p, kaug, bsz, seq_len)
    out = _out_projection(ya, yb, w_out.astype(BF16), x2, mod3, seq_len)
    return out.reshape(bsz, seq_len, d)


def kernel(x, c, w_ada, b_ada, g_norm, w_in, q_norm_a, k_norm_a, k_norm_idx, q_norm_b, k_norm_b, w_out):
    for i in range(w_ada.shape[0]):
        x = _layer(x, c, w_ada[i], b_ada[i], g_norm[i], w_in[i], q_norm_a[i], k_norm_a[i],
                   k_norm_idx[i], q_norm_b[i], k_norm_b[i], w_out[i])
    return x
```

```python
import functools

import jax
import jax.numpy as jnp
import numpy as np
from jax import lax
from jax.experimental import pallas as pl
from jax.experimental.pallas import tpu as pltpu

F32 = jnp.float32
BF16 = jnp.bfloat16

HEAD_DIM = 128
N_HEADS = 8
D_GROUP = N_HEADS * HEAD_DIM
IDX_HEADS = 16
IDX_DIM = 64
DSA_TOPK_MAX = 256
MOBA_BLOCK = 256
MOBA_TOPK_MAX = 3
RMS_EPS = 1e-6
NEG_INF = -1e30
BIG = 1e30
LOG2E = 1.4426950408889634

LANES = 128
VMEM_LIMIT_BYTES = 56 * 1024 * 1024

SEG_QA, SEG_KA, SEG_VA, SEG_ZA, SEG_QIDX, SEG_QB, SEG_KB, SEG_VB, SEG_ZB = range(9)
N_SEGS = 9
BLOCKS_PER_SEG = D_GROUP // LANES


def _alibi_slope(h, n):
    return float(2.0 ** (-8.0 * (h + 1) / n))


def _dot_nt(a, b):
    return lax.dot_general(a, b, (((1,), (1,)), ((), ())), preferred_element_type=F32)


def _mod_kernel(c_ref, w_ref, b_ref, o_ref):
    c = c_ref[...]
    s = c * jax.nn.sigmoid(c)
    o_ref[...] = jnp.dot(s, w_ref[...], preferred_element_type=F32) + b_ref[...]


def _modulation(c, w_ada, b_ada):
    bsz, d = c.shape
    n = w_ada.shape[1]
    tn = 1024
    return pl.pallas_call(
        _mod_kernel,
        grid=(n // tn,),
        in_specs=[
            pl.BlockSpec((bsz, d), lambda j: (0, 0)),
            pl.BlockSpec((d, tn), lambda j: (0, j)),
            pl.BlockSpec((1, tn), lambda j: (0, j)),
        ],
        out_specs=pl.BlockSpec((bsz, tn), lambda j: (0, j)),
        out_shape=jax.ShapeDtypeStruct((bsz, n), F32),
        compiler_params=pltpu.CompilerParams(
            dimension_semantics=("arbitrary",), vmem_limit_bytes=VMEM_LIMIT_BYTES),
        name="adaln_mod",
    )(c, w_ada, b_ada.reshape(1, n))


PROJ_CHUNK = 2 * LANES


def _proj_kernel(x_ref, mod_ref, g_ref, w_ref, wt_ref, gain_ref, gk_ref,
                 p_ref, kk_ref, wi_ref, h_ref):
    j = pl.program_id(1)

    @pl.when(j == 0)
    def _():
        x = x_ref[...]
        ms = jnp.mean(x * x, axis=-1, keepdims=True)
        shift = mod_ref[0, 0:1, :]
        gs = g_ref[...] * (1.0 + mod_ref[0, 1:2, :])
        hb = (x * lax.rsqrt(ms + RMS_EPS) * gs + shift).astype(BF16)
        h_ref[...] = hb
        t = _dot_nt(hb, wt_ref[...])
        tk = t[:, :LANES]
        kms = jnp.mean(tk * tk, axis=-1, keepdims=True)
        kk_ref[...] = (tk * lax.rsqrt(kms + RMS_EPS) * gk_ref[...]).astype(BF16)
        wi_ref[...] = t[:, LANES:]

    def segment(with_norm):
        g = gain_ref[0]
        h = h_ref[...]
        for cc in range(D_GROUP // PROJ_CHUNK):
            acc = _dot_nt(h, w_ref[cc * PROJ_CHUNK:(cc + 1) * PROJ_CHUNK, :])
            for c in range(PROJ_CHUNK // LANES):
                a = acc[:, c * LANES:(c + 1) * LANES]
                if with_norm:
                    ms = jnp.mean(a * a, axis=-1, keepdims=True)
                    a = a * lax.rsqrt(ms + RMS_EPS) * g
                p_ref[cc * (PROJ_CHUNK // LANES) + c] = a.astype(BF16)

    is_norm = (j == SEG_QA) | (j == SEG_KA) | (j == SEG_QB) | (j == SEG_KB)
    pl.when(is_norm)(functools.partial(segment, True))
    pl.when(jnp.logical_not(is_norm))(functools.partial(segment, False))


def _projection(x2, mod3, g_norm, w_main, w_tail, gains, gk, seq_len):
    m, d = x2.shape
    tm = 1024
    tiles_per_seq = seq_len // tm
    return pl.pallas_call(
        _proj_kernel,
        grid=(m // tm, N_SEGS),
        in_specs=[
            pl.BlockSpec((tm, d), lambda i, j: (i, 0)),
            pl.BlockSpec((1, 3, d), lambda i, j: (i // tiles_per_seq, 0, 0)),
            pl.BlockSpec((1, d), lambda i, j: (0, 0)),
            pl.BlockSpec((D_GROUP, d), lambda i, j: (j, 0)),
            pl.BlockSpec((2 * LANES, d), lambda i, j: (0, 0)),
            pl.BlockSpec((1, 1, LANES), lambda i, j: (j, 0, 0)),
            pl.BlockSpec((1, LANES), lambda i, j: (0, 0)),
        ],
        out_specs=[
            pl.BlockSpec((BLOCKS_PER_SEG, tm, LANES), lambda i, j: (j, i, 0)),
            pl.BlockSpec((tm, LANES), lambda i, j: (i, 0)),
            pl.BlockSpec((tm, LANES), lambda i, j: (i, 0)),
        ],
        out_shape=[
            jax.ShapeDtypeStruct((N_SEGS * BLOCKS_PER_SEG, m, LANES), BF16),
            jax.ShapeDtypeStruct((m, LANES), BF16),
            jax.ShapeDtypeStruct((m, LANES), F32),
        ],
        scratch_shapes=[pltpu.VMEM((tm, d), BF16)],
        compiler_params=pltpu.CompilerParams(
            dimension_semantics=("arbitrary", "arbitrary"), vmem_limit_bytes=VMEM_LIMIT_BYTES),
        name="in_proj",
    )(x2, mod3, g_norm, w_main, w_tail, gains, gk)


POS_TILE = MOBA_BLOCK
AUG_KLOC, AUG_KBLK, AUG_ONE_LOC, AUG_ONE_BLK, AUG_ONEHOT0 = 0, 1, 2, 3, 8


def _key_features(seq_len):
    pos = np.arange(seq_len)
    f = np.zeros((seq_len, LANES), np.float32)
    f[:, AUG_KLOC] = pos % POS_TILE
    f[:, AUG_KBLK] = pos // POS_TILE
    f[:, AUG_ONE_LOC] = 1.0
    f[:, AUG_ONE_BLK] = 1.0
    f[pos, AUG_ONEHOT0 + pos // POS_TILE] = 1.0
    return jnp.asarray(f, BF16)


def _alibi_query_features(slope, q_tile, tq):
    lane = lax.broadcasted_iota(jnp.int32, (tq, LANES), 1)
    row = lax.broadcasted_iota(jnp.int32, (tq, LANES), 0).astype(F32)
    f = jnp.where(lane == AUG_KLOC, slope, 0.0)
    f = jnp.where(lane == AUG_KBLK, slope * POS_TILE, f)
    f = jnp.where(lane == AUG_ONE_LOC, -slope * row, f)
    return jnp.where(lane == AUG_ONE_BLK, (-slope * POS_TILE) * q_tile.astype(F32), f)


FOLD_CHAINS = 4


def _fold_rows(x, op):
    r = x.reshape(x.shape[0] // 8, 8, x.shape[1])
    n_acc = min(FOLD_CHAINS, r.shape[0])
    accs = [r[k] for k in range(n_acc)]
    for g in range(n_acc, r.shape[0]):
        accs[g % n_acc] = op(accs[g % n_acc], r[g])
    while len(accs) > 1:
        accs = [op(accs[k], accs[k + 1]) for k in range(0, len(accs) - 1, 2)] + accs[len(accs) - len(accs) % 2:]
    return accs[0]


TILE_CHUNKS = (4, 2, 1)


def _for_tile_chunks(n_tiles, fn, chunks=TILE_CHUNKS):
    big = chunks[0]
    shift = big.bit_length() - 1

    def many(c, carry):
        fn(c * big, big)
        return carry

    n_big = lax.shift_right_logical(n_tiles, shift)
    lax.fori_loop(0, n_big, many, 0)
    done = n_big * big
    for size in chunks[1:]:
        has = (n_tiles & size) != 0

        @pl.when(has)
        def _(done=done, size=size):
            fn(done, size)

        done = done + (n_tiles & size)


def _softmax_pass_a(n_tiles, tile, score_rows, s_ref, mx_ref):
    tq = s_ref.shape[-1]

    def pass_a(first, count):
        for h in range(N_HEADS):
            s = score_rows(h, first, count) * LOG2E
            s_ref[h, pl.ds(first, count)] = s.reshape(count, tile, tq)
            mx_ref[h] = jnp.maximum(mx_ref[h], _fold_rows(s, jnp.maximum))

    _for_tile_chunks(n_tiles, pass_a)
    return [jnp.max(mx_ref[h], axis=0, keepdims=True) for h in range(N_HEADS)]


def _softmax_pass_b(n_tiles, tile, m, value_cols, s_ref, ls_ref, acc_ref):
    tq = acc_ref.shape[-1]
    ls_ref[...] = jnp.zeros(ls_ref.shape, F32)
    acc_ref[...] = jnp.zeros(acc_ref.shape, F32)

    def pass_b(first, count):
        for h in range(N_HEADS):
            p = jnp.exp2(s_ref[h, pl.ds(first, count)].reshape(count * tile, tq) - m[h])
            ls_ref[h] += _fold_rows(p, jnp.add)
            vt = jnp.concatenate([value_cols(h, first + u) for u in range(count)], axis=1)
            acc_ref[h] += jnp.dot(vt, p.astype(BF16), preferred_element_type=F32)

    _for_tile_chunks(n_tiles, pass_b)
    return [jnp.sum(ls_ref[h], axis=0, keepdims=True) for h in range(N_HEADS)]


def _gated_output(h, z_ref, l, acc_ref, o_ref):
    z = z_ref[h].astype(F32)
    o = (acc_ref[h] / l).T * (z * jax.nn.sigmoid(z))
    o_ref[:, h * HEAD_DIM:(h + 1) * HEAD_DIM] = o.astype(BF16)


def _transpose_values(v_ref, vt_ref, n_tiles, tile):
    for h in range(N_HEADS):
        for j in range(n_tiles):
            vt_ref[h, j] = v_ref[h, j * tile:(j + 1) * tile, :].astype(F32).T.astype(BF16)


DSA_TQ = 256
DSA_TK = POS_TILE
SEARCH_STEPS = 4
UNCHECKED_ROUNDS = 3


def _dsa_kernel(q_ref, k_ref, v_ref, z_ref, qi_ref, kk_ref, wi_ref, kaug_ref, o_ref,
                vt_ref, qs_ref, sc_ref, ext_ref, t_ref, j_ref, qcat_ref, s_ref, mx_ref, ls_ref, acc_ref,
                *, top_k, seq_len):
    tq, tk = DSA_TQ, DSA_TK
    i = pl.program_id(1)
    t0 = i * tq
    nk = i + 1
    idx_scale = (IDX_DIM ** -0.5) * (IDX_HEADS ** -0.5)

    @pl.when(i == 0)
    def _():
        _transpose_values(v_ref, vt_ref, seq_len // tk, tk)

    lane = lax.broadcasted_iota(jnp.int32, (tq, LANES), 1)
    key_loc = lax.broadcasted_iota(jnp.int32, (tk, tq), 0).astype(F32)
    qry_loc = lax.broadcasted_iota(jnp.int32, (tk, tq), 1).astype(F32)

    for c in range(BLOCKS_PER_SEG):
        qp = qi_ref[c].astype(F32)
        qs_ref[2 * c] = jnp.where(lane < IDX_DIM, qp, 0.0).astype(BF16)
        qs_ref[2 * c + 1] = jnp.where(lane >= IDX_DIM, qp, 0.0).astype(BF16)
    w_t = (wi_ref[...] * idx_scale).T
    w_rows = [w_t[h:h + 1, :] for h in range(IDX_HEADS)]

    def score_tiles(first, count):
        for u in range(count):
            kt = first + u
            kk = kk_ref[pl.ds(pl.multiple_of(kt * tk, tk), tk), :]
            acc = jnp.zeros((tk, tq), F32)
            for h in range(IDX_HEADS):
                acc = acc + w_rows[h] * jnp.maximum(_dot_nt(kk, qs_ref[h]), 0.0)
            off = (kt * tk - t0).astype(F32)
            causal = (key_loc + off) <= qry_loc
            sc_ref[kt] = jnp.where(causal, acc, -jnp.inf)
            ext_ref[0] = jnp.minimum(ext_ref[0], _fold_rows(jnp.where(causal, acc, BIG), jnp.minimum))
            ext_ref[1] = jnp.maximum(ext_ref[1], _fold_rows(jnp.where(causal, acc, NEG_INF), jnp.maximum))

    ext_ref[0] = jnp.full((8, tq), BIG, F32)
    ext_ref[1] = jnp.full((8, tq), NEG_INF, F32)
    _for_tile_chunks(nk, score_tiles)

    t_ref[...] = jnp.full((1, tq), NEG_INF, F32)
    j_ref[...] = jnp.full((1, tq), -1.0, F32)

    def count(pred):
        def body(kt, acc):
            off = (kt * tk).astype(F32)
            return acc + _fold_rows(jnp.where(pred(sc_ref[kt], key_loc + off), 1.0, 0.0), jnp.add)
        return jnp.sum(lax.fori_loop(0, nk, body, jnp.zeros((8, tq), F32)), axis=0, keepdims=True)

    @pl.when(t0 + tq > top_k)
    def _():
        kf = float(top_k)
        qpos = lax.broadcasted_iota(jnp.int32, (1, tq), 1) + t0
        done0 = jnp.where(qpos + 1 <= top_k, 1.0, 0.0)

        lo0 = jnp.min(ext_ref[0], axis=0, keepdims=True)
        hi0 = jnp.max(ext_ref[1], axis=0, keepdims=True)

        def step(x, degen, state):
            lo, hi, thr, done, tie = state
            c = count(lambda s, pos: s > x)
            active = done == 0.0
            live = jnp.logical_and(active, jnp.logical_not(degen))
            found = jnp.logical_and(live, c == kf)
            new_tie = jnp.logical_and(active, degen)
            thr = jnp.where(found, x, jnp.where(new_tie, hi, thr))
            tie = jnp.where(new_tie, 1.0, tie)
            done = jnp.where(jnp.logical_or(found, new_tie), 1.0, done)
            lo = jnp.where(jnp.logical_and(live, c > kf), x, lo)
            hi = jnp.where(jnp.logical_and(live, c < kf), x, hi)
            return lo, hi, thr, done, tie

        state0 = (lo0, hi0, jnp.full((1, tq), NEG_INF, F32), done0, jnp.zeros((1, tq), F32))

        def bisect(state):
            for _ in range(SEARCH_STEPS):
                lo, hi = state[0], state[1]
                mid = 0.5 * lo + 0.5 * hi
                degen = jnp.logical_or(mid <= lo, mid >= hi)
                state = step(mid, degen, state)
            return state

        state1 = lax.fori_loop(0, UNCHECKED_ROUNDS, lambda _, st: bisect(st), state0)

        def cond(carry):
            return carry[1] > 0.0

        def body(carry):
            state = bisect(carry[0])
            return state, jnp.max(1.0 - state[3])

        (lo, hi, thr_loop, done, tie), _ = lax.while_loop(cond, body, (state1, jnp.max(1.0 - state1[3])))
        t_ref[...] = thr_loop

        @pl.when(jnp.max(tie) > 0.0)
        def _():
            reach_hi = count(lambda s, pos: s >= hi)
            thr = jnp.where(jnp.logical_and(tie > 0.0, reach_hi < kf), lo, thr_loop)
            t_ref[...] = thr
            need = kf - count(lambda s, pos: s > thr)
            n_bits = int(np.ceil(np.log2(seq_len + 1)))

            def jstep(_, carry):
                jlo, jhi = carry
                jmid = jnp.floor((jlo + jhi) * 0.5)
                c = count(lambda s, pos: jnp.logical_and(s == thr, pos <= jmid))
                ge = c >= need
                return jnp.where(ge, jlo, jmid), jnp.where(ge, jmid, jhi)

            _, jhi = lax.fori_loop(0, n_bits, jstep,
                                   (jnp.full((1, tq), -1.0, F32),
                                    jnp.full((1, tq), float(seq_len - 1), F32)))
            j_ref[...] = jnp.where(tie > 0.0, jhi, -1.0)

    def mask_tile(kt, carry):
        s = sc_ref[kt]
        thr = t_ref[...]
        off = (kt * tk).astype(F32)
        sel = jnp.logical_or(s > thr, jnp.logical_and(s == thr, (key_loc + off) <= j_ref[...]))
        sc_ref[kt] = jnp.where(sel, 0.0, NEG_INF)
        return carry

    lax.fori_loop(0, nk, mask_tile, 0)

    for h in range(N_HEADS):
        feat = _alibi_query_features(_alibi_slope(h, N_HEADS), i, tq)
        qcat_ref[h] = jnp.concatenate([q_ref[h], feat.astype(BF16)], axis=1)

    def score_rows(h, first, count):
        ks = pl.ds(pl.multiple_of(first * tk, tk), count * tk)
        kcat = jnp.concatenate([k_ref[h, ks, :], kaug_ref[ks, :]], axis=1)
        return _dot_nt(kcat, qcat_ref[h]) + sc_ref[pl.ds(first, count)].reshape(count * tk, tq)

    mx_ref[...] = jnp.full(mx_ref.shape, NEG_INF, F32)
    value_cols = lambda h, kt: vt_ref[h, kt]
    m = _softmax_pass_a(nk, tk, score_rows, s_ref, mx_ref)
    lsum = _softmax_pass_b(nk, tk, m, value_cols, s_ref, ls_ref, acc_ref)
    for h in range(N_HEADS):
        _gated_output(h, z_ref, lsum[h], acc_ref, o_ref)


def _dsa_attention(p, kk, wi, kaug, bsz, seq_len):
    tq, tk = DSA_TQ, DSA_TK
    nq = seq_len // tq
    nkt = seq_len // tk
    m = bsz * seq_len
    top_k = min(DSA_TOPK_MAX, seq_len // 4)
    seg = lambda s: (lambda b, i: (s, b * nq + i, 0))
    seg_full = lambda s: (lambda b, i: (s, b, 0))
    kern = functools.partial(_dsa_kernel, top_k=top_k, seq_len=seq_len)
    return pl.pallas_call(
        kern,
        grid=(bsz, nq),
        in_specs=[
            pl.BlockSpec((BLOCKS_PER_SEG, tq, LANES), seg(SEG_QA)),
            pl.BlockSpec((BLOCKS_PER_SEG, seq_len, LANES), seg_full(SEG_KA)),
            pl.BlockSpec((BLOCKS_PER_SEG, seq_len, LANES), seg_full(SEG_VA)),
            pl.BlockSpec((BLOCKS_PER_SEG, tq, LANES), seg(SEG_ZA)),
            pl.BlockSpec((BLOCKS_PER_SEG, tq, LANES), seg(SEG_QIDX)),
            pl.BlockSpec((seq_len, LANES), lambda b, i: (b, 0)),
            pl.BlockSpec((tq, LANES), lambda b, i: (b * nq + i, 0)),
            pl.BlockSpec((seq_len, LANES), lambda b, i: (0, 0)),
        ],
        out_specs=pl.BlockSpec((tq, D_GROUP), lambda b, i: (b * nq + i, 0)),
        out_shape=jax.ShapeDtypeStruct((m, D_GROUP), BF16),
        scratch_shapes=[
            pltpu.VMEM((N_HEADS, nkt, HEAD_DIM, tk), BF16),
            pltpu.VMEM((IDX_HEADS, tq, LANES), BF16),
            pltpu.VMEM((nkt, tk, tq), F32),
            pltpu.VMEM((2, 8, tq), F32),
            pltpu.VMEM((1, tq), F32),
            pltpu.VMEM((1, tq), F32),
            pltpu.VMEM((N_HEADS, tq, 2 * LANES), BF16),
            pltpu.VMEM((N_HEADS, nkt, tk, tq), F32),
            pltpu.VMEM((N_HEADS, 8, tq), F32),
            pltpu.VMEM((N_HEADS, 8, tq), F32),
            pltpu.VMEM((N_HEADS, HEAD_DIM, tq), F32),
        ],
        compiler_params=pltpu.CompilerParams(
            dimension_semantics=("arbitrary", "arbitrary"), vmem_limit_bytes=VMEM_LIMIT_BYTES),
        name="dsa_attn",
    )(p, p, p, p, p, kk, wi, kaug)


def _moba_kernel(q_ref, k_ref, v_ref, z_ref, kaug_ref, o_ref,
                 kmean_ref, vt_ref, selb_ref, qcat_ref, s_ref, mx_ref, ls_ref, acc_ref, *, n_blocks):
    bs = MOBA_BLOCK
    n = pl.program_id(1)
    top_k = min(MOBA_TOPK_MAX, n_blocks - 1)

    @pl.when(n == 0)
    def _():
        kmean_ref[...] = jnp.zeros(kmean_ref.shape, F32)
        for h in range(N_HEADS):
            for j in range(n_blocks):
                kb = k_ref[h, j * bs:(j + 1) * bs, :].astype(F32)
                kmean_ref[h, j:j + 1, :] = jnp.mean(kb, axis=0, keepdims=True)
        _transpose_values(v_ref, vt_ref, n_blocks, bs)

    nb_rows = -(-n_blocks // 8) * 8
    blk = lax.broadcasted_iota(jnp.int32, (nb_rows, bs), 0)
    past = blk < n

    selb_ref[...] = jnp.zeros(selb_ref.shape, F32)
    for h in range(N_HEADS):
        q = q_ref[h]
        g = _dot_nt(kmean_ref[h].astype(BF16), q)[:nb_rows]
        for j in range(n_blocks):
            gj = g[j:j + 1, :]
            beats = jnp.logical_or(g > gj, jnp.logical_and(g == gj, blk < j))
            rank = jnp.sum(jnp.where(jnp.logical_and(beats, past), 1.0, 0.0), axis=0, keepdims=True)
            dropped = jnp.logical_and(rank >= float(top_k), j < n)
            selb_ref[AUG_ONEHOT0 + j:AUG_ONEHOT0 + j + 1, :] = jnp.where(dropped, NEG_INF, 0.0)
        feat = _alibi_query_features(_alibi_slope(h, N_HEADS), n, bs) + selb_ref[...].T
        qcat_ref[h] = jnp.concatenate([q, feat.astype(BF16)], axis=1)

    key_loc = lax.broadcasted_iota(jnp.int32, (bs, bs), 0)
    qry_loc = lax.broadcasted_iota(jnp.int32, (bs, bs), 1)
    causal = key_loc <= qry_loc

    def score_rows(h, first, count):
        ks = pl.ds(pl.multiple_of(first * bs, bs), count * bs)
        kcat = jnp.concatenate([k_ref[h, ks, :], kaug_ref[ks, :]], axis=1)
        return _dot_nt(kcat, qcat_ref[h])

    for h in range(N_HEADS):
        s = jnp.where(causal, score_rows(h, n, 1), NEG_INF) * LOG2E
        s_ref[h, n] = s
        mx_ref[h] = _fold_rows(s, jnp.maximum)
    value_cols = lambda h, j: vt_ref[h, j]
    m = _softmax_pass_a(n, bs, score_rows, s_ref, mx_ref)
    lsum = _softmax_pass_b(n + 1, bs, m, value_cols, s_ref, ls_ref, acc_ref)
    for h in range(N_HEADS):
        _gated_output(h, z_ref, lsum[h], acc_ref, o_ref)


def _moba_attention(p, kaug, bsz, seq_len):
    bs = MOBA_BLOCK
    nb = seq_len // bs
    m = bsz * seq_len
    seg = lambda s: (lambda b, i: (s, b * nb + i, 0))
    seg_full = lambda s: (lambda b, i: (s, b, 0))
    kern = functools.partial(_moba_kernel, n_blocks=nb)
    nb_pad = max(16, nb)
    assert AUG_ONEHOT0 + nb <= LANES
    return pl.pallas_call(
        kern,
        grid=(bsz, nb),
        in_specs=[
            pl.BlockSpec((BLOCKS_PER_SEG, bs, LANES), seg(SEG_QB)),
            pl.BlockSpec((BLOCKS_PER_SEG, seq_len, LANES), seg_full(SEG_KB)),
            pl.BlockSpec((BLOCKS_PER_SEG, seq_len, LANES), seg_full(SEG_VB)),
            pl.BlockSpec((BLOCKS_PER_SEG, bs, LANES), seg(SEG_ZB)),
            pl.BlockSpec((seq_len, LANES), lambda b, i: (0, 0)),
        ],
        out_specs=pl.BlockSpec((bs, D_GROUP), lambda b, i: (b * nb + i, 0)),
        out_shape=jax.ShapeDtypeStruct((m, D_GROUP), BF16),
        scratch_shapes=[
            pltpu.VMEM((N_HEADS, nb_pad, HEAD_DIM), F32),
            pltpu.VMEM((N_HEADS, nb, HEAD_DIM, bs), BF16),
            pltpu.VMEM((LANES, bs), F32),
            pltpu.VMEM((N_HEADS, bs, 2 * LANES), BF16),
            pltpu.VMEM((N_HEADS, nb, bs, bs), F32),
            pltpu.VMEM((N_HEADS, 8, bs), F32),
            pltpu.VMEM((N_HEADS, 8, bs), F32),
            pltpu.VMEM((N_HEADS, HEAD_DIM, bs), F32),
        ],
        compiler_params=pltpu.CompilerParams(
            dimension_semantics=("arbitrary", "arbitrary"), vmem_limit_bytes=VMEM_LIMIT_BYTES),
        name="moba_attn",
    )(p, p, p, p, kaug)


OUT_CHUNK = 512


def _out_kernel(ya_ref, yb_ref, wa_ref, wb_ref, x_ref, mod_ref, o_ref):
    ya = ya_ref[...]
    yb = yb_ref[...]
    for c in range(o_ref.shape[1] // OUT_CHUNK):
        cs = slice(c * OUT_CHUNK, (c + 1) * OUT_CHUNK)
        y = jnp.dot(ya, wa_ref[:, cs], preferred_element_type=F32)
        y = y + jnp.dot(yb, wb_ref[:, cs], preferred_element_type=F32)
        o_ref[:, cs] = x_ref[:, cs] + mod_ref[0, 2:3, cs] * y


def _out_projection(ya, yb, w_out_bf, x2, mod3, seq_len):
    m, d = x2.shape
    tm = 512
    tiles_per_seq = seq_len // tm
    return pl.pallas_call(
        _out_kernel,
        grid=(m // tm,),
        in_specs=[
            pl.BlockSpec((tm, D_GROUP), lambda i: (i, 0)),
            pl.BlockSpec((tm, D_GROUP), lambda i: (i, 0)),
            pl.BlockSpec((D_GROUP, d), lambda i: (0, 0)),
            pl.BlockSpec((D_GROUP, d), lambda i: (1, 0)),
            pl.BlockSpec((tm, d), lambda i: (i, 0)),
            pl.BlockSpec((1, 3, d), lambda i: (i // tiles_per_seq, 0, 0)),
        ],
        out_specs=pl.BlockSpec((tm, d), lambda i: (i, 0)),
        out_shape=jax.ShapeDtypeStruct((m, d), F32),
        compiler_params=pltpu.CompilerParams(
            dimension_semantics=("arbitrary",), vmem_limit_bytes=VMEM_LIMIT_BYTES),
        name="out_proj",
    )(ya, yb, w_out_bf, w_out_bf, x2, mod3)


RELAYOUT_ROWS = 512
RELAYOUT_HEAD = 128


def _relayout_kernel(a_ref, b_ref, mid_ref, main_ref, tail_ref, *, n_a_tiles, n_mid):
    r = pl.program_id(0)

    @pl.when(r < n_a_tiles)
    def _():
        main_ref[...] = a_ref[...].astype(BF16)

    @pl.when(r >= n_a_tiles)
    def _():
        main_ref[:RELAYOUT_ROWS - n_mid, :] = a_ref[n_mid:, :].astype(BF16)
        main_ref[RELAYOUT_ROWS - n_mid:, :] = b_ref[:n_mid, :].astype(BF16)

    kidx = mid_ref[:IDX_DIM, :].astype(BF16)
    tail_ref[0:IDX_DIM, :] = kidx
    tail_ref[IDX_DIM:2 * IDX_DIM, :] = kidx
    tail_ref[2 * IDX_DIM:2 * IDX_DIM + IDX_HEADS, :] = mid_ref[IDX_DIM:IDX_DIM + IDX_HEADS, :].astype(BF16)
    tail_ref[2 * IDX_DIM + IDX_HEADS:, :] = jnp.zeros(
        (tail_ref.shape[0] - 2 * IDX_DIM - IDX_HEADS, tail_ref.shape[1]), BF16)


def _relayout_weights(w_t):
    n_in, d = w_t.shape
    n_a = 4 * D_GROUP + IDX_HEADS * IDX_DIM
    n_mid = IDX_DIM + IDX_HEADS
    tr = RELAYOUT_ROWS
    assert n_in == n_a + n_mid + 4 * D_GROUP and n_a % tr == 0 and (4 * D_GROUP) % tr == 0
    assert n_mid % 16 == 0 and n_mid <= RELAYOUT_HEAD and tr % RELAYOUT_HEAD == 0 and n_a % n_mid == 0
    n_main = n_in - n_mid
    n_a_tiles = n_a // tr
    return pl.pallas_call(
        functools.partial(_relayout_kernel, n_a_tiles=n_a_tiles, n_mid=n_mid),
        grid=(n_main // tr,),
        in_specs=[
            pl.BlockSpec((tr, d), lambda r: (r, 0)),
            pl.BlockSpec((RELAYOUT_HEAD, d),
                         lambda r: ((jnp.maximum(r, n_a_tiles - 1) + 1) * (tr // RELAYOUT_HEAD), 0)),
            pl.BlockSpec((n_mid, d), lambda r: (n_a // n_mid, 0)),
        ],
        out_specs=[
            pl.BlockSpec((tr, d), lambda r: (r, 0)),
            pl.BlockSpec((2 * LANES, d), lambda r: (0, 0)),
        ],
        out_shape=[
            jax.ShapeDtypeStruct((n_main, d), BF16),
            jax.ShapeDtypeStruct((2 * LANES, d), BF16),
        ],
        compiler_params=pltpu.CompilerParams(
            dimension_semantics=("arbitrary",), vmem_limit_bytes=VMEM_LIMIT_BYTES),
        name="w_relayout",
    )(w_t, w_t, w_t)


def _layer(x, c, w_ada, b_ada, g_norm, w_in, q_norm_a, k_norm_a, k_norm_idx, q_norm_b, k_norm_b, w_out):
    bsz, seq_len, d = x.shape
    assert seq_len % MOBA_BLOCK == 0 and seq_len % 1024 == 0 and d % 1024 == 0
    x2 = x.reshape(bsz * seq_len, d)

    mod3 = _modulation(c, w_ada, b_ada).reshape(bsz, 3, d)

    w_main, w_tail = _relayout_weights(jnp.swapaxes(w_in, 0, 1))
    ones = jnp.ones((HEAD_DIM,), F32)
    sm_scale = HEAD_DIM ** -0.5
    gains = jnp.stack([q_norm_a * sm_scale, k_norm_a, ones, ones, ones,
                       q_norm_b * sm_scale, k_norm_b, ones, ones]).reshape(N_SEGS, 1, HEAD_DIM)
    for h in range(N_HEADS):
        slope = _alibi_slope(h, N_HEADS)
        assert float(np.float32(slope).astype(jnp.bfloat16)) == slope, "ALiBi slopes must be exact in bf16"
    assert seq_len // POS_TILE <= 256 and POS_TILE <= 256, "positions must split into bf16-exact parts"
    kaug = _key_features(seq_len)
    gk = jnp.concatenate([k_norm_idx, k_norm_idx]).reshape(1, LANES)

    p, kk, wi = _projection(x2, mod3, g_norm.reshape(1, d), w_main, w_tail, gains, gk, seq_len)
    ya = _dsa_attention(p, kk, wi, kaug, bsz, seq_len)
    yb = _moba_attention(p, kaug, bsz, seq_len)
    out = _out_projection(ya, yb, w_out.astype(BF16), x2, mod3, seq_len)
    return out.reshape(bsz, seq_len, d)


def kernel(x, c, w_ada, b_ada, g_norm, w_in, q_norm_a, k_norm_a, k_norm_idx, q_norm_b, k_norm_b, w_out):
    for i in range(w_ada.shape[0]):
        x = _layer(x, c, w_ada[i], b_ada[i], g_norm[i], w_in[i], q_norm_a[i], k_norm_a[i],
                   k_norm_idx[i], q_norm_b[i], k_norm_b[i], w_out[i])
    return x
```

```python
import functools

import jax
import jax.numpy as jnp
import numpy as np
from jax import lax
from jax.experimental import pallas as pl
from jax.experimental.pallas import tpu as pltpu

F32 = jnp.float32
BF16 = jnp.bfloat16

HEAD_DIM = 128
N_HEADS = 8
D_GROUP = N_HEADS * HEAD_DIM
IDX_HEADS = 16
IDX_DIM = 64
DSA_TOPK_MAX = 256
MOBA_BLOCK = 256
MOBA_TOPK_MAX = 3
RMS_EPS = 1e-6
NEG_INF = -1e30
BIG = 1e30
LOG2E = 1.4426950408889634

LANES = 128
VMEM_LIMIT_BYTES = 56 * 1024 * 1024

SEG_QA, SEG_KA, SEG_VA, SEG_ZA, SEG_QIDX, SEG_QB, SEG_KB, SEG_VB, SEG_ZB = range(9)
N_SEGS = 9
BLOCKS_PER_SEG = D_GROUP // LANES


def _alibi_slope(h, n):
    return float(2.0 ** (-8.0 * (h + 1) / n))


def _dot_nt(a, b):
    return lax.dot_general(a, b, (((1,), (1,)), ((), ())), preferred_element_type=F32)


def _mod_kernel(c_ref, w_ref, b_ref, o_ref):
    c = c_ref[...]
    s = c * jax.nn.sigmoid(c)
    o_ref[...] = jnp.dot(s, w_ref[...], preferred_element_type=F32) + b_ref[...]


def _modulation(c, w_ada, b_ada):
    bsz, d = c.shape
    n = w_ada.shape[1]
    tn = 1024
    return pl.pallas_call(
        _mod_kernel,
        grid=(n // tn,),
        in_specs=[
            pl.BlockSpec((bsz, d), lambda j: (0, 0)),
            pl.BlockSpec((d, tn), lambda j: (0, j)),
            pl.BlockSpec((1, tn), lambda j: (0, j)),
        ],
        out_specs=pl.BlockSpec((bsz, tn), lambda j: (0, j)),
        out_shape=jax.ShapeDtypeStruct((bsz, n), F32),
        compiler_params=pltpu.CompilerParams(
            dimension_semantics=("arbitrary",), vmem_limit_bytes=VMEM_LIMIT_BYTES),
        name="adaln_mod",
    )(c, w_ada, b_ada.reshape(1, n))


PROJ_CHUNK = 2 * LANES


def _proj_kernel(x_ref, mod_ref, g_ref, w_ref, wt_ref, gain_ref, gk_ref,
                 p_ref, kk_ref, wi_ref, h_ref):
    j = pl.program_id(1)

    @pl.when(j == 0)
    def _():
        x = x_ref[...]
        ms = jnp.mean(x * x, axis=-1, keepdims=True)
        shift = mod_ref[0, 0:1, :]
        gs = g_ref[...] * (1.0 + mod_ref[0, 1:2, :])
        hb = (x * lax.rsqrt(ms + RMS_EPS) * gs + shift).astype(BF16)
        h_ref[...] = hb
        t = _dot_nt(hb, wt_ref[...])
        tk = t[:, :LANES]
        kms = jnp.mean(tk * tk, axis=-1, keepdims=True)
        kk_ref[...] = (tk * lax.rsqrt(kms + RMS_EPS) * gk_ref[...]).astype(BF16)
        wi_ref[...] = t[:, LANES:]

    def segment(with_norm):
        g = gain_ref[0]
        h = h_ref[...]
        for cc in range(D_GROUP // PROJ_CHUNK):
            acc = _dot_nt(h, w_ref[cc * PROJ_CHUNK:(cc + 1) * PROJ_CHUNK, :])
            for c in range(PROJ_CHUNK // LANES):
                a = acc[:, c * LANES:(c + 1) * LANES]
                if with_norm:
                    ms = jnp.mean(a * a, axis=-1, keepdims=True)
                    a = a * lax.rsqrt(ms + RMS_EPS) * g
                p_ref[cc * (PROJ_CHUNK // LANES) + c] = a.astype(BF16)

    is_norm = (j == SEG_QA) | (j == SEG_KA) | (j == SEG_QB) | (j == SEG_KB)
    pl.when(is_norm)(functools.partial(segment, True))
    pl.when(jnp.logical_not(is_norm))(functools.partial(segment, False))


def _projection(x2, mod3, g_norm, w_main, w_tail, gains, gk, seq_len):
    m, d = x2.shape
    tm = 1024
    tiles_per_seq = seq_len // tm
    return pl.pallas_call(
        _proj_kernel,
        grid=(m // tm, N_SEGS),
        in_specs=[
            pl.BlockSpec((tm, d), lambda i, j: (i, 0)),
            pl.BlockSpec((1, 3, d), lambda i, j: (i // tiles_per_seq, 0, 0)),
            pl.BlockSpec((1, d), lambda i, j: (0, 0)),
            pl.BlockSpec((D_GROUP, d), lambda i, j: (j, 0)),
            pl.BlockSpec((2 * LANES, d), lambda i, j: (0, 0)),
            pl.BlockSpec((1, 1, LANES), lambda i, j: (j, 0, 0)),
            pl.BlockSpec((1, LANES), lambda i, j: (0, 0)),
        ],
        out_specs=[
            pl.BlockSpec((BLOCKS_PER_SEG, tm, LANES), lambda i, j: (j, i, 0)),
            pl.BlockSpec((tm, LANES), lambda i, j: (i, 0)),
            pl.BlockSpec((tm, LANES), lambda i, j: (i, 0)),
        ],
        out_shape=[
            jax.ShapeDtypeStruct((N_SEGS * BLOCKS_PER_SEG, m, LANES), BF16),
            jax.ShapeDtypeStruct((m, LANES), BF16),
            jax.ShapeDtypeStruct((m, LANES), F32),
        ],
        scratch_shapes=[pltpu.VMEM((tm, d), BF16)],
        compiler_params=pltpu.CompilerParams(
            dimension_semantics=("arbitrary", "arbitrary"), vmem_limit_bytes=VMEM_LIMIT_BYTES),
        name="in_proj",
    )(x2, mod3, g_norm, w_main, w_tail, gains, gk)


POS_TILE = MOBA_BLOCK
AUG_KLOC, AUG_KBLK, AUG_ONE_LOC, AUG_ONE_BLK, AUG_ONEHOT0 = 0, 1, 2, 3, 8


def _key_features(seq_len):
    pos = np.arange(seq_len)
    f = np.zeros((seq_len, LANES), np.float32)
    f[:, AUG_KLOC] = pos % POS_TILE
    f[:, AUG_KBLK] = pos // POS_TILE
    f[:, AUG_ONE_LOC] = 1.0
    f[:, AUG_ONE_BLK] = 1.0
    f[pos, AUG_ONEHOT0 + pos // POS_TILE] = 1.0
    return jnp.asarray(f, BF16)


def _alibi_query_features(slope, q_tile, tq):
    lane = lax.broadcasted_iota(jnp.int32, (tq, LANES), 1)
    row = lax.broadcasted_iota(jnp.int32, (tq, LANES), 0).astype(F32)
    f = jnp.where(lane == AUG_KLOC, slope, 0.0)
    f = jnp.where(lane == AUG_KBLK, slope * POS_TILE, f)
    f = jnp.where(lane == AUG_ONE_LOC, -slope * row, f)
    return jnp.where(lane == AUG_ONE_BLK, (-slope * POS_TILE) * q_tile.astype(F32), f)


FOLD_CHAINS = 4


def _fold_rows(x, op):
    r = x.reshape(x.shape[0] // 8, 8, x.shape[1])
    n_acc = min(FOLD_CHAINS, r.shape[0])
    accs = [r[k] for k in range(n_acc)]
    for g in range(n_acc, r.shape[0]):
        accs[g % n_acc] = op(accs[g % n_acc], r[g])
    while len(accs) > 1:
        accs = [op(accs[k], accs[k + 1]) for k in range(0, len(accs) - 1, 2)] + accs[len(accs) - len(accs) % 2:]
    return accs[0]


TILE_CHUNKS = (4, 2, 1)


def _for_tile_chunks(n_tiles, fn, chunks=TILE_CHUNKS):
    big = chunks[0]
    shift = big.bit_length() - 1

    def many(c, carry):
        fn(c * big, big)
        return carry

    n_big = lax.shift_right_logical(n_tiles, shift)
    lax.fori_loop(0, n_big, many, 0)
    done = n_big * big
    for size in chunks[1:]:
        has = (n_tiles & size) != 0

        @pl.when(has)
        def _(done=done, size=size):
            fn(done, size)

        done = done + (n_tiles & size)


def _softmax_pass_a(n_tiles, tile, score_rows, s_ref, mx_ref):
    tq = s_ref.shape[-1]

    def pass_a(first, count):
        for h in range(N_HEADS):
            s = score_rows(h, first, count) * LOG2E
            s_ref[h, pl.ds(first, count)] = s.reshape(count, tile, tq)
            mx_ref[h] = jnp.maximum(mx_ref[h], _fold_rows(s, jnp.maximum))

    _for_tile_chunks(n_tiles, pass_a)
    return [jnp.max(mx_ref[h], axis=0, keepdims=True) for h in range(N_HEADS)]


def _softmax_pass_b(n_tiles, tile, m, value_cols, s_ref, ls_ref, acc_ref):
    tq = acc_ref.shape[-1]
    ls_ref[...] = jnp.zeros(ls_ref.shape, F32)
    acc_ref[...] = jnp.zeros(acc_ref.shape, F32)

    def pass_b(first, count):
        for h in range(N_HEADS):
            p = jnp.exp2(s_ref[h, pl.ds(first, count)].reshape(count * tile, tq) - m[h])
            ls_ref[h] += _fold_rows(p, jnp.add)
            vt = jnp.concatenate([value_cols(h, first + u) for u in range(count)], axis=1)
            acc_ref[h] += jnp.dot(vt, p.astype(BF16), preferred_element_type=F32)

    _for_tile_chunks(n_tiles, pass_b)
    return [jnp.sum(ls_ref[h], axis=0, keepdims=True) for h in range(N_HEADS)]


def _gated_output(h, z_ref, l, acc_ref, o_ref):
    z = z_ref[h].astype(F32)
    o = (acc_ref[h] / l).T * (z * jax.nn.sigmoid(z))
    o_ref[:, h * HEAD_DIM:(h + 1) * HEAD_DIM] = o.astype(BF16)


def _transpose_values(v_ref, vt_ref, n_tiles, tile):
    for h in range(N_HEADS):
        for j in range(n_tiles):
            vt_ref[h, j] = v_ref[h, j * tile:(j + 1) * tile, :].astype(F32).T.astype(BF16)


DSA_TQ = 256
DSA_TK = POS_TILE
SEARCH_STEPS = 4
UNCHECKED_ROUNDS = 4


def _dsa_kernel(q_ref, k_ref, v_ref, z_ref, qi_ref, kk_ref, wi_ref, kaug_ref, o_ref,
                vt_ref, qs_ref, sc_ref, ext_ref, t_ref, j_ref, qcat_ref, s_ref, mx_ref, ls_ref, acc_ref,
                *, top_k, seq_len):
    tq, tk = DSA_TQ, DSA_TK
    i = pl.program_id(1)
    t0 = i * tq
    nk = i + 1
    idx_scale = (IDX_DIM ** -0.5) * (IDX_HEADS ** -0.5)

    @pl.when(i == 0)
    def _():
        _transpose_values(v_ref, vt_ref, seq_len // tk, tk)

    lane = lax.broadcasted_iota(jnp.int32, (tq, LANES), 1)
    key_loc = lax.broadcasted_iota(jnp.int32, (tk, tq), 0).astype(F32)
    qry_loc = lax.broadcasted_iota(jnp.int32, (tk, tq), 1).astype(F32)

    for c in range(BLOCKS_PER_SEG):
        qp = qi_ref[c].astype(F32)
        qs_ref[2 * c] = jnp.where(lane < IDX_DIM, qp, 0.0).astype(BF16)
        qs_ref[2 * c + 1] = jnp.where(lane >= IDX_DIM, qp, 0.0).astype(BF16)
    w_t = (wi_ref[...] * idx_scale).T
    w_rows = [w_t[h:h + 1, :] for h in range(IDX_HEADS)]

    def score_tiles(first, count):
        for u in range(count):
            kt = first + u
            kk = kk_ref[pl.ds(pl.multiple_of(kt * tk, tk), tk), :]
            acc = jnp.zeros((tk, tq), F32)
            for h in range(IDX_HEADS):
                acc = acc + w_rows[h] * jnp.maximum(_dot_nt(kk, qs_ref[h]), 0.0)
            off = (kt * tk - t0).astype(F32)
            causal = (key_loc + off) <= qry_loc
            sc_ref[kt] = jnp.where(causal, acc, -jnp.inf)
            ext_ref[0] = jnp.minimum(ext_ref[0], _fold_rows(jnp.where(causal, acc, BIG), jnp.minimum))
            ext_ref[1] = jnp.maximum(ext_ref[1], _fold_rows(jnp.where(causal, acc, NEG_INF), jnp.maximum))

    ext_ref[0] = jnp.full((8, tq), BIG, F32)
    ext_ref[1] = jnp.full((8, tq), NEG_INF, F32)
    _for_tile_chunks(nk, score_tiles)

    t_ref[...] = jnp.full((1, tq), NEG_INF, F32)
    j_ref[...] = jnp.full((1, tq), -1.0, F32)

    def count(pred):
        def body(kt, acc):
            off = (kt * tk).astype(F32)
            return acc + _fold_rows(jnp.where(pred(sc_ref[kt], key_loc + off), 1.0, 0.0), jnp.add)
        return jnp.sum(lax.fori_loop(0, nk, body, jnp.zeros((8, tq), F32)), axis=0, keepdims=True)

    @pl.when(t0 + tq > top_k)
    def _():
        kf = float(top_k)
        qpos = lax.broadcasted_iota(jnp.int32, (1, tq), 1) + t0
        done0 = jnp.where(qpos + 1 <= top_k, 1.0, 0.0)

        lo0 = jnp.min(ext_ref[0], axis=0, keepdims=True)
        hi0 = jnp.max(ext_ref[1], axis=0, keepdims=True)

        def step(x, degen, state):
            lo, hi, thr, done, tie = state
            c = count(lambda s, pos: s > x)
            active = done == 0.0
            live = jnp.logical_and(active, jnp.logical_not(degen))
            found = jnp.logical_and(live, c == kf)
            new_tie = jnp.logical_and(active, degen)
            thr = jnp.where(found, x, jnp.where(new_tie, hi, thr))
            tie = jnp.where(new_tie, 1.0, tie)
            done = jnp.where(jnp.logical_or(found, new_tie), 1.0, done)
            lo = jnp.where(jnp.logical_and(live, c > kf), x, lo)
            hi = jnp.where(jnp.logical_and(live, c < kf), x, hi)
            return lo, hi, thr, done, tie

        state0 = (lo0, hi0, jnp.full((1, tq), NEG_INF, F32), done0, jnp.zeros((1, tq), F32))

        def bisect(state):
            for _ in range(SEARCH_STEPS):
                lo, hi = state[0], state[1]
                mid = 0.5 * lo + 0.5 * hi
                degen = jnp.logical_or(mid <= lo, mid >= hi)
                state = step(mid, degen, state)
            return state

        state1 = lax.fori_loop(0, UNCHECKED_ROUNDS, lambda _, st: bisect(st), state0)

        def cond(carry):
            return carry[1] > 0.0

        def body(carry):
            state = bisect(carry[0])
            return state, jnp.max(1.0 - state[3])

        (lo, hi, thr_loop, done, tie), _ = lax.while_loop(cond, body, (state1, jnp.max(1.0 - state1[3])))
        t_ref[...] = thr_loop

        @pl.when(jnp.max(tie) > 0.0)
        def _():
            reach_hi = count(lambda s, pos: s >= hi)
            thr = jnp.where(jnp.logical_and(tie > 0.0, reach_hi < kf), lo, thr_loop)
            t_ref[...] = thr
            need = kf - count(lambda s, pos: s > thr)
            n_bits = int(np.ceil(np.log2(seq_len + 1)))

            def jstep(_, carry):
                jlo, jhi = carry
                jmid = jnp.floor((jlo + jhi) * 0.5)
                c = count(lambda s, pos: jnp.logical_and(s == thr, pos <= jmid))
                ge = c >= need
                return jnp.where(ge, jlo, jmid), jnp.where(ge, jmid, jhi)

            _, jhi = lax.fori_loop(0, n_bits, jstep,
                                   (jnp.full((1, tq), -1.0, F32),
                                    jnp.full((1, tq), float(seq_len - 1), F32)))
            j_ref[...] = jnp.where(tie > 0.0, jhi, -1.0)

    def mask_tile(kt, carry):
        s = sc_ref[kt]
        thr = t_ref[...]
        off = (kt * tk).astype(F32)
        sel = jnp.logical_or(s > thr, jnp.logical_and(s == thr, (key_loc + off) <= j_ref[...]))
        sc_ref[kt] = jnp.where(sel, 0.0, NEG_INF)
        return carry

    lax.fori_loop(0, nk, mask_tile, 0)

    for h in range(N_HEADS):
        feat = _alibi_query_features(_alibi_slope(h, N_HEADS), i, tq)
        qcat_ref[h] = jnp.concatenate([q_ref[h], feat.astype(BF16)], axis=1)

    def score_rows(h, first, count):
        ks = pl.ds(pl.multiple_of(first * tk, tk), count * tk)
        kcat = jnp.concatenate([k_ref[h, ks, :], kaug_ref[ks, :]], axis=1)
        return _dot_nt(kcat, qcat_ref[h]) + sc_ref[pl.ds(first, count)].reshape(count * tk, tq)

    mx_ref[...] = jnp.full(mx_ref.shape, NEG_INF, F32)
    value_cols = lambda h, kt: vt_ref[h, kt]
    m = _softmax_pass_a(nk, tk, score_rows, s_ref, mx_ref)
    lsum = _softmax_pass_b(nk, tk, m, value_cols, s_ref, ls_ref, acc_ref)
    for h in range(N_HEADS):
        _gated_output(h, z_ref, lsum[h], acc_ref, o_ref)


def _dsa_attention(p, kk, wi, kaug, bsz, seq_len):
    tq, tk = DSA_TQ, DSA_TK
    nq = seq_len // tq
    nkt = seq_len // tk
    m = bsz * seq_len
    top_k = min(DSA_TOPK_MAX, seq_len // 4)
    seg = lambda s: (lambda b, i: (s, b * nq + i, 0))
    seg_full = lambda s: (lambda b, i: (s, b, 0))
    kern = functools.partial(_dsa_kernel, top_k=top_k, seq_len=seq_len)
    return pl.pallas_call(
        kern,
        grid=(bsz, nq),
        in_specs=[
            pl.BlockSpec((BLOCKS_PER_SEG, tq, LANES), seg(SEG_QA)),
            pl.BlockSpec((BLOCKS_PER_SEG, seq_len, LANES), seg_full(SEG_KA)),
            pl.BlockSpec((BLOCKS_PER_SEG, seq_len, LANES), seg_full(SEG_VA)),
            pl.BlockSpec((BLOCKS_PER_SEG, tq, LANES), seg(SEG_ZA)),
            pl.BlockSpec((BLOCKS_PER_SEG, tq, LANES), seg(SEG_QIDX)),
            pl.BlockSpec((seq_len, LANES), lambda b, i: (b, 0)),
            pl.BlockSpec((tq, LANES), lambda b, i: (b * nq + i, 0)),
            pl.BlockSpec((seq_len, LANES), lambda b, i: (0, 0)),
        ],
        out_specs=pl.BlockSpec((tq, D_GROUP), lambda b, i: (b * nq + i, 0)),
        out_shape=jax.ShapeDtypeStruct((m, D_GROUP), BF16),
        scratch_shapes=[
            pltpu.VMEM((N_HEADS, nkt, HEAD_DIM, tk), BF16),
            pltpu.VMEM((IDX_HEADS, tq, LANES), BF16),
            pltpu.VMEM((nkt, tk, tq), F32),
            pltpu.VMEM((2, 8, tq), F32),
            pltpu.VMEM((1, tq), F32),
            pltpu.VMEM((1, tq), F32),
            pltpu.VMEM((N_HEADS, tq, 2 * LANES), BF16),
            pltpu.VMEM((N_HEADS, nkt, tk, tq), F32),
            pltpu.VMEM((N_HEADS, 8, tq), F32),
            pltpu.VMEM((N_HEADS, 8, tq), F32),
            pltpu.VMEM((N_HEADS, HEAD_DIM, tq), F32),
        ],
        compiler_params=pltpu.CompilerParams(
            dimension_semantics=("arbitrary", "arbitrary"), vmem_limit_bytes=VMEM_LIMIT_BYTES),
        name="dsa_attn",
    )(p, p, p, p, p, kk, wi, kaug)


def _moba_kernel(q_ref, k_ref, v_ref, z_ref, kaug_ref, o_ref,
                 kmean_ref, vt_ref, selb_ref, qcat_ref, s_ref, mx_ref, ls_ref, acc_ref, *, n_blocks):
    bs = MOBA_BLOCK
    n = pl.program_id(1)
    top_k = min(MOBA_TOPK_MAX, n_blocks - 1)

    @pl.when(n == 0)
    def _():
        kmean_ref[...] = jnp.zeros(kmean_ref.shape, F32)
        for h in range(N_HEADS):
            for j in range(n_blocks):
                kb = k_ref[h, j * bs:(j + 1) * bs, :].astype(F32)
                kmean_ref[h, j:j + 1, :] = jnp.mean(kb, axis=0, keepdims=True)
        _transpose_values(v_ref, vt_ref, n_blocks, bs)

    nb_rows = -(-n_blocks // 8) * 8
    blk = lax.broadcasted_iota(jnp.int32, (nb_rows, bs), 0)
    past = blk < n

    selb_ref[...] = jnp.zeros(selb_ref.shape, F32)
    for h in range(N_HEADS):
        q = q_ref[h]
        g = _dot_nt(kmean_ref[h].astype(BF16), q)[:nb_rows]
        for j in range(n_blocks):
            gj = g[j:j + 1, :]
            beats = jnp.logical_or(g > gj, jnp.logical_and(g == gj, blk < j))
            rank = jnp.sum(jnp.where(jnp.logical_and(beats, past), 1.0, 0.0), axis=0, keepdims=True)
            dropped = jnp.logical_and(rank >= float(top_k), j < n)
            selb_ref[AUG_ONEHOT0 + j:AUG_ONEHOT0 + j + 1, :] = jnp.where(dropped, NEG_INF, 0.0)
        feat = _alibi_query_features(_alibi_slope(h, N_HEADS), n, bs) + selb_ref[...].T
        qcat_ref[h] = jnp.concatenate([q, feat.astype(BF16)], axis=1)

    key_loc = lax.broadcasted_iota(jnp.int32, (bs, bs), 0)
    qry_loc = lax.broadcasted_iota(jnp.int32, (bs, bs), 1)
    causal = key_loc <= qry_loc

    def score_rows(h, first, count):
        ks = pl.ds(pl.multiple_of(first * bs, bs), count * bs)
        kcat = jnp.concatenate([k_ref[h, ks, :], kaug_ref[ks, :]], axis=1)
        return _dot_nt(kcat, qcat_ref[h])

    for h in range(N_HEADS):
        s = jnp.where(causal, score_rows(h, n, 1), NEG_INF) * LOG2E
        s_ref[h, n] = s
        mx_ref[h] = _fold_rows(s, jnp.maximum)
    value_cols = lambda h, j: vt_ref[h, j]
    m = _softmax_pass_a(n, bs, score_rows, s_ref, mx_ref)
    lsum = _softmax_pass_b(n + 1, bs, m, value_cols, s_ref, ls_ref, acc_ref)
    for h in range(N_HEADS):
        _gated_output(h, z_ref, lsum[h], acc_ref, o_ref)


def _moba_attention(p, kaug, bsz, seq_len):
    bs = MOBA_BLOCK
    nb = seq_len // bs
    m = bsz * seq_len
    seg = lambda s: (lambda b, i: (s, b * nb + i, 0))
    seg_full = lambda s: (lambda b, i: (s, b, 0))
    kern = functools.partial(_moba_kernel, n_blocks=nb)
    nb_pad = max(16, nb)
    assert AUG_ONEHOT0 + nb <= LANES
    return pl.pallas_call(
        kern,
        grid=(bsz, nb),
        in_specs=[
            pl.BlockSpec((BLOCKS_PER_SEG, bs, LANES), seg(SEG_QB)),
            pl.BlockSpec((BLOCKS_PER_SEG, seq_len, LANES), seg_full(SEG_KB)),
            pl.BlockSpec((BLOCKS_PER_SEG, seq_len, LANES), seg_full(SEG_VB)),
            pl.BlockSpec((BLOCKS_PER_SEG, bs, LANES), seg(SEG_ZB)),
            pl.BlockSpec((seq_len, LANES), lambda b, i: (0, 0)),
        ],
        out_specs=pl.BlockSpec((bs, D_GROUP), lambda b, i: (b * nb + i, 0)),
        out_shape=jax.ShapeDtypeStruct((m, D_GROUP), BF16),
        scratch_shapes=[
            pltpu.VMEM((N_HEADS, nb_pad, HEAD_DIM), F32),
            pltpu.VMEM((N_HEADS, nb, HEAD_DIM, bs), BF16),
            pltpu.VMEM((LANES, bs), F32),
            pltpu.VMEM((N_HEADS, bs, 2 * LANES), BF16),
            pltpu.VMEM((N_HEADS, nb, bs, bs), F32),
            pltpu.VMEM((N_HEADS, 8, bs), F32),
            pltpu.VMEM((N_HEADS, 8, bs), F32),
            pltpu.VMEM((N_HEADS, HEAD_DIM, bs), F32),
        ],
        compiler_params=pltpu.CompilerParams(
            dimension_semantics=("arbitrary", "arbitrary"), vmem_limit_bytes=VMEM_LIMIT_BYTES),
        name="moba_attn",
    )(p, p, p, p, kaug)


OUT_CHUNK = 512


def _out_kernel(ya_ref, yb_ref, wa_ref, wb_ref, x_ref, mod_ref, o_ref):
    ya = ya_ref[...]
    yb = yb_ref[...]
    for c in range(o_ref.shape[1] // OUT_CHUNK):
        cs = slice(c * OUT_CHUNK, (c + 1) * OUT_CHUNK)
        y = jnp.dot(ya, wa_ref[:, cs], preferred_element_type=F32)
        y = y + jnp.dot(yb, wb_ref[:, cs], preferred_element_type=F32)
        o_ref[:, cs] = x_ref[:, cs] + mod_ref[0, 2:3, cs] * y


def _out_projection(ya, yb, w_out_bf, x2, mod3, seq_len):
    m, d = x2.shape
    tm = 512
    tiles_per_seq = seq_len // tm
    return pl.pallas_call(
        _out_kernel,
        grid=(m // tm,),
        in_specs=[
            pl.BlockSpec((tm, D_GROUP), lambda i: (i, 0)),
            pl.BlockSpec((tm, D_GROUP), lambda i: (i, 0)),
            pl.BlockSpec((D_GROUP, d), lambda i: (0, 0)),
            pl.BlockSpec((D_GROUP, d), lambda i: (1, 0)),
            pl.BlockSpec((tm, d), lambda i: (i, 0)),
            pl.BlockSpec((1, 3, d), lambda i: (i // tiles_per_seq, 0, 0)),
        ],
        out_specs=pl.BlockSpec((tm, d), lambda i: (i, 0)),
        out_shape=jax.ShapeDtypeStruct((m, d), F32),
        compiler_params=pltpu.CompilerParams(
            dimension_semantics=("arbitrary",), vmem_limit_bytes=VMEM_LIMIT_BYTES),
        name="out_proj",
    )(ya, yb, w_out_bf, w_out_bf, x2, mod3)


RELAYOUT_ROWS = 512
RELAYOUT_HEAD = 128


def _relayout_kernel(a_ref, b_ref, mid_ref, main_ref, tail_ref, *, n_a_tiles, n_mid):
    r = pl.program_id(0)

    @pl.when(r < n_a_tiles)
    def _():
        main_ref[...] = a_ref[...].astype(BF16)

    @pl.when(r >= n_a_tiles)
    def _():
        main_ref[:RELAYOUT_ROWS - n_mid, :] = a_ref[n_mid:, :].astype(BF16)
        main_ref[RELAYOUT_ROWS - n_mid:, :] = b_ref[:n_mid, :].astype(BF16)

    kidx = mid_ref[:IDX_DIM, :].astype(BF16)
    tail_ref[0:IDX_DIM, :] = kidx
    tail_ref[IDX_DIM:2 * IDX_DIM, :] = kidx
    tail_ref[2 * IDX_DIM:2 * IDX_DIM + IDX_HEADS, :] = mid_ref[IDX_DIM:IDX_DIM + IDX_HEADS, :].astype(BF16)
    tail_ref[2 * IDX_DIM + IDX_HEADS:, :] = jnp.zeros(
        (tail_ref.shape[0] - 2 * IDX_DIM - IDX_HEADS, tail_ref.shape[1]), BF16)


def _relayout_weights(w_t):
    n_in, d = w_t.shape
    n_a = 4 * D_GROUP + IDX_HEADS * IDX_DIM
    n_mid = IDX_DIM + IDX_HEADS
    tr = RELAYOUT_ROWS
    assert n_in == n_a + n_mid + 4 * D_GROUP and n_a % tr == 0 and (4 * D_GROUP) % tr == 0
    assert n_mid % 16 == 0 and n_mid <= RELAYOUT_HEAD and tr % RELAYOUT_HEAD == 0 and n_a % n_mid == 0
    n_main = n_in - n_mid
    n_a_tiles = n_a // tr
    return pl.pallas_call(
        functools.partial(_relayout_kernel, n_a_tiles=n_a_tiles, n_mid=n_mid),
        grid=(n_main // tr,),
        in_specs=[
            pl.BlockSpec((tr, d), lambda r: (r, 0)),
            pl.BlockSpec((RELAYOUT_HEAD, d),
                         lambda r: ((jnp.maximum(r, n_a_tiles - 1) + 1) * (tr // RELAYOUT_HEAD), 0)),
            pl.BlockSpec((n_mid, d), lambda r: (n_a // n_mid, 0)),
        ],
        out_specs=[
            pl.BlockSpec((tr, d), lambda r: (r, 0)),
            pl.BlockSpec((2 * LANES, d), lambda r: (0, 0)),
        ],
        out_shape=[
            jax.ShapeDtypeStruct((n_main, d), BF16),
            jax.ShapeDtypeStruct((2 * LANES, d), BF16),
        ],
        compiler_params=pltpu.CompilerParams(
            dimension_semantics=("arbitrary",), vmem_limit_bytes=VMEM_LIMIT_BYTES),
        name="w_relayout",
    )(w_t, w_t, w_t)


def _layer(x, c, w_ada, b_ada, g_norm, w_in, q_norm_a, k_norm_a, k_norm_idx, q_norm_b, k_norm_b, w_out):
    bsz, seq_len, d = x.shape
    assert seq_len % MOBA_BLOCK == 0 and seq_len % 1024 == 0 and d % 1024 == 0
    x2 = x.reshape(bsz * seq_len, d)

    mod3 = _modulation(c, w_ada, b_ada).reshape(bsz, 3, d)

    w_main, w_tail = _relayout_weights(jnp.swapaxes(w_in, 0, 1))
    ones = jnp.ones((HEAD_DIM,), F32)
    sm_scale = HEAD_DIM ** -0.5
    gains = jnp.stack([q_norm_a * sm_scale, k_norm_a, ones, ones, ones,
                       q_norm_b * sm_scale, k_norm_b, ones, ones]).reshape(N_SEGS, 1, HEAD_DIM)
    for h in range(N_HEADS):
        slope = _alibi_slope(h, N_HEADS)
        assert float(np.float32(slope).astype(jnp.bfloat16)) == slope, "ALiBi slopes must be exact in bf16"
    assert seq_len // POS_TILE <= 256 and POS_TILE <= 256, "positions must split into bf16-exact parts"
    kaug = _key_features(seq_len)
    gk = jnp.concatenate([k_norm_idx, k_norm_idx]).reshape(1, LANES)

    p, kk, wi = _projection(x2, mod3, g_norm.reshape(1, d), w_main, w_tail, gains, gk, seq_len)
    ya = _dsa_attention(p, kk, wi, kaug, bsz, seq_len)
    yb = _moba_attention(p, kaug, bsz, seq_len)
    out = _out_projection(ya, yb, w_out.astype(BF16), x2, mod3, seq_len)
    return out.reshape(bsz, seq_len, d)


def kernel(x, c, w_ada, b_ada, g_norm, w_in, q_norm_a, k_norm_a, k_norm_idx, q_norm_b, k_norm_b, w_out):
    for i in range(w_ada.shape[0]):
        x = _layer(x, c, w_ada[i], b_ada[i], g_norm[i], w_in[i], q_norm_a[i], k_norm_a[i],
                   k_norm_idx[i], q_norm_b[i], k_norm_b[i], w_out[i])
    return x
```

```python
import functools

import jax
import jax.numpy as jnp
import numpy as np
from jax import lax
from jax.experimental import pallas as pl
from jax.experimental.pallas import tpu as pltpu

F32 = jnp.float32
BF16 = jnp.bfloat16

HEAD_DIM = 128
N_HEADS = 8
D_GROUP = N_HEADS * HEAD_DIM
IDX_HEADS = 16
IDX_DIM = 64
DSA_TOPK_MAX = 256
MOBA_BLOCK = 256
MOBA_TOPK_MAX = 3
RMS_EPS = 1e-6
NEG_INF = -1e30
BIG = 1e30
LOG2E = 1.4426950408889634

LANES = 128
VMEM_LIMIT_BYTES = 56 * 1024 * 1024

SEG_QA, SEG_KA, SEG_VA, SEG_ZA, SEG_QIDX, SEG_QB, SEG_KB, SEG_VB, SEG_ZB = range(9)
N_SEGS = 9
BLOCKS_PER_SEG = D_GROUP // LANES


def _alibi_slope(h, n):
    return float(2.0 ** (-8.0 * (h + 1) / n))


def _dot_nt(a, b):
    return lax.dot_general(a, b, (((1,), (1,)), ((), ())), preferred_element_type=F32)


def _mod_kernel(c_ref, w_ref, b_ref, o_ref):
    c = c_ref[...]
    s = c * jax.nn.sigmoid(c)
    o_ref[...] = jnp.dot(s, w_ref[...], preferred_element_type=F32) + b_ref[...]


def _modulation(c, w_ada, b_ada):
    bsz, d = c.shape
    n = w_ada.shape[1]
    tn = 1024
    return pl.pallas_call(
        _mod_kernel,
        grid=(n // tn,),
        in_specs=[
            pl.BlockSpec((bsz, d), lambda j: (0, 0)),
            pl.BlockSpec((d, tn), lambda j: (0, j)),
            pl.BlockSpec((1, tn), lambda j: (0, j)),
        ],
        out_specs=pl.BlockSpec((bsz, tn), lambda j: (0, j)),
        out_shape=jax.ShapeDtypeStruct((bsz, n), F32),
        compiler_params=pltpu.CompilerParams(
            dimension_semantics=("arbitrary",), vmem_limit_bytes=VMEM_LIMIT_BYTES),
        name="adaln_mod",
    )(c, w_ada, b_ada.reshape(1, n))


PROJ_CHUNK = 2 * LANES


def _proj_kernel(x_ref, mod_ref, g_ref, w_ref, wt_ref, gain_ref, gk_ref,
                 p_ref, kk_ref, wi_ref, h_ref):
    j = pl.program_id(1)

    @pl.when(j == 0)
    def _():
        x = x_ref[...]
        ms = jnp.mean(x * x, axis=-1, keepdims=True)
        shift = mod_ref[0, 0:1, :]
        gs = g_ref[...] * (1.0 + mod_ref[0, 1:2, :])
        hb = (x * lax.rsqrt(ms + RMS_EPS) * gs + shift).astype(BF16)
        h_ref[...] = hb
        t = _dot_nt(hb, wt_ref[...])
        tk = t[:, :LANES]
        kms = jnp.mean(tk * tk, axis=-1, keepdims=True)
        kk_ref[...] = (tk * lax.rsqrt(kms + RMS_EPS) * gk_ref[...]).astype(BF16)
        wi_ref[...] = t[:, LANES:]

    def segment(with_norm):
        g = gain_ref[0]
        h = h_ref[...]
        for cc in range(D_GROUP // PROJ_CHUNK):
            acc = _dot_nt(h, w_ref[cc * PROJ_CHUNK:(cc + 1) * PROJ_CHUNK, :])
            for c in range(PROJ_CHUNK // LANES):
                a = acc[:, c * LANES:(c + 1) * LANES]
                if with_norm:
                    ms = jnp.mean(a * a, axis=-1, keepdims=True)
                    a = a * lax.rsqrt(ms + RMS_EPS) * g
                p_ref[cc * (PROJ_CHUNK // LANES) + c] = a.astype(BF16)

    is_norm = (j == SEG_QA) | (j == SEG_KA) | (j == SEG_QB) | (j == SEG_KB)
    pl.when(is_norm)(functools.partial(segment, True))
    pl.when(jnp.logical_not(is_norm))(functools.partial(segment, False))


def _projection(x2, mod3, g_norm, w_main, w_tail, gains, gk, seq_len):
    m, d = x2.shape
    tm = 1024
    tiles_per_seq = seq_len // tm
    return pl.pallas_call(
        _proj_kernel,
        grid=(m // tm, N_SEGS),
        in_specs=[
            pl.BlockSpec((tm, d), lambda i, j: (i, 0)),
            pl.BlockSpec((1, 3, d), lambda i, j: (i // tiles_per_seq, 0, 0)),
            pl.BlockSpec((1, d), lambda i, j: (0, 0)),
            pl.BlockSpec((D_GROUP, d), lambda i, j: (j, 0)),
            pl.BlockSpec((2 * LANES, d), lambda i, j: (0, 0)),
            pl.BlockSpec((1, 1, LANES), lambda i, j: (j, 0, 0)),
            pl.BlockSpec((1, LANES), lambda i, j: (0, 0)),
        ],
        out_specs=[
            pl.BlockSpec((BLOCKS_PER_SEG, tm, LANES), lambda i, j: (j, i, 0)),
            pl.BlockSpec((tm, LANES), lambda i, j: (i, 0)),
            pl.BlockSpec((tm, LANES), lambda i, j: (i, 0)),
        ],
        out_shape=[
            jax.ShapeDtypeStruct((N_SEGS * BLOCKS_PER_SEG, m, LANES), BF16),
            jax.ShapeDtypeStruct((m, LANES), BF16),
            jax.ShapeDtypeStruct((m, LANES), F32),
        ],
        scratch_shapes=[pltpu.VMEM((tm, d), BF16)],
        compiler_params=pltpu.CompilerParams(
            dimension_semantics=("arbitrary", "arbitrary"), vmem_limit_bytes=VMEM_LIMIT_BYTES),
        name="in_proj",
    )(x2, mod3, g_norm, w_main, w_tail, gains, gk)


POS_TILE = MOBA_BLOCK
AUG_KLOC, AUG_KBLK, AUG_ONE_LOC, AUG_ONE_BLK, AUG_ONEHOT0 = 0, 1, 2, 3, 8


def _key_features(seq_len):
    pos = np.arange(seq_len)
    f = np.zeros((seq_len, LANES), np.float32)
    f[:, AUG_KLOC] = pos % POS_TILE
    f[:, AUG_KBLK] = pos // POS_TILE
    f[:, AUG_ONE_LOC] = 1.0
    f[:, AUG_ONE_BLK] = 1.0
    f[pos, AUG_ONEHOT0 + pos // POS_TILE] = 1.0
    return jnp.asarray(f, BF16)


def _alibi_query_features(slope, q_tile, tq):
    lane = lax.broadcasted_iota(jnp.int32, (tq, LANES), 1)
    row = lax.broadcasted_iota(jnp.int32, (tq, LANES), 0).astype(F32)
    f = jnp.where(lane == AUG_KLOC, slope, 0.0)
    f = jnp.where(lane == AUG_KBLK, slope * POS_TILE, f)
    f = jnp.where(lane == AUG_ONE_LOC, -slope * row, f)
    return jnp.where(lane == AUG_ONE_BLK, (-slope * POS_TILE) * q_tile.astype(F32), f)


FOLD_CHAINS = 4


def _fold_rows(x, op):
    r = x.reshape(x.shape[0] // 8, 8, x.shape[1])
    n_acc = min(FOLD_CHAINS, r.shape[0])
    accs = [r[k] for k in range(n_acc)]
    for g in range(n_acc, r.shape[0]):
        accs[g % n_acc] = op(accs[g % n_acc], r[g])
    while len(accs) > 1:
        accs = [op(accs[k], accs[k + 1]) for k in range(0, len(accs) - 1, 2)] + accs[len(accs) - len(accs) % 2:]
    return accs[0]


TILE_CHUNKS = (4, 2, 1)


def _for_tile_chunks(n_tiles, fn, chunks=TILE_CHUNKS):
    big = chunks[0]
    shift = big.bit_length() - 1

    def many(c, carry):
        fn(c * big, big)
        return carry

    n_big = lax.shift_right_logical(n_tiles, shift)
    lax.fori_loop(0, n_big, many, 0)
    done = n_big * big
    for size in chunks[1:]:
        has = (n_tiles & size) != 0

        @pl.when(has)
        def _(done=done, size=size):
            fn(done, size)

        done = done + (n_tiles & size)


def _softmax_pass_a(n_tiles, tile, score_rows, s_ref, mx_ref):
    tq = s_ref.shape[-1]

    def pass_a(first, count):
        for h in range(N_HEADS):
            s = score_rows(h, first, count) * LOG2E
            s_ref[h, pl.ds(first, count)] = s.reshape(count, tile, tq)
            mx_ref[h] = jnp.maximum(mx_ref[h], _fold_rows(s, jnp.maximum))

    _for_tile_chunks(n_tiles, pass_a)
    return [jnp.max(mx_ref[h], axis=0, keepdims=True) for h in range(N_HEADS)]


def _softmax_pass_b(n_tiles, tile, m, value_cols, s_ref, ls_ref, acc_ref):
    tq = acc_ref.shape[-1]
    ls_ref[...] = jnp.zeros(ls_ref.shape, F32)
    acc_ref[...] = jnp.zeros(acc_ref.shape, F32)

    def pass_b(first, count):
        for h in range(N_HEADS):
            p = jnp.exp2(s_ref[h, pl.ds(first, count)].reshape(count * tile, tq) - m[h])
            ls_ref[h] += _fold_rows(p, jnp.add)
            vt = jnp.concatenate([value_cols(h, first + u) for u in range(count)], axis=1)
            acc_ref[h] += jnp.dot(vt, p.astype(BF16), preferred_element_type=F32)

    _for_tile_chunks(n_tiles, pass_b)
    return [jnp.sum(ls_ref[h], axis=0, keepdims=True) for h in range(N_HEADS)]


def _gated_output(h, z_ref, l, acc_ref, o_ref):
    z = z_ref[h].astype(F32)
    o = (acc_ref[h] / l).T * (z * jax.nn.sigmoid(z))
    o_ref[:, h * HEAD_DIM:(h + 1) * HEAD_DIM] = o.astype(BF16)


def _transpose_values(v_ref, vt_ref, n_tiles, tile):
    for h in range(N_HEADS):
        for j in range(n_tiles):
            vt_ref[h, j] = v_ref[h, j * tile:(j + 1) * tile, :].astype(F32).T.astype(BF16)


DSA_TQ = 256
DSA_TK = POS_TILE
SEARCH_STEPS = 4
SEARCH_SLACK_STEPS = 9


def _dsa_kernel(q_ref, k_ref, v_ref, z_ref, qi_ref, kk_ref, wi_ref, kaug_ref, o_ref,
                vt_ref, qs_ref, sc_ref, ext_ref, t_ref, j_ref, qcat_ref, s_ref, mx_ref, ls_ref, acc_ref,
                *, top_k, seq_len):
    tq, tk = DSA_TQ, DSA_TK
    i = pl.program_id(1)
    t0 = i * tq
    nk = i + 1
    idx_scale = (IDX_DIM ** -0.5) * (IDX_HEADS ** -0.5)

    @pl.when(i == 0)
    def _():
        _transpose_values(v_ref, vt_ref, seq_len // tk, tk)

    lane = lax.broadcasted_iota(jnp.int32, (tq, LANES), 1)
    key_loc = lax.broadcasted_iota(jnp.int32, (tk, tq), 0).astype(F32)
    qry_loc = lax.broadcasted_iota(jnp.int32, (tk, tq), 1).astype(F32)

    for c in range(BLOCKS_PER_SEG):
        qp = qi_ref[c].astype(F32)
        qs_ref[2 * c] = jnp.where(lane < IDX_DIM, qp, 0.0).astype(BF16)
        qs_ref[2 * c + 1] = jnp.where(lane >= IDX_DIM, qp, 0.0).astype(BF16)
    w_t = (wi_ref[...] * idx_scale).T
    w_rows = [w_t[h:h + 1, :] for h in range(IDX_HEADS)]

    def score_tiles(first, count):
        for u in range(count):
            kt = first + u
            kk = kk_ref[pl.ds(pl.multiple_of(kt * tk, tk), tk), :]
            acc = jnp.zeros((tk, tq), F32)
            for h in range(IDX_HEADS):
                acc = acc + w_rows[h] * jnp.maximum(_dot_nt(kk, qs_ref[h]), 0.0)
            off = (kt * tk - t0).astype(F32)
            causal = (key_loc + off) <= qry_loc
            sc_ref[kt] = jnp.where(causal, acc, -jnp.inf)
            ext_ref[0] = jnp.minimum(ext_ref[0], _fold_rows(jnp.where(causal, acc, BIG), jnp.minimum))
            ext_ref[1] = jnp.maximum(ext_ref[1], _fold_rows(jnp.where(causal, acc, NEG_INF), jnp.maximum))

    ext_ref[0] = jnp.full((8, tq), BIG, F32)
    ext_ref[1] = jnp.full((8, tq), NEG_INF, F32)
    _for_tile_chunks(nk, score_tiles)

    t_ref[...] = jnp.full((1, tq), NEG_INF, F32)
    j_ref[...] = jnp.full((1, tq), -1.0, F32)

    def count(pred):
        def body(kt, acc):
            off = (kt * tk).astype(F32)
            return acc + _fold_rows(jnp.where(pred(sc_ref[kt], key_loc + off), 1.0, 0.0), jnp.add)
        return jnp.sum(lax.fori_loop(0, nk, body, jnp.zeros((8, tq), F32)), axis=0, keepdims=True)

    def write_mask(with_ties):
        def mask_tile(kt, carry):
            s = sc_ref[kt]
            thr = t_ref[...]
            sel = s > thr
            if with_ties:
                off = (kt * tk).astype(F32)
                sel = jnp.logical_or(sel, jnp.logical_and(s == thr, (key_loc + off) <= j_ref[...]))
            sc_ref[kt] = jnp.where(sel, 0.0, NEG_INF)
            return carry

        lax.fori_loop(0, nk, mask_tile, 0)

    @pl.when(t0 + tq > top_k)
    def _():
        kf = float(top_k)
        qpos = lax.broadcasted_iota(jnp.int32, (1, tq), 1) + t0
        done0 = jnp.where(qpos + 1 <= top_k, 1.0, 0.0)

        lo0 = jnp.min(ext_ref[0], axis=0, keepdims=True)
        hi0 = jnp.max(ext_ref[1], axis=0, keepdims=True)

        def step(x, degen, state):
            lo, hi, thr, done, tie = state
            c = count(lambda s, pos: s > x)
            active = done == 0.0
            live = jnp.logical_and(active, jnp.logical_not(degen))
            found = jnp.logical_and(live, c == kf)
            new_tie = jnp.logical_and(active, degen)
            thr = jnp.where(found, x, jnp.where(new_tie, hi, thr))
            tie = jnp.where(new_tie, 1.0, tie)
            done = jnp.where(jnp.logical_or(found, new_tie), 1.0, done)
            lo = jnp.where(jnp.logical_and(live, c > kf), x, lo)
            hi = jnp.where(jnp.logical_and(live, c < kf), x, hi)
            return lo, hi, thr, done, tie

        state0 = (lo0, hi0, jnp.full((1, tq), NEG_INF, F32), done0, jnp.zeros((1, tq), F32))

        def bisect(state):
            for _ in range(SEARCH_STEPS):
                lo, hi = state[0], state[1]
                mid = 0.5 * lo + 0.5 * hi
                degen = jnp.logical_or(mid <= lo, mid >= hi)
                state = step(mid, degen, state)
            return state

        unchecked_rounds = (int(np.log2(seq_len)) + SEARCH_SLACK_STEPS) // SEARCH_STEPS
        state1 = lax.fori_loop(0, unchecked_rounds, lambda _, st: bisect(st), state0)

        def flags(state):
            return jnp.max(1.0 - state[3]), jnp.max(state[4])

        def cond(carry):
            return carry[1] > 0.0

        def body(carry):
            state = bisect(carry[0])
            return (state,) + flags(state)

        (lo, hi, thr_loop, done, tie), _, any_tie = lax.while_loop(cond, body, (state1,) + flags(state1))
        t_ref[...] = thr_loop

        @pl.when(any_tie > 0.0)
        def _():
            reach_hi = count(lambda s, pos: s >= hi)
            thr = jnp.where(jnp.logical_and(tie > 0.0, reach_hi < kf), lo, thr_loop)
            t_ref[...] = thr
            need = kf - count(lambda s, pos: s > thr)
            n_bits = int(np.ceil(np.log2(seq_len + 1)))

            def jstep(_, carry):
                jlo, jhi = carry
                jmid = jnp.floor((jlo + jhi) * 0.5)
                c = count(lambda s, pos: jnp.logical_and(s == thr, pos <= jmid))
                ge = c >= need
                return jnp.where(ge, jlo, jmid), jnp.where(ge, jmid, jhi)

            _, jhi = lax.fori_loop(0, n_bits, jstep,
                                   (jnp.full((1, tq), -1.0, F32),
                                    jnp.full((1, tq), float(seq_len - 1), F32)))
            j_ref[...] = jnp.where(tie > 0.0, jhi, -1.0)
            write_mask(True)

        pl.when(any_tie <= 0.0)(functools.partial(write_mask, False))

    pl.when(t0 + tq <= top_k)(functools.partial(write_mask, False))

    for h in range(N_HEADS):
        feat = _alibi_query_features(_alibi_slope(h, N_HEADS), i, tq)
        qcat_ref[h] = jnp.concatenate([q_ref[h], feat.astype(BF16)], axis=1)

    def score_rows(h, first, count):
        ks = pl.ds(pl.multiple_of(first * tk, tk), count * tk)
        kcat = jnp.concatenate([k_ref[h, ks, :], kaug_ref[ks, :]], axis=1)
        return _dot_nt(kcat, qcat_ref[h]) + sc_ref[pl.ds(first, count)].reshape(count * tk, tq)

    mx_ref[...] = jnp.full(mx_ref.shape, NEG_INF, F32)
    value_cols = lambda h, kt: vt_ref[h, kt]
    m = _softmax_pass_a(nk, tk, score_rows, s_ref, mx_ref)
    lsum = _softmax_pass_b(nk, tk, m, value_cols, s_ref, ls_ref, acc_ref)
    for h in range(N_HEADS):
        _gated_output(h, z_ref, lsum[h], acc_ref, o_ref)


def _dsa_attention(p, kk, wi, kaug, bsz, seq_len):
    tq, tk = DSA_TQ, DSA_TK
    nq = seq_len // tq
    nkt = seq_len // tk
    m = bsz * seq_len
    top_k = min(DSA_TOPK_MAX, seq_len // 4)
    seg = lambda s: (lambda b, i: (s, b * nq + i, 0))
    seg_full = lambda s: (lambda b, i: (s, b, 0))
    kern = functools.partial(_dsa_kernel, top_k=top_k, seq_len=seq_len)
    return pl.pallas_call(
        kern,
        grid=(bsz, nq),
        in_specs=[
            pl.BlockSpec((BLOCKS_PER_SEG, tq, LANES), seg(SEG_QA)),
            pl.BlockSpec((BLOCKS_PER_SEG, seq_len, LANES), seg_full(SEG_KA)),
            pl.BlockSpec((BLOCKS_PER_SEG, seq_len, LANES), seg_full(SEG_VA)),
            pl.BlockSpec((BLOCKS_PER_SEG, tq, LANES), seg(SEG_ZA)),
            pl.BlockSpec((BLOCKS_PER_SEG, tq, LANES), seg(SEG_QIDX)),
            pl.BlockSpec((seq_len, LANES), lambda b, i: (b, 0)),
            pl.BlockSpec((tq, LANES), lambda b, i: (b * nq + i, 0)),
            pl.BlockSpec((seq_len, LANES), lambda b, i: (0, 0)),
        ],
        out_specs=pl.BlockSpec((tq, D_GROUP), lambda b, i: (b * nq + i, 0)),
        out_shape=jax.ShapeDtypeStruct((m, D_GROUP), BF16),
        scratch_shapes=[
            pltpu.VMEM((N_HEADS, nkt, HEAD_DIM, tk), BF16),
            pltpu.VMEM((IDX_HEADS, tq, LANES), BF16),
            pltpu.VMEM((nkt, tk, tq), F32),
            pltpu.VMEM((2, 8, tq), F32),
            pltpu.VMEM((1, tq), F32),
            pltpu.VMEM((1, tq), F32),
            pltpu.VMEM((N_HEADS, tq, 2 * LANES), BF16),
            pltpu.VMEM((N_HEADS, nkt, tk, tq), F32),
            pltpu.VMEM((N_HEADS, 8, tq), F32),
            pltpu.VMEM((N_HEADS, 8, tq), F32),
            pltpu.VMEM((N_HEADS, HEAD_DIM, tq), F32),
        ],
        compiler_params=pltpu.CompilerParams(
            dimension_semantics=("arbitrary", "arbitrary"), vmem_limit_bytes=VMEM_LIMIT_BYTES),
        name="dsa_attn",
    )(p, p, p, p, p, kk, wi, kaug)


def _moba_kernel(q_ref, k_ref, v_ref, z_ref, kaug_ref, o_ref,
                 kmean_ref, vt_ref, selb_ref, qcat_ref, s_ref, mx_ref, ls_ref, acc_ref, *, n_blocks):
    bs = MOBA_BLOCK
    n = pl.program_id(1)
    top_k = min(MOBA_TOPK_MAX, n_blocks - 1)

    @pl.when(n == 0)
    def _():
        kmean_ref[...] = jnp.zeros(kmean_ref.shape, F32)
        for h in range(N_HEADS):
            for j in range(n_blocks):
                kb = k_ref[h, j * bs:(j + 1) * bs, :].astype(F32)
                kmean_ref[h, j:j + 1, :] = jnp.mean(kb, axis=0, keepdims=True)
        _transpose_values(v_ref, vt_ref, n_blocks, bs)

    nb_rows = -(-n_blocks // 8) * 8
    blk = lax.broadcasted_iota(jnp.int32, (nb_rows, bs), 0)
    past = blk < n

    selb_ref[...] = jnp.zeros(selb_ref.shape, F32)
    for h in range(N_HEADS):
        q = q_ref[h]
        g = _dot_nt(kmean_ref[h].astype(BF16), q)[:nb_rows]
        for j in range(n_blocks):
            gj = g[j:j + 1, :]
            beats = jnp.logical_or(g > gj, jnp.logical_and(g == gj, blk < j))
            rank = jnp.sum(jnp.where(jnp.logical_and(beats, past), 1.0, 0.0), axis=0, keepdims=True)
            dropped = jnp.logical_and(rank >= float(top_k), j < n)
            selb_ref[AUG_ONEHOT0 + j:AUG_ONEHOT0 + j + 1, :] = jnp.where(dropped, NEG_INF, 0.0)
        feat = _alibi_query_features(_alibi_slope(h, N_HEADS), n, bs) + selb_ref[...].T
        qcat_ref[h] = jnp.concatenate([q, feat.astype(BF16)], axis=1)

    key_loc = lax.broadcasted_iota(jnp.int32, (bs, bs), 0)
    qry_loc = lax.broadcasted_iota(jnp.int32, (bs, bs), 1)
    causal = key_loc <= qry_loc

    def score_rows(h, first, count):
        ks = pl.ds(pl.multiple_of(first * bs, bs), count * bs)
        kcat = jnp.concatenate([k_ref[h, ks, :], kaug_ref[ks, :]], axis=1)
        return _dot_nt(kcat, qcat_ref[h])

    for h in range(N_HEADS):
        s = jnp.where(causal, score_rows(h, n, 1), NEG_INF) * LOG2E
        s_ref[h, n] = s
        mx_ref[h] = _fold_rows(s, jnp.maximum)
    value_cols = lambda h, j: vt_ref[h, j]
    m = _softmax_pass_a(n, bs, score_rows, s_ref, mx_ref)
    lsum = _softmax_pass_b(n + 1, bs, m, value_cols, s_ref, ls_ref, acc_ref)
    for h in range(N_HEADS):
        _gated_output(h, z_ref, lsum[h], acc_ref, o_ref)


def _moba_attention(p, kaug, bsz, seq_len):
    bs = MOBA_BLOCK
    nb = seq_len // bs
    m = bsz * seq_len
    seg = lambda s: (lambda b, i: (s, b * nb + i, 0))
    seg_full = lambda s: (lambda b, i: (s, b, 0))
    kern = functools.partial(_moba_kernel, n_blocks=nb)
    nb_pad = max(16, nb)
    assert AUG_ONEHOT0 + nb <= LANES
    return pl.pallas_call(
        kern,
        grid=(bsz, nb),
        in_specs=[
            pl.BlockSpec((BLOCKS_PER_SEG, bs, LANES), seg(SEG_QB)),
            pl.BlockSpec((BLOCKS_PER_SEG, seq_len, LANES), seg_full(SEG_KB)),
            pl.BlockSpec((BLOCKS_PER_SEG, seq_len, LANES), seg_full(SEG_VB)),
            pl.BlockSpec((BLOCKS_PER_SEG, bs, LANES), seg(SEG_ZB)),
            pl.BlockSpec((seq_len, LANES), lambda b, i: (0, 0)),
        ],
        out_specs=pl.BlockSpec((bs, D_GROUP), lambda b, i: (b * nb + i, 0)),
        out_shape=jax.ShapeDtypeStruct((m, D_GROUP), BF16),
        scratch_shapes=[
            pltpu.VMEM((N_HEADS, nb_pad, HEAD_DIM), F32),
            pltpu.VMEM((N_HEADS, nb, HEAD_DIM, bs), BF16),
            pltpu.VMEM((LANES, bs), F32),
            pltpu.VMEM((N_HEADS, bs, 2 * LANES), BF16),
            pltpu.VMEM((N_HEADS, nb, bs, bs), F32),
            pltpu.VMEM((N_HEADS, 8, bs), F32),
            pltpu.VMEM((N_HEADS, 8, bs), F32),
            pltpu.VMEM((N_HEADS, HEAD_DIM, bs), F32),
        ],
        compiler_params=pltpu.CompilerParams(
            dimension_semantics=("arbitrary", "arbitrary"), vmem_limit_bytes=VMEM_LIMIT_BYTES),
        name="moba_attn",
    )(p, p, p, p, kaug)


OUT_CHUNK = 512


def _out_kernel(ya_ref, yb_ref, wa_ref, wb_ref, x_ref, mod_ref, o_ref):
    ya = ya_ref[...]
    yb = yb_ref[...]
    for c in range(o_ref.shape[1] // OUT_CHUNK):
        cs = slice(c * OUT_CHUNK, (c + 1) * OUT_CHUNK)
        y = jnp.dot(ya, wa_ref[:, cs], preferred_element_type=F32)
        y = y + jnp.dot(yb, wb_ref[:, cs], preferred_element_type=F32)
        o_ref[:, cs] = x_ref[:, cs] + mod_ref[0, 2:3, cs] * y


def _out_projection(ya, yb, w_out_bf, x2, mod3, seq_len):
    m, d = x2.shape
    tm = 512
    tiles_per_seq = seq_len // tm
    return pl.pallas_call(
        _out_kernel,
        grid=(m // tm,),
        in_specs=[
            pl.BlockSpec((tm, D_GROUP), lambda i: (i, 0)),
            pl.BlockSpec((tm, D_GROUP), lambda i: (i, 0)),
            pl.BlockSpec((D_GROUP, d), lambda i: (0, 0)),
            pl.BlockSpec((D_GROUP, d), lambda i: (1, 0)),
            pl.BlockSpec((tm, d), lambda i: (i, 0)),
            pl.BlockSpec((1, 3, d), lambda i: (i // tiles_per_seq, 0, 0)),
        ],
        out_specs=pl.BlockSpec((tm, d), lambda i: (i, 0)),
        out_shape=jax.ShapeDtypeStruct((m, d), F32),
        compiler_params=pltpu.CompilerParams(
            dimension_semantics=("arbitrary",), vmem_limit_bytes=VMEM_LIMIT_BYTES),
        name="out_proj",
    )(ya, yb, w_out_bf, w_out_bf, x2, mod3)


RELAYOUT_ROWS = 512
RELAYOUT_HEAD = 128


def _relayout_kernel(a_ref, b_ref, mid_ref, main_ref, tail_ref, *, n_a_tiles, n_mid):
    r = pl.program_id(0)

    @pl.when(r < n_a_tiles)
    def _():
        main_ref[...] = a_ref[...].astype(BF16)

    @pl.when(r >= n_a_tiles)
    def _():
        main_ref[:RELAYOUT_ROWS - n_mid, :] = a_ref[n_mid:, :].astype(BF16)
        main_ref[RELAYOUT_ROWS - n_mid:, :] = b_ref[:n_mid, :].astype(BF16)

    kidx = mid_ref[:IDX_DIM, :].astype(BF16)
    tail_ref[0:IDX_DIM, :] = kidx
    tail_ref[IDX_DIM:2 * IDX_DIM, :] = kidx
    tail_ref[2 * IDX_DIM:2 * IDX_DIM + IDX_HEADS, :] = mid_ref[IDX_DIM:IDX_DIM + IDX_HEADS, :].astype(BF16)
    tail_ref[2 * IDX_DIM + IDX_HEADS:, :] = jnp.zeros(
        (tail_ref.shape[0] - 2 * IDX_DIM - IDX_HEADS, tail_ref.shape[1]), BF16)


def _relayout_weights(w_t):
    n_in, d = w_t.shape
    n_a = 4 * D_GROUP + IDX_HEADS * IDX_DIM
    n_mid = IDX_DIM + IDX_HEADS
    tr = RELAYOUT_ROWS
    assert n_in == n_a + n_mid + 4 * D_GROUP and n_a % tr == 0 and (4 * D_GROUP) % tr == 0
    assert n_mid % 16 == 0 and n_mid <= RELAYOUT_HEAD and tr % RELAYOUT_HEAD == 0 and n_a % n_mid == 0
    n_main = n_in - n_mid
    n_a_tiles = n_a // tr
    return pl.pallas_call(
        functools.partial(_relayout_kernel, n_a_tiles=n_a_tiles, n_mid=n_mid),
        grid=(n_main // tr,),
        in_specs=[
            pl.BlockSpec((tr, d), lambda r: (r, 0)),
            pl.BlockSpec((RELAYOUT_HEAD, d),
                         lambda r: ((jnp.maximum(r, n_a_tiles - 1) + 1) * (tr // RELAYOUT_HEAD), 0)),
            pl.BlockSpec((n_mid, d), lambda r: (n_a // n_mid, 0)),
        ],
        out_specs=[
            pl.BlockSpec((tr, d), lambda r: (r, 0)),
            pl.BlockSpec((2 * LANES, d), lambda r: (0, 0)),
        ],
        out_shape=[
            jax.ShapeDtypeStruct((n_main, d), BF16),
            jax.ShapeDtypeStruct((2 * LANES, d), BF16),
        ],
        compiler_params=pltpu.CompilerParams(
            dimension_semantics=("arbitrary",), vmem_limit_bytes=VMEM_LIMIT_BYTES),
        name="w_relayout",
    )(w_t, w_t, w_t)


def _layer(x, c, w_ada, b_ada, g_norm, w_in, q_norm_a, k_norm_a, k_norm_idx, q_norm_b, k_norm_b, w_out):
    bsz, seq_len, d = x.shape
    assert seq_len % MOBA_BLOCK == 0 and seq_len % 1024 == 0 and d % 1024 == 0
    x2 = x.reshape(bsz * seq_len, d)

    mod3 = _modulation(c, w_ada, b_ada).reshape(bsz, 3, d)

    w_main, w_tail = _relayout_weights(jnp.swapaxes(w_in, 0, 1))
    ones = jnp.ones((HEAD_DIM,), F32)
    sm_scale = HEAD_DIM ** -0.5
    gains = jnp.stack([q_norm_a * sm_scale, k_norm_a, ones, ones, ones,
                       q_norm_b * sm_scale, k_norm_b, ones, ones]).reshape(N_SEGS, 1, HEAD_DIM)
    for h in range(N_HEADS):
        slope = _alibi_slope(h, N_HEADS)
        assert float(np.float32(slope).astype(jnp.bfloat16)) == slope, "ALiBi slopes must be exact in bf16"
    assert seq_len // POS_TILE <= 256 and POS_TILE <= 256, "positions must split into bf16-exact parts"
    kaug = _key_features(seq_len)
    gk = jnp.concatenate([k_norm_idx, k_norm_idx]).reshape(1, LANES)

    p, kk, wi = _projection(x2, mod3, g_norm.reshape(1, d), w_main, w_tail, gains, gk, seq_len)
    ya = _dsa_attention(p, kk, wi, kaug, bsz, seq_len)
    yb = _moba_attention(p, kaug, bsz, seq_len)
    out = _out_projection(ya, yb, w_out.astype(BF16), x2, mod3, seq_len)
    return out.reshape(bsz, seq_len, d)


def kernel(x, c, w_ada, b_ada, g_norm, w_in, q_norm_a, k_norm_a, k_norm_idx, q_norm_b, k_norm_b, w_out):
    for i in range(w_ada.shape[0]):
        x = _layer(x, c, w_ada[i], b_ada[i], g_norm[i], w_in[i], q_norm_a[i], k_norm_a[i],
                   k_norm_idx[i], q_norm_b[i], k_norm_b[i], w_out[i])
    return x
```

```python
import functools

import jax
import jax.numpy as jnp
import numpy as np
from jax import lax
from jax.experimental import pallas as pl
from jax.experimental.pallas import tpu as pltpu

F32 = jnp.float32
BF16 = jnp.bfloat16

HEAD_DIM = 128
N_HEADS = 8
D_GROUP = N_HEADS * HEAD_DIM
IDX_HEADS = 16
IDX_DIM = 64
DSA_TOPK_MAX = 256
MOBA_BLOCK = 256
MOBA_TOPK_MAX = 3
RMS_EPS = 1e-6
NEG_INF = -1e30
BIG = 1e30
LOG2E = 1.4426950408889634

LANES = 128
VMEM_LIMIT_BYTES = 56 * 1024 * 1024

SEG_QA, SEG_KA, SEG_VA, SEG_ZA, SEG_QIDX, SEG_QB, SEG_KB, SEG_VB, SEG_ZB = range(9)
N_SEGS = 9
BLOCKS_PER_SEG = D_GROUP // LANES
SEG_HAS_NORM = (True, True, False, False, False, True, True, False, False)
SEGS_PER_STEP = 2
N_SEG_STEPS = -(-N_SEGS // SEGS_PER_STEP)


def _alibi_slope(h, n):
    return float(2.0 ** (-8.0 * (h + 1) / n))


def _dot_nt(a, b):
    return lax.dot_general(a, b, (((1,), (1,)), ((), ())), preferred_element_type=F32)


def _mod_kernel(c_ref, w_ref, b_ref, o_ref):
    c = c_ref[...]
    s = c * jax.nn.sigmoid(c)
    o_ref[...] = jnp.dot(s, w_ref[...], preferred_element_type=F32) + b_ref[...]


def _modulation(c, w_ada, b_ada):
    bsz, d = c.shape
    n = w_ada.shape[1]
    tn = 1024
    return pl.pallas_call(
        _mod_kernel,
        grid=(n // tn,),
        in_specs=[
            pl.BlockSpec((bsz, d), lambda j: (0, 0)),
            pl.BlockSpec((d, tn), lambda j: (0, j)),
            pl.BlockSpec((1, tn), lambda j: (0, j)),
        ],
        out_specs=pl.BlockSpec((bsz, tn), lambda j: (0, j)),
        out_shape=jax.ShapeDtypeStruct((bsz, n), F32),
        compiler_params=pltpu.CompilerParams(
            dimension_semantics=("arbitrary",), vmem_limit_bytes=VMEM_LIMIT_BYTES),
        name="adaln_mod",
    )(c, w_ada, b_ada.reshape(1, n))


PROJ_CHUNK = 2 * LANES


def _proj_kernel(x_ref, mod_ref, g_ref, wa_ref, wb_ref, wt_ref, gain_ref, gk_ref,
                 p_ref, kk_ref, wi_ref, h_ref):
    j = pl.program_id(1)

    @pl.when(j == 0)
    def _():
        x = x_ref[...]
        ms = jnp.mean(x * x, axis=-1, keepdims=True)
        shift = mod_ref[0, 0:1, :]
        gs = g_ref[...] * (1.0 + mod_ref[0, 1:2, :])
        hb = (x * lax.rsqrt(ms + RMS_EPS) * gs + shift).astype(BF16)
        h_ref[...] = hb
        t = _dot_nt(hb, wt_ref[...])
        tk = t[:, :LANES]
        kms = jnp.mean(tk * tk, axis=-1, keepdims=True)
        kk_ref[...] = (tk * lax.rsqrt(kms + RMS_EPS) * gk_ref[...]).astype(BF16)
        wi_ref[...] = t[:, LANES:]

    def segment(w_ref, g, with_norm, base):
        h = h_ref[...]
        for cc in range(D_GROUP // PROJ_CHUNK):
            acc = _dot_nt(h, w_ref[cc * PROJ_CHUNK:(cc + 1) * PROJ_CHUNK, :])
            for c in range(PROJ_CHUNK // LANES):
                a = acc[:, c * LANES:(c + 1) * LANES]
                if with_norm:
                    ms = jnp.mean(a * a, axis=-1, keepdims=True)
                    a = a * lax.rsqrt(ms + RMS_EPS) * g
                p_ref[base + cc * (PROJ_CHUNK // LANES) + c] = a.astype(BF16)

    def pair(jj):
        first, second = SEGS_PER_STEP * jj, SEGS_PER_STEP * jj + 1
        segment(wa_ref, gain_ref[0, 0:1], SEG_HAS_NORM[first], 0)
        if second < N_SEGS:
            segment(wb_ref, gain_ref[0, 1:2], SEG_HAS_NORM[second], BLOCKS_PER_SEG)
        else:
            p_ref[BLOCKS_PER_SEG:] = jnp.zeros((BLOCKS_PER_SEG,) + p_ref.shape[1:], BF16)

    for jj in range(N_SEG_STEPS):
        pl.when(j == jj)(functools.partial(pair, jj))


def _projection(x2, mod3, g_norm, w_main, w_tail, gains, gk, seq_len):
    m, d = x2.shape
    tm = 1024
    tiles_per_seq = seq_len // tm
    return pl.pallas_call(
        _proj_kernel,
        grid=(m // tm, N_SEG_STEPS),
        in_specs=[
            pl.BlockSpec((tm, d), lambda i, j: (i, 0)),
            pl.BlockSpec((1, 3, d), lambda i, j: (i // tiles_per_seq, 0, 0)),
            pl.BlockSpec((1, d), lambda i, j: (0, 0)),
            pl.BlockSpec((D_GROUP, d), lambda i, j: (SEGS_PER_STEP * j, 0)),
            pl.BlockSpec((D_GROUP, d), lambda i, j: (jnp.minimum(SEGS_PER_STEP * j + 1, N_SEGS - 1), 0)),
            pl.BlockSpec((2 * LANES, d), lambda i, j: (0, 0)),
            pl.BlockSpec((1, SEGS_PER_STEP, LANES), lambda i, j: (j, 0, 0)),
            pl.BlockSpec((1, LANES), lambda i, j: (0, 0)),
        ],
        out_specs=[
            pl.BlockSpec((SEGS_PER_STEP * BLOCKS_PER_SEG, tm, LANES), lambda i, j: (j, i, 0)),
            pl.BlockSpec((tm, LANES), lambda i, j: (i, 0)),
            pl.BlockSpec((tm, LANES), lambda i, j: (i, 0)),
        ],
        out_shape=[
            jax.ShapeDtypeStruct((N_SEG_STEPS * SEGS_PER_STEP * BLOCKS_PER_SEG, m, LANES), BF16),
            jax.ShapeDtypeStruct((m, LANES), BF16),
            jax.ShapeDtypeStruct((m, LANES), F32),
        ],
        scratch_shapes=[pltpu.VMEM((tm, d), BF16)],
        compiler_params=pltpu.CompilerParams(
            dimension_semantics=("arbitrary", "arbitrary"), vmem_limit_bytes=VMEM_LIMIT_BYTES),
        name="in_proj",
    )(x2, mod3, g_norm, w_main, w_main, w_tail, gains, gk)


POS_TILE = MOBA_BLOCK
AUG_KLOC, AUG_KBLK, AUG_ONE_LOC, AUG_ONE_BLK, AUG_ONEHOT0 = 0, 1, 2, 3, 8


def _key_features(seq_len):
    pos = np.arange(seq_len)
    f = np.zeros((seq_len, LANES), np.float32)
    f[:, AUG_KLOC] = pos % POS_TILE
    f[:, AUG_KBLK] = pos // POS_TILE
    f[:, AUG_ONE_LOC] = 1.0
    f[:, AUG_ONE_BLK] = 1.0
    f[pos, AUG_ONEHOT0 + pos // POS_TILE] = 1.0
    return jnp.asarray(f, BF16)


def _alibi_query_features(slope, q_tile, tq):
    lane = lax.broadcasted_iota(jnp.int32, (tq, LANES), 1)
    row = lax.broadcasted_iota(jnp.int32, (tq, LANES), 0).astype(F32)
    f = jnp.where(lane == AUG_KLOC, slope, 0.0)
    f = jnp.where(lane == AUG_KBLK, slope * POS_TILE, f)
    f = jnp.where(lane == AUG_ONE_LOC, -slope * row, f)
    return jnp.where(lane == AUG_ONE_BLK, (-slope * POS_TILE) * q_tile.astype(F32), f)


FOLD_CHAINS = 4


def _fold_rows(x, op):
    r = x.reshape(x.shape[0] // 8, 8, x.shape[1])
    n_acc = min(FOLD_CHAINS, r.shape[0])
    accs = [r[k] for k in range(n_acc)]
    for g in range(n_acc, r.shape[0]):
        accs[g % n_acc] = op(accs[g % n_acc], r[g])
    while len(accs) > 1:
        accs = [op(accs[k], accs[k + 1]) for k in range(0, len(accs) - 1, 2)] + accs[len(accs) - len(accs) % 2:]
    return accs[0]


TILE_CHUNKS = (4, 2, 1)


def _for_tile_chunks(n_tiles, fn, chunks=TILE_CHUNKS):
    big = chunks[0]
    shift = big.bit_length() - 1

    def many(c, carry):
        fn(c * big, big)
        return carry

    n_big = lax.shift_right_logical(n_tiles, shift)
    lax.fori_loop(0, n_big, many, 0)
    done = n_big * big
    for size in chunks[1:]:
        has = (n_tiles & size) != 0

        @pl.when(has)
        def _(done=done, size=size):
            fn(done, size)

        done = done + (n_tiles & size)


def _softmax_pass_a(n_tiles, tile, score_rows, s_ref, mx_ref):
    tq = s_ref.shape[-1]

    def pass_a(first, count):
        for h in range(N_HEADS):
            s = score_rows(h, first, count) * LOG2E
            s_ref[h, pl.ds(first, count)] = s.reshape(count, tile, tq)
            mx_ref[h] = jnp.maximum(mx_ref[h], _fold_rows(s, jnp.maximum))

    _for_tile_chunks(n_tiles, pass_a)
    return [jnp.max(mx_ref[h], axis=0, keepdims=True) for h in range(N_HEADS)]


def _softmax_pass_b(n_tiles, tile, m, value_cols, s_ref, ls_ref, acc_ref):
    tq = acc_ref.shape[-1]
    ls_ref[...] = jnp.zeros(ls_ref.shape, F32)
    acc_ref[...] = jnp.zeros(acc_ref.shape, F32)

    def pass_b(first, count):
        for h in range(N_HEADS):
            p = jnp.exp2(s_ref[h, pl.ds(first, count)].reshape(count * tile, tq) - m[h])
            ls_ref[h] += _fold_rows(p, jnp.add)
            vt = jnp.concatenate([value_cols(h, first + u) for u in range(count)], axis=1)
            acc_ref[h] += jnp.dot(vt, p.astype(BF16), preferred_element_type=F32)

    _for_tile_chunks(n_tiles, pass_b)
    return [jnp.sum(ls_ref[h], axis=0, keepdims=True) for h in range(N_HEADS)]


def _gated_output(h, z_ref, l, acc_ref, o_ref):
    z = z_ref[h].astype(F32)
    o = (acc_ref[h] / l).T * (z * jax.nn.sigmoid(z))
    o_ref[:, h * HEAD_DIM:(h + 1) * HEAD_DIM] = o.astype(BF16)


def _transpose_values(v_ref, vt_ref, n_tiles, tile):
    for h in range(N_HEADS):
        for j in range(n_tiles):
            vt_ref[h, j] = v_ref[h, j * tile:(j + 1) * tile, :].astype(F32).T.astype(BF16)


DSA_TQ = 256
DSA_TK = POS_TILE
SEARCH_STEPS = 4
SEARCH_SLACK_STEPS = 9


def _dsa_kernel(q_ref, k_ref, v_ref, z_ref, qi_ref, kk_ref, wi_ref, kaug_ref, o_ref,
                vt_ref, qs_ref, sc_ref, ext_ref, t_ref, j_ref, qcat_ref, s_ref, mx_ref, ls_ref, acc_ref,
                *, top_k, seq_len):
    tq, tk = DSA_TQ, DSA_TK
    i = pl.program_id(1)
    t0 = i * tq
    nk = i + 1
    idx_scale = (IDX_DIM ** -0.5) * (IDX_HEADS ** -0.5)

    @pl.when(i == 0)
    def _():
        _transpose_values(v_ref, vt_ref, seq_len // tk, tk)

    lane = lax.broadcasted_iota(jnp.int32, (tq, LANES), 1)
    key_loc = lax.broadcasted_iota(jnp.int32, (tk, tq), 0).astype(F32)
    qry_loc = lax.broadcasted_iota(jnp.int32, (tk, tq), 1).astype(F32)

    for c in range(BLOCKS_PER_SEG):
        qp = qi_ref[c].astype(F32)
        qs_ref[2 * c] = jnp.where(lane < IDX_DIM, qp, 0.0).astype(BF16)
        qs_ref[2 * c + 1] = jnp.where(lane >= IDX_DIM, qp, 0.0).astype(BF16)
    w_t = (wi_ref[...] * idx_scale).T
    w_rows = [w_t[h:h + 1, :] for h in range(IDX_HEADS)]

    def score_tiles(first, count):
        for u in range(count):
            kt = first + u
            kk = kk_ref[pl.ds(pl.multiple_of(kt * tk, tk), tk), :]
            acc = jnp.zeros((tk, tq), F32)
            for h in range(IDX_HEADS):
                acc = acc + w_rows[h] * jnp.maximum(_dot_nt(kk, qs_ref[h]), 0.0)
            off = (kt * tk - t0).astype(F32)
            causal = (key_loc + off) <= qry_loc
            sc_ref[kt] = jnp.where(causal, acc, -jnp.inf)
            ext_ref[0] = jnp.minimum(ext_ref[0], _fold_rows(jnp.where(causal, acc, BIG), jnp.minimum))
            ext_ref[1] = jnp.maximum(ext_ref[1], _fold_rows(jnp.where(causal, acc, NEG_INF), jnp.maximum))

    ext_ref[0] = jnp.full((8, tq), BIG, F32)
    ext_ref[1] = jnp.full((8, tq), NEG_INF, F32)
    _for_tile_chunks(nk, score_tiles)

    t_ref[...] = jnp.full((1, tq), NEG_INF, F32)
    j_ref[...] = jnp.full((1, tq), -1.0, F32)

    def count(pred):
        def body(kt, acc):
            off = (kt * tk).astype(F32)
            return acc + _fold_rows(jnp.where(pred(sc_ref[kt], key_loc + off), 1.0, 0.0), jnp.add)
        return jnp.sum(lax.fori_loop(0, nk, body, jnp.zeros((8, tq), F32)), axis=0, keepdims=True)

    def write_mask(with_ties):
        def mask_tile(kt, carry):
            s = sc_ref[kt]
            thr = t_ref[...]
            sel = s > thr
            if with_ties:
                off = (kt * tk).astype(F32)
                sel = jnp.logical_or(sel, jnp.logical_and(s == thr, (key_loc + off) <= j_ref[...]))
            sc_ref[kt] = jnp.where(sel, 0.0, NEG_INF)
            return carry

        lax.fori_loop(0, nk, mask_tile, 0)

    @pl.when(t0 + tq > top_k)
    def _():
        kf = float(top_k)
        qpos = lax.broadcasted_iota(jnp.int32, (1, tq), 1) + t0
        done0 = jnp.where(qpos + 1 <= top_k, 1.0, 0.0)

        lo0 = jnp.min(ext_ref[0], axis=0, keepdims=True)
        hi0 = jnp.max(ext_ref[1], axis=0, keepdims=True)

        def step(x, degen, state):
            lo, hi, thr, done, tie = state
            c = count(lambda s, pos: s > x)
            active = done == 0.0
            live = jnp.logical_and(active, jnp.logical_not(degen))
            found = jnp.logical_and(live, c == kf)
            new_tie = jnp.logical_and(active, degen)
            thr = jnp.where(found, x, jnp.where(new_tie, hi, thr))
            tie = jnp.where(new_tie, 1.0, tie)
            done = jnp.where(jnp.logical_or(found, new_tie), 1.0, done)
            lo = jnp.where(jnp.logical_and(live, c > kf), x, lo)
            hi = jnp.where(jnp.logical_and(live, c < kf), x, hi)
            return lo, hi, thr, done, tie

        state0 = (lo0, hi0, jnp.full((1, tq), NEG_INF, F32), done0, jnp.zeros((1, tq), F32))

        def bisect(state):
            for _ in range(SEARCH_STEPS):
                lo, hi = state[0], state[1]
                mid = 0.5 * lo + 0.5 * hi
                degen = jnp.logical_or(mid <= lo, mid >= hi)
                state = step(mid, degen, state)
            return state

        unchecked_rounds = (int(np.log2(seq_len)) + SEARCH_SLACK_STEPS) // SEARCH_STEPS
        state1 = lax.fori_loop(0, unchecked_rounds, lambda _, st: bisect(st), state0)

        def flags(state):
            return jnp.max(1.0 - state[3]), jnp.max(state[4])

        def cond(carry):
            return carry[1] > 0.0

        def body(carry):
            state = bisect(carry[0])
            return (state,) + flags(state)

        (lo, hi, thr_loop, done, tie), _, any_tie = lax.while_loop(cond, body, (state1,) + flags(state1))
        t_ref[...] = thr_loop

        @pl.when(any_tie > 0.0)
        def _():
            reach_hi = count(lambda s, pos: s >= hi)
            thr = jnp.where(jnp.logical_and(tie > 0.0, reach_hi < kf), lo, thr_loop)
            t_ref[...] = thr
            need = kf - count(lambda s, pos: s > thr)
            n_bits = int(np.ceil(np.log2(seq_len + 1)))

            def jstep(_, carry):
                jlo, jhi = carry
                jmid = jnp.floor((jlo + jhi) * 0.5)
                c = count(lambda s, pos: jnp.logical_and(s == thr, pos <= jmid))
                ge = c >= need
                return jnp.where(ge, jlo, jmid), jnp.where(ge, jmid, jhi)

            _, jhi = lax.fori_loop(0, n_bits, jstep,
                                   (jnp.full((1, tq), -1.0, F32),
                                    jnp.full((1, tq), float(seq_len - 1), F32)))
            j_ref[...] = jnp.where(tie > 0.0, jhi, -1.0)
            write_mask(True)

        pl.when(any_tie <= 0.0)(functools.partial(write_mask, False))

    pl.when(t0 + tq <= top_k)(functools.partial(write_mask, False))

    for h in range(N_HEADS):
        feat = _alibi_query_features(_alibi_slope(h, N_HEADS), i, tq)
        qcat_ref[h] = jnp.concatenate([q_ref[h], feat.astype(BF16)], axis=1)

    def score_rows(h, first, count):
        ks = pl.ds(pl.multiple_of(first * tk, tk), count * tk)
        kcat = jnp.concatenate([k_ref[h, ks, :], kaug_ref[ks, :]], axis=1)
        return _dot_nt(kcat, qcat_ref[h]) + sc_ref[pl.ds(first, count)].reshape(count * tk, tq)

    mx_ref[...] = jnp.full(mx_ref.shape, NEG_INF, F32)
    value_cols = lambda h, kt: vt_ref[h, kt]
    m = _softmax_pass_a(nk, tk, score_rows, s_ref, mx_ref)
    lsum = _softmax_pass_b(nk, tk, m, value_cols, s_ref, ls_ref, acc_ref)
    for h in range(N_HEADS):
        _gated_output(h, z_ref, lsum[h], acc_ref, o_ref)


def _dsa_attention(p, kk, wi, kaug, bsz, seq_len):
    tq, tk = DSA_TQ, DSA_TK
    nq = seq_len // tq
    nkt = seq_len // tk
    m = bsz * seq_len
    top_k = min(DSA_TOPK_MAX, seq_len // 4)
    seg = lambda s: (lambda b, i: (s, b * nq + i, 0))
    seg_full = lambda s: (lambda b, i: (s, b, 0))
    kern = functools.partial(_dsa_kernel, top_k=top_k, seq_len=seq_len)
    return pl.pallas_call(
        kern,
        grid=(bsz, nq),
        in_specs=[
            pl.BlockSpec((BLOCKS_PER_SEG, tq, LANES), seg(SEG_QA)),
            pl.BlockSpec((BLOCKS_PER_SEG, seq_len, LANES), seg_full(SEG_KA)),
            pl.BlockSpec((BLOCKS_PER_SEG, seq_len, LANES), seg_full(SEG_VA)),
            pl.BlockSpec((BLOCKS_PER_SEG, tq, LANES), seg(SEG_ZA)),
            pl.BlockSpec((BLOCKS_PER_SEG, tq, LANES), seg(SEG_QIDX)),
            pl.BlockSpec((seq_len, LANES), lambda b, i: (b, 0)),
            pl.BlockSpec((tq, LANES), lambda b, i: (b * nq + i, 0)),
            pl.BlockSpec((seq_len, LANES), lambda b, i: (0, 0)),
        ],
        out_specs=pl.BlockSpec((tq, D_GROUP), lambda b, i: (b * nq + i, 0)),
        out_shape=jax.ShapeDtypeStruct((m, D_GROUP), BF16),
        scratch_shapes=[
            pltpu.VMEM((N_HEADS, nkt, HEAD_DIM, tk), BF16),
            pltpu.VMEM((IDX_HEADS, tq, LANES), BF16),
            pltpu.VMEM((nkt, tk, tq), F32),
            pltpu.VMEM((2, 8, tq), F32),
            pltpu.VMEM((1, tq), F32),
            pltpu.VMEM((1, tq), F32),
            pltpu.VMEM((N_HEADS, tq, 2 * LANES), BF16),
            pltpu.VMEM((N_HEADS, nkt, tk, tq), F32),
            pltpu.VMEM((N_HEADS, 8, tq), F32),
            pltpu.VMEM((N_HEADS, 8, tq), F32),
            pltpu.VMEM((N_HEADS, HEAD_DIM, tq), F32),
        ],
        compiler_params=pltpu.CompilerParams(
            dimension_semantics=("arbitrary", "arbitrary"), vmem_limit_bytes=VMEM_LIMIT_BYTES),
        name="dsa_attn",
    )(p, p, p, p, p, kk, wi, kaug)


def _moba_kernel(q_ref, k_ref, v_ref, z_ref, kaug_ref, o_ref,
                 kmean_ref, vt_ref, selb_ref, qcat_ref, s_ref, mx_ref, ls_ref, acc_ref, *, n_blocks):
    bs = MOBA_BLOCK
    n = pl.program_id(1)
    top_k = min(MOBA_TOPK_MAX, n_blocks - 1)

    @pl.when(n == 0)
    def _():
        kmean_ref[...] = jnp.zeros(kmean_ref.shape, F32)
        for h in range(N_HEADS):
            for j in range(n_blocks):
                kb = k_ref[h, j * bs:(j + 1) * bs, :].astype(F32)
                kmean_ref[h, j:j + 1, :] = jnp.mean(kb, axis=0, keepdims=True)
        _transpose_values(v_ref, vt_ref, n_blocks, bs)

    nb_rows = -(-n_blocks // 8) * 8
    blk = lax.broadcasted_iota(jnp.int32, (nb_rows, bs), 0)
    past = blk < n

    selb_ref[...] = jnp.zeros(selb_ref.shape, F32)
    for h in range(N_HEADS):
        q = q_ref[h]
        g = _dot_nt(kmean_ref[h].astype(BF16), q)[:nb_rows]
        for j in range(n_blocks):
            gj = g[j:j + 1, :]
            beats = jnp.logical_or(g > gj, jnp.logical_and(g == gj, blk < j))
            rank = jnp.sum(jnp.where(jnp.logical_and(beats, past), 1.0, 0.0), axis=0, keepdims=True)
            dropped = jnp.logical_and(rank >= float(top_k), j < n)
            selb_ref[AUG_ONEHOT0 + j:AUG_ONEHOT0 + j + 1, :] = jnp.where(dropped, NEG_INF, 0.0)
        feat = _alibi_query_features(_alibi_slope(h, N_HEADS), n, bs) + selb_ref[...].T
        qcat_ref[h] = jnp.concatenate([q, feat.astype(BF16)], axis=1)

    key_loc = lax.broadcasted_iota(jnp.int32, (bs, bs), 0)
    qry_loc = lax.broadcasted_iota(jnp.int32, (bs, bs), 1)
    causal = key_loc <= qry_loc

    def score_rows(h, first, count):
        ks = pl.ds(pl.multiple_of(first * bs, bs), count * bs)
        kcat = jnp.concatenate([k_ref[h, ks, :], kaug_ref[ks, :]], axis=1)
        return _dot_nt(kcat, qcat_ref[h])

    for h in range(N_HEADS):
        s = jnp.where(causal, score_rows(h, n, 1), NEG_INF) * LOG2E
        s_ref[h, n] = s
        mx_ref[h] = _fold_rows(s, jnp.maximum)
    value_cols = lambda h, j: vt_ref[h, j]
    m = _softmax_pass_a(n, bs, score_rows, s_ref, mx_ref)
    lsum = _softmax_pass_b(n + 1, bs, m, value_cols, s_ref, ls_ref, acc_ref)
    for h in range(N_HEADS):
        _gated_output(h, z_ref, lsum[h], acc_ref, o_ref)


def _moba_attention(p, kaug, bsz, seq_len):
    bs = MOBA_BLOCK
    nb = seq_len // bs
    m = bsz * seq_len
    seg = lambda s: (lambda b, i: (s, b * nb + i, 0))
    seg_full = lambda s: (lambda b, i: (s, b, 0))
    kern = functools.partial(_moba_kernel, n_blocks=nb)
    nb_pad = max(16, nb)
    assert AUG_ONEHOT0 + nb <= LANES
    return pl.pallas_call(
        kern,
        grid=(bsz, nb),
        in_specs=[
            pl.BlockSpec((BLOCKS_PER_SEG, bs, LANES), seg(SEG_QB)),
            pl.BlockSpec((BLOCKS_PER_SEG, seq_len, LANES), seg_full(SEG_KB)),
            pl.BlockSpec((BLOCKS_PER_SEG, seq_len, LANES), seg_full(SEG_VB)),
            pl.BlockSpec((BLOCKS_PER_SEG, bs, LANES), seg(SEG_ZB)),
            pl.BlockSpec((seq_len, LANES), lambda b, i: (0, 0)),
        ],
        out_specs=pl.BlockSpec((bs, D_GROUP), lambda b, i: (b * nb + i, 0)),
        out_shape=jax.ShapeDtypeStruct((m, D_GROUP), BF16),
        scratch_shapes=[
            pltpu.VMEM((N_HEADS, nb_pad, HEAD_DIM), F32),
            pltpu.VMEM((N_HEADS, nb, HEAD_DIM, bs), BF16),
            pltpu.VMEM((LANES, bs), F32),
            pltpu.VMEM((N_HEADS, bs, 2 * LANES), BF16),
            pltpu.VMEM((N_HEADS, nb, bs, bs), F32),
            pltpu.VMEM((N_HEADS, 8, bs), F32),
            pltpu.VMEM((N_HEADS, 8, bs), F32),
            pltpu.VMEM((N_HEADS, HEAD_DIM, bs), F32),
        ],
        compiler_params=pltpu.CompilerParams(
            dimension_semantics=("arbitrary", "arbitrary"), vmem_limit_bytes=VMEM_LIMIT_BYTES),
        name="moba_attn",
    )(p, p, p, p, kaug)


OUT_CHUNK = 512


def _out_kernel(ya_ref, yb_ref, wa_ref, wb_ref, x_ref, mod_ref, o_ref):
    ya = ya_ref[...]
    yb = yb_ref[...]
    for c in range(o_ref.shape[1] // OUT_CHUNK):
        cs = slice(c * OUT_CHUNK, (c + 1) * OUT_CHUNK)
        y = jnp.dot(ya, wa_ref[:, cs], preferred_element_type=F32)
        y = y + jnp.dot(yb, wb_ref[:, cs], preferred_element_type=F32)
        o_ref[:, cs] = x_ref[:, cs] + mod_ref[0, 2:3, cs] * y


def _out_projection(ya, yb, w_out_bf, x2, mod3, seq_len):
    m, d = x2.shape
    tm = 512
    tiles_per_seq = seq_len // tm
    return pl.pallas_call(
        _out_kernel,
        grid=(m // tm,),
        in_specs=[
            pl.BlockSpec((tm, D_GROUP), lambda i: (i, 0)),
            pl.BlockSpec((tm, D_GROUP), lambda i: (i, 0)),
            pl.BlockSpec((D_GROUP, d), lambda i: (0, 0)),
            pl.BlockSpec((D_GROUP, d), lambda i: (1, 0)),
            pl.BlockSpec((tm, d), lambda i: (i, 0)),
            pl.BlockSpec((1, 3, d), lambda i: (i // tiles_per_seq, 0, 0)),
        ],
        out_specs=pl.BlockSpec((tm, d), lambda i: (i, 0)),
        out_shape=jax.ShapeDtypeStruct((m, d), F32),
        compiler_params=pltpu.CompilerParams(
            dimension_semantics=("arbitrary",), vmem_limit_bytes=VMEM_LIMIT_BYTES),
        name="out_proj",
    )(ya, yb, w_out_bf, w_out_bf, x2, mod3)


RELAYOUT_ROWS = 512
RELAYOUT_HEAD = 128


def _relayout_kernel(a_ref, b_ref, mid_ref, main_ref, tail_ref, *, n_a_tiles, n_mid):
    r = pl.program_id(0)

    @pl.when(r < n_a_tiles)
    def _():
        main_ref[...] = a_ref[...].astype(BF16)

    @pl.when(r >= n_a_tiles)
    def _():
        main_ref[:RELAYOUT_ROWS - n_mid, :] = a_ref[n_mid:, :].astype(BF16)
        main_ref[RELAYOUT_ROWS - n_mid:, :] = b_ref[:n_mid, :].astype(BF16)

    kidx = mid_ref[:IDX_DIM, :].astype(BF16)
    tail_ref[0:IDX_DIM, :] = kidx
    tail_ref[IDX_DIM:2 * IDX_DIM, :] = kidx
    tail_ref[2 * IDX_DIM:2 * IDX_DIM + IDX_HEADS, :] = mid_ref[IDX_DIM:IDX_DIM + IDX_HEADS, :].astype(BF16)
    tail_ref[2 * IDX_DIM + IDX_HEADS:, :] = jnp.zeros(
        (tail_ref.shape[0] - 2 * IDX_DIM - IDX_HEADS, tail_ref.shape[1]), BF16)


def _relayout_weights(w_t):
    n_in, d = w_t.shape
    n_a = 4 * D_GROUP + IDX_HEADS * IDX_DIM
    n_mid = IDX_DIM + IDX_HEADS
    tr = RELAYOUT_ROWS
    assert n_in == n_a + n_mid + 4 * D_GROUP and n_a % tr == 0 and (4 * D_GROUP) % tr == 0
    assert n_mid % 16 == 0 and n_mid <= RELAYOUT_HEAD and tr % RELAYOUT_HEAD == 0 and n_a % n_mid == 0
    n_main = n_in - n_mid
    n_a_tiles = n_a // tr
    return pl.pallas_call(
        functools.partial(_relayout_kernel, n_a_tiles=n_a_tiles, n_mid=n_mid),
        grid=(n_main // tr,),
        in_specs=[
            pl.BlockSpec((tr, d), lambda r: (r, 0)),
            pl.BlockSpec((RELAYOUT_HEAD, d),
                         lambda r: ((jnp.maximum(r, n_a_tiles - 1) + 1) * (tr // RELAYOUT_HEAD), 0)),
            pl.BlockSpec((n_mid, d), lambda r: (n_a // n_mid, 0)),
        ],
        out_specs=[
            pl.BlockSpec((tr, d), lambda r: (r, 0)),
            pl.BlockSpec((2 * LANES, d), lambda r: (0, 0)),
        ],
        out_shape=[
            jax.ShapeDtypeStruct((n_main, d), BF16),
            jax.ShapeDtypeStruct((2 * LANES, d), BF16),
        ],
        compiler_params=pltpu.CompilerParams(
            dimension_semantics=("arbitrary",), vmem_limit_bytes=VMEM_LIMIT_BYTES),
        name="w_relayout",
    )(w_t, w_t, w_t)


def _layer(x, c, w_ada, b_ada, g_norm, w_in, q_norm_a, k_norm_a, k_norm_idx, q_norm_b, k_norm_b, w_out):
    bsz, seq_len, d = x.shape
    assert seq_len % MOBA_BLOCK == 0 and seq_len % 1024 == 0 and d % 1024 == 0
    x2 = x.reshape(bsz * seq_len, d)

    mod3 = _modulation(c, w_ada, b_ada).reshape(bsz, 3, d)

    w_main, w_tail = _relayout_weights(jnp.swapaxes(w_in, 0, 1))
    ones = jnp.ones((HEAD_DIM,), F32)
    sm_scale = HEAD_DIM ** -0.5
    gains = jnp.stack([q_norm_a * sm_scale, k_norm_a, ones, ones, ones,
                       q_norm_b * sm_scale, k_norm_b, ones, ones, ones]
                      ).reshape(N_SEG_STEPS, SEGS_PER_STEP, HEAD_DIM)
    for h in range(N_HEADS):
        slope = _alibi_slope(h, N_HEADS)
        assert float(np.float32(slope).astype(jnp.bfloat16)) == slope, "ALiBi slopes must be exact in bf16"
    assert seq_len // POS_TILE <= 256 and POS_TILE <= 256, "positions must split into bf16-exact parts"
    kaug = _key_features(seq_len)
    gk = jnp.concatenate([k_norm_idx, k_norm_idx]).reshape(1, LANES)

    p, kk, wi = _projection(x2, mod3, g_norm.reshape(1, d), w_main, w_tail, gains, gk, seq_len)
    ya = _dsa_attention(p, kk, wi, kaug, bsz, seq_len)
    yb = _moba_attention(p, kaug, bsz, seq_len)
    out = _out_projection(ya, yb, w_out.astype(BF16), x2, mod3, seq_len)
    return out.reshape(bsz, seq_len, d)


def kernel(x, c, w_ada, b_ada, g_norm, w_in, q_norm_a, k_norm_a, k_norm_idx, q_norm_b, k_norm_b, w_out):
    for i in range(w_ada.shape[0]):
        x = _layer(x, c, w_ada[i], b_ada[i], g_norm[i], w_in[i], q_norm_a[i], k_norm_a[i],
                   k_norm_idx[i], q_norm_b[i], k_norm_b[i], w_out[i])
    return x
```

```python
import functools

import jax
import jax.numpy as jnp
import numpy as np
from jax import lax
from jax.experimental import pallas as pl
from jax.experimental.pallas import tpu as pltpu

F32 = jnp.float32
BF16 = jnp.bfloat16

HEAD_DIM = 128
N_HEADS = 8
D_GROUP = N_HEADS * HEAD_DIM
IDX_HEADS = 16
IDX_DIM = 64
DSA_TOPK_MAX = 256
MOBA_BLOCK = 256
MOBA_TOPK_MAX = 3
RMS_EPS = 1e-6
NEG_INF = -1e30
BIG = 1e30
LOG2E = 1.4426950408889634

LANES = 128
VMEM_LIMIT_BYTES = 56 * 1024 * 1024

SEG_QA, SEG_KA, SEG_VA, SEG_ZA, SEG_QIDX, SEG_QB, SEG_KB, SEG_VB, SEG_ZB = range(9)
N_SEGS = 9
BLOCKS_PER_SEG = D_GROUP // LANES


def _alibi_slope(h, n):
    return float(2.0 ** (-8.0 * (h + 1) / n))


def _dot_nt(a, b):
    return lax.dot_general(a, b, (((1,), (1,)), ((), ())), preferred_element_type=F32)


def _mod_kernel(c_ref, w_ref, b_ref, o_ref):
    c = c_ref[...]
    s = c * jax.nn.sigmoid(c)
    o_ref[...] = jnp.dot(s, w_ref[...], preferred_element_type=F32) + b_ref[...]


def _modulation(c, w_ada, b_ada):
    bsz, d = c.shape
    n = w_ada.shape[1]
    tn = 1024
    return pl.pallas_call(
        _mod_kernel,
        grid=(n // tn,),
        in_specs=[
            pl.BlockSpec((bsz, d), lambda j: (0, 0)),
            pl.BlockSpec((d, tn), lambda j: (0, j)),
            pl.BlockSpec((1, tn), lambda j: (0, j)),
        ],
        out_specs=pl.BlockSpec((bsz, tn), lambda j: (0, j)),
        out_shape=jax.ShapeDtypeStruct((bsz, n), F32),
        compiler_params=pltpu.CompilerParams(
            dimension_semantics=("arbitrary",), vmem_limit_bytes=VMEM_LIMIT_BYTES),
        name="adaln_mod",
    )(c, w_ada, b_ada.reshape(1, n))


PROJ_CHUNK = 2 * LANES


def _proj_kernel(x_ref, mod_ref, g_ref, w_ref, wt_ref, gain_ref, gk_ref,
                 p_ref, kk_ref, wi_ref, h_ref):
    j = pl.program_id(1)

    @pl.when(j == 0)
    def _():
        x = x_ref[...]
        ms = jnp.mean(x * x, axis=-1, keepdims=True)
        shift = mod_ref[0, 0:1, :]
        gs = g_ref[...] * (1.0 + mod_ref[0, 1:2, :])
        hb = (x * lax.rsqrt(ms + RMS_EPS) * gs + shift).astype(BF16)
        h_ref[...] = hb
        t = _dot_nt(hb, wt_ref[...])
        tk = t[:, :LANES]
        kms = jnp.mean(tk * tk, axis=-1, keepdims=True)
        kk_ref[...] = (tk * lax.rsqrt(kms + RMS_EPS) * gk_ref[...]).astype(BF16)
        wi_ref[...] = t[:, LANES:]

    def segment(with_norm):
        g = gain_ref[0]
        h = h_ref[...]
        for cc in range(D_GROUP // PROJ_CHUNK):
            acc = _dot_nt(h, w_ref[cc * PROJ_CHUNK:(cc + 1) * PROJ_CHUNK, :])
            for c in range(PROJ_CHUNK // LANES):
                a = acc[:, c * LANES:(c + 1) * LANES]
                if with_norm:
                    ms = jnp.mean(a * a, axis=-1, keepdims=True)
                    a = a * lax.rsqrt(ms + RMS_EPS) * g
                p_ref[cc * (PROJ_CHUNK // LANES) + c] = a.astype(BF16)

    seg = _segment_of_step(pl.program_id(0), j)
    is_norm = (seg == SEG_QA) | (seg == SEG_KA) | (seg == SEG_QB) | (seg == SEG_KB)
    pl.when(is_norm)(functools.partial(segment, True))
    pl.when(jnp.logical_not(is_norm))(functools.partial(segment, False))


def _segment_of_step(i, j):
    return jnp.where(i % 2 == 0, j, N_SEGS - 1 - j)


def _projection(x2, mod3, g_norm, w_main, w_tail, gains, gk, seq_len):
    m, d = x2.shape
    tm = 1024
    tiles_per_seq = seq_len // tm
    return pl.pallas_call(
        _proj_kernel,
        grid=(m // tm, N_SEGS),
        in_specs=[
            pl.BlockSpec((tm, d), lambda i, j: (i, 0)),
            pl.BlockSpec((1, 3, d), lambda i, j: (i // tiles_per_seq, 0, 0)),
            pl.BlockSpec((1, d), lambda i, j: (0, 0)),
            pl.BlockSpec((D_GROUP, d), lambda i, j: (_segment_of_step(i, j), 0)),
            pl.BlockSpec((2 * LANES, d), lambda i, j: (0, 0)),
            pl.BlockSpec((1, 1, LANES), lambda i, j: (_segment_of_step(i, j), 0, 0)),
            pl.BlockSpec((1, LANES), lambda i, j: (0, 0)),
        ],
        out_specs=[
            pl.BlockSpec((BLOCKS_PER_SEG, tm, LANES), lambda i, j: (_segment_of_step(i, j), i, 0)),
            pl.BlockSpec((tm, LANES), lambda i, j: (i, 0)),
            pl.BlockSpec((tm, LANES), lambda i, j: (i, 0)),
        ],
        out_shape=[
            jax.ShapeDtypeStruct((N_SEGS * BLOCKS_PER_SEG, m, LANES), BF16),
            jax.ShapeDtypeStruct((m, LANES), BF16),
            jax.ShapeDtypeStruct((m, LANES), F32),
        ],
        scratch_shapes=[pltpu.VMEM((tm, d), BF16)],
        compiler_params=pltpu.CompilerParams(
            dimension_semantics=("arbitrary", "arbitrary"), vmem_limit_bytes=VMEM_LIMIT_BYTES),
        name="in_proj",
    )(x2, mod3, g_norm, w_main, w_tail, gains, gk)


POS_TILE = MOBA_BLOCK
AUG_KLOC, AUG_KBLK, AUG_ONE_LOC, AUG_ONE_BLK, AUG_ONEHOT0 = 0, 1, 2, 3, 8


def _key_features(seq_len):
    pos = np.arange(seq_len)
    f = np.zeros((seq_len, LANES), np.float32)
    f[:, AUG_KLOC] = pos % POS_TILE
    f[:, AUG_KBLK] = pos // POS_TILE
    f[:, AUG_ONE_LOC] = 1.0
    f[:, AUG_ONE_BLK] = 1.0
    f[pos, AUG_ONEHOT0 + pos // POS_TILE] = 1.0
    return jnp.asarray(f, BF16)


def _alibi_query_features(slope, q_tile, tq):
    lane = lax.broadcasted_iota(jnp.int32, (tq, LANES), 1)
    row = lax.broadcasted_iota(jnp.int32, (tq, LANES), 0).astype(F32)
    f = jnp.where(lane == AUG_KLOC, slope, 0.0)
    f = jnp.where(lane == AUG_KBLK, slope * POS_TILE, f)
    f = jnp.where(lane == AUG_ONE_LOC, -slope * row, f)
    return jnp.where(lane == AUG_ONE_BLK, (-slope * POS_TILE) * q_tile.astype(F32), f)


FOLD_CHAINS = 4


def _fold_rows(x, op):
    r = x.reshape(x.shape[0] // 8, 8, x.shape[1])
    n_acc = min(FOLD_CHAINS, r.shape[0])
    accs = [r[k] for k in range(n_acc)]
    for g in range(n_acc, r.shape[0]):
        accs[g % n_acc] = op(accs[g % n_acc], r[g])
    while len(accs) > 1:
        accs = [op(accs[k], accs[k + 1]) for k in range(0, len(accs) - 1, 2)] + accs[len(accs) - len(accs) % 2:]
    return accs[0]


TILE_CHUNKS = (4, 2, 1)


def _for_tile_chunks(n_tiles, fn, chunks=TILE_CHUNKS):
    big = chunks[0]
    shift = big.bit_length() - 1

    def many(c, carry):
        fn(c * big, big)
        return carry

    n_big = lax.shift_right_logical(n_tiles, shift)
    lax.fori_loop(0, n_big, many, 0)
    done = n_big * big
    for size in chunks[1:]:
        has = (n_tiles & size) != 0

        @pl.when(has)
        def _(done=done, size=size):
            fn(done, size)

        done = done + (n_tiles & size)


def _softmax_pass_a(n_tiles, tile, score_rows, s_ref, mx_ref):
    tq = s_ref.shape[-1]

    def pass_a(first, count):
        for h in range(N_HEADS):
            s = score_rows(h, first, count) * LOG2E
            s_ref[h, pl.ds(first, count)] = s.reshape(count, tile, tq)
            mx_ref[h] = jnp.maximum(mx_ref[h], _fold_rows(s, jnp.maximum))

    _for_tile_chunks(n_tiles, pass_a)
    return [jnp.max(mx_ref[h], axis=0, keepdims=True) for h in range(N_HEADS)]


def _softmax_pass_b(n_tiles, tile, m, value_cols, s_ref, ls_ref, acc_ref):
    tq = acc_ref.shape[-1]
    ls_ref[...] = jnp.zeros(ls_ref.shape, F32)
    acc_ref[...] = jnp.zeros(acc_ref.shape, F32)

    def pass_b(first, count):
        for h in range(N_HEADS):
            p = jnp.exp2(s_ref[h, pl.ds(first, count)].reshape(count * tile, tq) - m[h])
            ls_ref[h] += _fold_rows(p, jnp.add)
            vt = jnp.concatenate([value_cols(h, first + u) for u in range(count)], axis=1)
            acc_ref[h] += jnp.dot(vt, p.astype(BF16), preferred_element_type=F32)

    _for_tile_chunks(n_tiles, pass_b)
    return [jnp.sum(ls_ref[h], axis=0, keepdims=True) for h in range(N_HEADS)]


def _gated_output(h, z_ref, l, acc_ref, o_ref):
    z = z_ref[h].astype(F32)
    o = (acc_ref[h] / l).T * (z * jax.nn.sigmoid(z))
    o_ref[:, h * HEAD_DIM:(h + 1) * HEAD_DIM] = o.astype(BF16)


def _transpose_values(v_ref, vt_ref, n_tiles, tile):
    for h in range(N_HEADS):
        for j in range(n_tiles):
            vt_ref[h, j] = v_ref[h, j * tile:(j + 1) * tile, :].astype(F32).T.astype(BF16)


DSA_TQ = 256
DSA_TK = POS_TILE
SEARCH_STEPS = 4
SEARCH_SLACK_STEPS = 9


def _dsa_kernel(q_ref, k_ref, v_ref, z_ref, qi_ref, kk_ref, wi_ref, kaug_ref, o_ref,
                vt_ref, qs_ref, sc_ref, ext_ref, t_ref, j_ref, qcat_ref, s_ref, mx_ref, ls_ref, acc_ref,
                *, top_k, seq_len):
    tq, tk = DSA_TQ, DSA_TK
    i = pl.program_id(1)
    t0 = i * tq
    nk = i + 1
    idx_scale = (IDX_DIM ** -0.5) * (IDX_HEADS ** -0.5)

    @pl.when(i == 0)
    def _():
        _transpose_values(v_ref, vt_ref, seq_len // tk, tk)

    lane = lax.broadcasted_iota(jnp.int32, (tq, LANES), 1)
    key_loc = lax.broadcasted_iota(jnp.int32, (tk, tq), 0).astype(F32)
    qry_loc = lax.broadcasted_iota(jnp.int32, (tk, tq), 1).astype(F32)

    for c in range(BLOCKS_PER_SEG):
        qp = qi_ref[c].astype(F32)
        qs_ref[2 * c] = jnp.where(lane < IDX_DIM, qp, 0.0).astype(BF16)
        qs_ref[2 * c + 1] = jnp.where(lane >= IDX_DIM, qp, 0.0).astype(BF16)
    w_t = (wi_ref[...] * idx_scale).T
    w_rows = [w_t[h:h + 1, :] for h in range(IDX_HEADS)]

    def score_tiles(first, count):
        for u in range(count):
            kt = first + u
            kk = kk_ref[pl.ds(pl.multiple_of(kt * tk, tk), tk), :]
            acc = jnp.zeros((tk, tq), F32)
            for h in range(IDX_HEADS):
                acc = acc + w_rows[h] * jnp.maximum(_dot_nt(kk, qs_ref[h]), 0.0)
            off = (kt * tk - t0).astype(F32)
            causal = (key_loc + off) <= qry_loc
            sc_ref[kt] = jnp.where(causal, acc, -jnp.inf)
            ext_ref[0] = jnp.minimum(ext_ref[0], _fold_rows(jnp.where(causal, acc, BIG), jnp.minimum))
            ext_ref[1] = jnp.maximum(ext_ref[1], _fold_rows(jnp.where(causal, acc, NEG_INF), jnp.maximum))

    ext_ref[0] = jnp.full((8, tq), BIG, F32)
    ext_ref[1] = jnp.full((8, tq), NEG_INF, F32)
    _for_tile_chunks(nk, score_tiles)

    t_ref[...] = jnp.full((1, tq), NEG_INF, F32)
    j_ref[...] = jnp.full((1, tq), -1.0, F32)

    def count(pred):
        def body(kt, acc):
            off = (kt * tk).astype(F32)
            return acc + _fold_rows(jnp.where(pred(sc_ref[kt], key_loc + off), 1.0, 0.0), jnp.add)
        return jnp.sum(lax.fori_loop(0, nk, body, jnp.zeros((8, tq), F32)), axis=0, keepdims=True)

    def write_mask(with_ties):
        def mask_tile(kt, carry):
            s = sc_ref[kt]
            thr = t_ref[...]
            sel = s > thr
            if with_ties:
                off = (kt * tk).astype(F32)
                sel = jnp.logical_or(sel, jnp.logical_and(s == thr, (key_loc + off) <= j_ref[...]))
            sc_ref[kt] = jnp.where(sel, 0.0, NEG_INF)
            return carry

        lax.fori_loop(0, nk, mask_tile, 0)

    @pl.when(t0 + tq > top_k)
    def _():
        kf = float(top_k)
        qpos = lax.broadcasted_iota(jnp.int32, (1, tq), 1) + t0
        done0 = jnp.where(qpos + 1 <= top_k, 1.0, 0.0)

        lo0 = jnp.min(ext_ref[0], axis=0, keepdims=True)
        hi0 = jnp.max(ext_ref[1], axis=0, keepdims=True)

        def step(x, degen, state):
            lo, hi, thr, done, tie = state
            c = count(lambda s, pos: s > x)
            active = done == 0.0
            live = jnp.logical_and(active, jnp.logical_not(degen))
            found = jnp.logical_and(live, c == kf)
            new_tie = jnp.logical_and(active, degen)
            thr = jnp.where(found, x, jnp.where(new_tie, hi, thr))
            tie = jnp.where(new_tie, 1.0, tie)
            done = jnp.where(jnp.logical_or(found, new_tie), 1.0, done)
            lo = jnp.where(jnp.logical_and(live, c > kf), x, lo)
            hi = jnp.where(jnp.logical_and(live, c < kf), x, hi)
            return lo, hi, thr, done, tie

        state0 = (lo0, hi0, jnp.full((1, tq), NEG_INF, F32), done0, jnp.zeros((1, tq), F32))

        def bisect(state):
            for _ in range(SEARCH_STEPS):
                lo, hi = state[0], state[1]
                mid = 0.5 * lo + 0.5 * hi
                degen = jnp.logical_or(mid <= lo, mid >= hi)
                state = step(mid, degen, state)
            return state

        unchecked_rounds = (int(np.log2(seq_len)) + SEARCH_SLACK_STEPS) // SEARCH_STEPS
        state1 = lax.fori_loop(0, unchecked_rounds, lambda _, st: bisect(st), state0)

        def flags(state):
            return jnp.max(1.0 - state[3]), jnp.max(state[4])

        def cond(carry):
            return carry[1] > 0.0

        def body(carry):
            state = bisect(carry[0])
            return (state,) + flags(state)

        (lo, hi, thr_loop, done, tie), _, any_tie = lax.while_loop(cond, body, (state1,) + flags(state1))
        t_ref[...] = thr_loop

        @pl.when(any_tie > 0.0)
        def _():
            reach_hi = count(lambda s, pos: s >= hi)
            thr = jnp.where(jnp.logical_and(tie > 0.0, reach_hi < kf), lo, thr_loop)
            t_ref[...] = thr
            need = kf - count(lambda s, pos: s > thr)
            n_bits = int(np.ceil(np.log2(seq_len + 1)))

            def jstep(_, carry):
                jlo, jhi = carry
                jmid = jnp.floor((jlo + jhi) * 0.5)
                c = count(lambda s, pos: jnp.logical_and(s == thr, pos <= jmid))
                ge = c >= need
                return jnp.where(ge, jlo, jmid), jnp.where(ge, jmid, jhi)

            _, jhi = lax.fori_loop(0, n_bits, jstep,
                                   (jnp.full((1, tq), -1.0, F32),
                                    jnp.full((1, tq), float(seq_len - 1), F32)))
            j_ref[...] = jnp.where(tie > 0.0, jhi, -1.0)
            write_mask(True)

        pl.when(any_tie <= 0.0)(functools.partial(write_mask, False))

    pl.when(t0 + tq <= top_k)(functools.partial(write_mask, False))

    for h in range(N_HEADS):
        feat = _alibi_query_features(_alibi_slope(h, N_HEADS), i, tq)
        qcat_ref[h] = jnp.concatenate([q_ref[h], feat.astype(BF16)], axis=1)

    def score_rows(h, first, count):
        ks = pl.ds(pl.multiple_of(first * tk, tk), count * tk)
        kcat = jnp.concatenate([k_ref[h, ks, :], kaug_ref[ks, :]], axis=1)
        return _dot_nt(kcat, qcat_ref[h]) + sc_ref[pl.ds(first, count)].reshape(count * tk, tq)

    mx_ref[...] = jnp.full(mx_ref.shape, NEG_INF, F32)
    value_cols = lambda h, kt: vt_ref[h, kt]
    m = _softmax_pass_a(nk, tk, score_rows, s_ref, mx_ref)
    lsum = _softmax_pass_b(nk, tk, m, value_cols, s_ref, ls_ref, acc_ref)
    for h in range(N_HEADS):
        _gated_output(h, z_ref, lsum[h], acc_ref, o_ref)


def _dsa_attention(p, kk, wi, kaug, bsz, seq_len):
    tq, tk = DSA_TQ, DSA_TK
    nq = seq_len // tq
    nkt = seq_len // tk
    m = bsz * seq_len
    top_k = min(DSA_TOPK_MAX, seq_len // 4)
    seg = lambda s: (lambda b, i: (s, b * nq + i, 0))
    seg_full = lambda s: (lambda b, i: (s, b, 0))
    kern = functools.partial(_dsa_kernel, top_k=top_k, seq_len=seq_len)
    return pl.pallas_call(
        kern,
        grid=(bsz, nq),
        in_specs=[
            pl.BlockSpec((BLOCKS_PER_SEG, tq, LANES), seg(SEG_QA)),
            pl.BlockSpec((BLOCKS_PER_SEG, seq_len, LANES), seg_full(SEG_KA)),
            pl.BlockSpec((BLOCKS_PER_SEG, seq_len, LANES), seg_full(SEG_VA)),
            pl.BlockSpec((BLOCKS_PER_SEG, tq, LANES), seg(SEG_ZA)),
            pl.BlockSpec((BLOCKS_PER_SEG, tq, LANES), seg(SEG_QIDX)),
            pl.BlockSpec((seq_len, LANES), lambda b, i: (b, 0)),
            pl.BlockSpec((tq, LANES), lambda b, i: (b * nq + i, 0)),
            pl.BlockSpec((seq_len, LANES), lambda b, i: (0, 0)),
        ],
        out_specs=pl.BlockSpec((tq, D_GROUP), lambda b, i: (b * nq + i, 0)),
        out_shape=jax.ShapeDtypeStruct((m, D_GROUP), BF16),
        scratch_shapes=[
            pltpu.VMEM((N_HEADS, nkt, HEAD_DIM, tk), BF16),
            pltpu.VMEM((IDX_HEADS, tq, LANES), BF16),
            pltpu.VMEM((nkt, tk, tq), F32),
            pltpu.VMEM((2, 8, tq), F32),
            pltpu.VMEM((1, tq), F32),
            pltpu.VMEM((1, tq), F32),
            pltpu.VMEM((N_HEADS, tq, 2 * LANES), BF16),
            pltpu.VMEM((N_HEADS, nkt, tk, tq), F32),
            pltpu.VMEM((N_HEADS, 8, tq), F32),
            pltpu.VMEM((N_HEADS, 8, tq), F32),
            pltpu.VMEM((N_HEADS, HEAD_DIM, tq), F32),
        ],
        compiler_params=pltpu.CompilerParams(
            dimension_semantics=("arbitrary", "arbitrary"), vmem_limit_bytes=VMEM_LIMIT_BYTES),
        name="dsa_attn",
    )(p, p, p, p, p, kk, wi, kaug)


def _moba_kernel(q_ref, k_ref, v_ref, z_ref, kaug_ref, o_ref,
                 kmean_ref, vt_ref, selb_ref, qcat_ref, s_ref, mx_ref, ls_ref, acc_ref, *, n_blocks):
    bs = MOBA_BLOCK
    n = pl.program_id(1)
    top_k = min(MOBA_TOPK_MAX, n_blocks - 1)

    @pl.when(n == 0)
    def _():
        kmean_ref[...] = jnp.zeros(kmean_ref.shape, F32)
        for h in range(N_HEADS):
            for j in range(n_blocks):
                kb = k_ref[h, j * bs:(j + 1) * bs, :].astype(F32)
                kmean_ref[h, j:j + 1, :] = jnp.mean(kb, axis=0, keepdims=True)
        _transpose_values(v_ref, vt_ref, n_blocks, bs)

    nb_rows = -(-n_blocks // 8) * 8
    blk = lax.broadcasted_iota(jnp.int32, (nb_rows, bs), 0)
    past = blk < n

    selb_ref[...] = jnp.zeros(selb_ref.shape, F32)
    for h in range(N_HEADS):
        q = q_ref[h]
        g = _dot_nt(kmean_ref[h].astype(BF16), q)[:nb_rows]
        for j in range(n_blocks):
            gj = g[j:j + 1, :]
            beats = jnp.logical_or(g > gj, jnp.logical_and(g == gj, blk < j))
            rank = jnp.sum(jnp.where(jnp.logical_and(beats, past), 1.0, 0.0), axis=0, keepdims=True)
            dropped = jnp.logical_and(rank >= float(top_k), j < n)
            selb_ref[AUG_ONEHOT0 + j:AUG_ONEHOT0 + j + 1, :] = jnp.where(dropped, NEG_INF, 0.0)
        feat = _alibi_query_features(_alibi_slope(h, N_HEADS), n, bs) + selb_ref[...].T
        qcat_ref[h] = jnp.concatenate([q, feat.astype(BF16)], axis=1)

    key_loc = lax.broadcasted_iota(jnp.int32, (bs, bs), 0)
    qry_loc = lax.broadcasted_iota(jnp.int32, (bs, bs), 1)
    causal = key_loc <= qry_loc

    def score_rows(h, first, count):
        ks = pl.ds(pl.multiple_of(first * bs, bs), count * bs)
        kcat = jnp.concatenate([k_ref[h, ks, :], kaug_ref[ks, :]], axis=1)
        return _dot_nt(kcat, qcat_ref[h])

    for h in range(N_HEADS):
        s = jnp.where(causal, score_rows(h, n, 1), NEG_INF) * LOG2E
        s_ref[h, n] = s
        mx_ref[h] = _fold_rows(s, jnp.maximum)
    value_cols = lambda h, j: vt_ref[h, j]
    m = _softmax_pass_a(n, bs, score_rows, s_ref, mx_ref)
    lsum = _softmax_pass_b(n + 1, bs, m, value_cols, s_ref, ls_ref, acc_ref)
    for h in range(N_HEADS):
        _gated_output(h, z_ref, lsum[h], acc_ref, o_ref)


def _moba_attention(p, kaug, bsz, seq_len):
    bs = MOBA_BLOCK
    nb = seq_len // bs
    m = bsz * seq_len
    seg = lambda s: (lambda b, i: (s, b * nb + i, 0))
    seg_full = lambda s: (lambda b, i: (s, b, 0))
    kern = functools.partial(_moba_kernel, n_blocks=nb)
    nb_pad = max(16, nb)
    assert AUG_ONEHOT0 + nb <= LANES
    return pl.pallas_call(
        kern,
        grid=(bsz, nb),
        in_specs=[
            pl.BlockSpec((BLOCKS_PER_SEG, bs, LANES), seg(SEG_QB)),
            pl.BlockSpec((BLOCKS_PER_SEG, seq_len, LANES), seg_full(SEG_KB)),
            pl.BlockSpec((BLOCKS_PER_SEG, seq_len, LANES), seg_full(SEG_VB)),
            pl.BlockSpec((BLOCKS_PER_SEG, bs, LANES), seg(SEG_ZB)),
            pl.BlockSpec((seq_len, LANES), lambda b, i: (0, 0)),
        ],
        out_specs=pl.BlockSpec((bs, D_GROUP), lambda b, i: (b * nb + i, 0)),
        out_shape=jax.ShapeDtypeStruct((m, D_GROUP), BF16),
        scratch_shapes=[
            pltpu.VMEM((N_HEADS, nb_pad, HEAD_DIM), F32),
            pltpu.VMEM((N_HEADS, nb, HEAD_DIM, bs), BF16),
            pltpu.VMEM((LANES, bs), F32),
            pltpu.VMEM((N_HEADS, bs, 2 * LANES), BF16),
            pltpu.VMEM((N_HEADS, nb, bs, bs), F32),
            pltpu.VMEM((N_HEADS, 8, bs), F32),
            pltpu.VMEM((N_HEADS, 8, bs), F32),
            pltpu.VMEM((N_HEADS, HEAD_DIM, bs), F32),
        ],
        compiler_params=pltpu.CompilerParams(
            dimension_semantics=("arbitrary", "arbitrary"), vmem_limit_bytes=VMEM_LIMIT_BYTES),
        name="moba_attn",
    )(p, p, p, p, kaug)


OUT_CHUNK = 512


def _out_kernel(ya_ref, yb_ref, wa_ref, wb_ref, x_ref, mod_ref, o_ref):
    ya = ya_ref[...]
    yb = yb_ref[...]
    for c in range(o_ref.shape[1] // OUT_CHUNK):
        cs = slice(c * OUT_CHUNK, (c + 1) * OUT_CHUNK)
        y = jnp.dot(ya, wa_ref[:, cs], preferred_element_type=F32)
        y = y + jnp.dot(yb, wb_ref[:, cs], preferred_element_type=F32)
        o_ref[:, cs] = x_ref[:, cs] + mod_ref[0, 2:3, cs] * y


def _out_projection(ya, yb, w_out_bf, x2, mod3, seq_len):
    m, d = x2.shape
    tm = 512
    tiles_per_seq = seq_len // tm
    return pl.pallas_call(
        _out_kernel,
        grid=(m // tm,),
        in_specs=[
            pl.BlockSpec((tm, D_GROUP), lambda i: (i, 0)),
            pl.BlockSpec((tm, D_GROUP), lambda i: (i, 0)),
            pl.BlockSpec((D_GROUP, d), lambda i: (0, 0)),
            pl.BlockSpec((D_GROUP, d), lambda i: (1, 0)),
            pl.BlockSpec((tm, d), lambda i: (i, 0)),
            pl.BlockSpec((1, 3, d), lambda i: (i // tiles_per_seq, 0, 0)),
        ],
        out_specs=pl.BlockSpec((tm, d), lambda i: (i, 0)),
        out_shape=jax.ShapeDtypeStruct((m, d), F32),
        compiler_params=pltpu.CompilerParams(
            dimension_semantics=("arbitrary",), vmem_limit_bytes=VMEM_LIMIT_BYTES),
        name="out_proj",
    )(ya, yb, w_out_bf, w_out_bf, x2, mod3)


RELAYOUT_ROWS = 512
RELAYOUT_HEAD = 128


def _relayout_kernel(a_ref, b_ref, mid_ref, main_ref, tail_ref, *, n_a_tiles, n_mid):
    r = pl.program_id(0)

    @pl.when(r < n_a_tiles)
    def _():
        main_ref[...] = a_ref[...].astype(BF16)

    @pl.when(r >= n_a_tiles)
    def _():
        main_ref[:RELAYOUT_ROWS - n_mid, :] = a_ref[n_mid:, :].astype(BF16)
        main_ref[RELAYOUT_ROWS - n_mid:, :] = b_ref[:n_mid, :].astype(BF16)

    kidx = mid_ref[:IDX_DIM, :].astype(BF16)
    tail_ref[0:IDX_DIM, :] = kidx
    tail_ref[IDX_DIM:2 * IDX_DIM, :] = kidx
    tail_ref[2 * IDX_DIM:2 * IDX_DIM + IDX_HEADS, :] = mid_ref[IDX_DIM:IDX_DIM + IDX_HEADS, :].astype(BF16)
    tail_ref[2 * IDX_DIM + IDX_HEADS:, :] = jnp.zeros(
        (tail_ref.shape[0] - 2 * IDX_DIM - IDX_HEADS, tail_ref.shape[1]), BF16)


def _relayout_weights(w_t):
    n_in, d = w_t.shape
    n_a = 4 * D_GROUP + IDX_HEADS * IDX_DIM
    n_mid = IDX_DIM + IDX_HEADS
    tr = RELAYOUT_ROWS
    assert n_in == n_a + n_mid + 4 * D_GROUP and n_a % tr == 0 and (4 * D_GROUP) % tr == 0
    assert n_mid % 16 == 0 and n_mid <= RELAYOUT_HEAD and tr % RELAYOUT_HEAD == 0 and n_a % n_mid == 0
    n_main = n_in - n_mid
    n_a_tiles = n_a // tr
    return pl.pallas_call(
        functools.partial(_relayout_kernel, n_a_tiles=n_a_tiles, n_mid=n_mid),
        grid=(n_main // tr,),
        in_specs=[
            pl.BlockSpec((tr, d), lambda r: (r, 0)),
            pl.BlockSpec((RELAYOUT_HEAD, d),
                         lambda r: ((jnp.maximum(r, n_a_tiles - 1) + 1) * (tr // RELAYOUT_HEAD), 0)),
            pl.BlockSpec((n_mid, d), lambda r: (n_a // n_mid, 0)),
        ],
        out_specs=[
            pl.BlockSpec((tr, d), lambda r: (r, 0)),
            pl.BlockSpec((2 * LANES, d), lambda r: (0, 0)),
        ],
        out_shape=[
            jax.ShapeDtypeStruct((n_main, d), BF16),
            jax.ShapeDtypeStruct((2 * LANES, d), BF16),
        ],
        compiler_params=pltpu.CompilerParams(
            dimension_semantics=("arbitrary",), vmem_limit_bytes=VMEM_LIMIT_BYTES),
        name="w_relayout",
    )(w_t, w_t, w_t)


def _layer(x, c, w_ada, b_ada, g_norm, w_in, q_norm_a, k_norm_a, k_norm_idx, q_norm_b, k_norm_b, w_out):
    bsz, seq_len, d = x.shape
    assert seq_len % MOBA_BLOCK == 0 and seq_len % 1024 == 0 and d % 1024 == 0
    x2 = x.reshape(bsz * seq_len, d)

    mod3 = _modulation(c, w_ada, b_ada).reshape(bsz, 3, d)

    w_main, w_tail = _relayout_weights(jnp.swapaxes(w_in, 0, 1))
    ones = jnp.ones((HEAD_DIM,), F32)
    sm_scale = HEAD_DIM ** -0.5
    gains = jnp.stack([q_norm_a * sm_scale, k_norm_a, ones, ones, ones,
                       q_norm_b * sm_scale, k_norm_b, ones, ones]).reshape(N_SEGS, 1, HEAD_DIM)
    for h in range(N_HEADS):
        slope = _alibi_slope(h, N_HEADS)
        assert float(np.float32(slope).astype(jnp.bfloat16)) == slope, "ALiBi slopes must be exact in bf16"
    assert seq_len // POS_TILE <= 256 and POS_TILE <= 256, "positions must split into bf16-exact parts"
    kaug = _key_features(seq_len)
    gk = jnp.concatenate([k_norm_idx, k_norm_idx]).reshape(1, LANES)

    p, kk, wi = _projection(x2, mod3, g_norm.reshape(1, d), w_main, w_tail, gains, gk, seq_len)
    ya = _dsa_attention(p, kk, wi, kaug, bsz, seq_len)
    yb = _moba_attention(p, kaug, bsz, seq_len)
    out = _out_projection(ya, yb, w_out.astype(BF16), x2, mod3, seq_len)
    return out.reshape(bsz, seq_len, d)


def kernel(x, c, w_ada, b_ada, g_norm, w_in, q_norm_a, k_norm_a, k_norm_idx, q_norm_b, k_norm_b, w_out):
    for i in range(w_ada.shape[0]):
        x = _layer(x, c, w_ada[i], b_ada[i], g_norm[i], w_in[i], q_norm_a[i], k_norm_a[i],
                   k_norm_idx[i], q_norm_b[i], k_norm_b[i], w_out[i])
    return x
```

```python
import functools

import jax
import jax.numpy as jnp
import numpy as np
from jax import lax
from jax.experimental import pallas as pl
from jax.experimental.pallas import tpu as pltpu

F32 = jnp.float32
BF16 = jnp.bfloat16

HEAD_DIM = 128
N_HEADS = 8
D_GROUP = N_HEADS * HEAD_DIM
IDX_HEADS = 16
IDX_DIM = 64
DSA_TOPK_MAX = 256
MOBA_BLOCK = 256
MOBA_TOPK_MAX = 3
RMS_EPS = 1e-6
NEG_INF = -1e30
BIG = 1e30
LOG2E = 1.4426950408889634

LANES = 128
VMEM_LIMIT_BYTES = 56 * 1024 * 1024

SEG_QA, SEG_KA, SEG_VA, SEG_ZA, SEG_QIDX, SEG_QB, SEG_KB, SEG_VB, SEG_ZB = range(9)
N_SEGS = 9
BLOCKS_PER_SEG = D_GROUP // LANES


def _alibi_slope(h, n):
    return float(2.0 ** (-8.0 * (h + 1) / n))


def _dot_nt(a, b):
    return lax.dot_general(a, b, (((1,), (1,)), ((), ())), preferred_element_type=F32)


def _mod_kernel(c_ref, w_ref, b_ref, o_ref):
    c = c_ref[...]
    s = c * jax.nn.sigmoid(c)
    o_ref[...] = jnp.dot(s, w_ref[...], preferred_element_type=F32) + b_ref[...]


def _modulation(c, w_ada, b_ada):
    bsz, d = c.shape
    n = w_ada.shape[1]
    tn = 1024
    return pl.pallas_call(
        _mod_kernel,
        grid=(n // tn,),
        in_specs=[
            pl.BlockSpec((bsz, d), lambda j: (0, 0)),
            pl.BlockSpec((d, tn), lambda j: (0, j)),
            pl.BlockSpec((1, tn), lambda j: (0, j)),
        ],
        out_specs=pl.BlockSpec((bsz, tn), lambda j: (0, j)),
        out_shape=jax.ShapeDtypeStruct((bsz, n), F32),
        compiler_params=pltpu.CompilerParams(
            dimension_semantics=("arbitrary",), vmem_limit_bytes=VMEM_LIMIT_BYTES),
        name="adaln_mod",
    )(c, w_ada, b_ada.reshape(1, n))


PROJ_CHUNK = 2 * LANES


def _proj_kernel(x_hbm, mod_ref, g_ref, w_ref, wt_ref, gain_ref, gk_ref,
                 p_ref, kk_ref, wi_ref, h_ref, xbuf_ref, xsem):
    i = pl.program_id(0)
    j = pl.program_id(1)
    tm = xbuf_ref.shape[1]

    def x_copy(tile, slot):
        rows = pl.ds(pl.multiple_of(tile * tm, tm), tm)
        return pltpu.make_async_copy(x_hbm.at[rows, :], xbuf_ref.at[slot], xsem.at[slot])

    @pl.when(j == 0)
    def _():
        slot = i % 2

        @pl.when(i == 0)
        def _():
            x_copy(0, 0).start()

        x_copy(i, slot).wait()

        @pl.when(i + 1 < pl.num_programs(0))
        def _():
            x_copy(i + 1, 1 - slot).start()

        x = xbuf_ref[slot]
        ms = jnp.mean(x * x, axis=-1, keepdims=True)
        shift = mod_ref[0, 0:1, :]
        gs = g_ref[...] * (1.0 + mod_ref[0, 1:2, :])
        hb = (x * lax.rsqrt(ms + RMS_EPS) * gs + shift).astype(BF16)
        h_ref[...] = hb
        t = _dot_nt(hb, wt_ref[...])
        tk = t[:, :LANES]
        kms = jnp.mean(tk * tk, axis=-1, keepdims=True)
        kk_ref[...] = (tk * lax.rsqrt(kms + RMS_EPS) * gk_ref[...]).astype(BF16)
        wi_ref[...] = t[:, LANES:]

    def segment(with_norm):
        g = gain_ref[0]
        h = h_ref[...]
        for cc in range(D_GROUP // PROJ_CHUNK):
            acc = _dot_nt(h, w_ref[cc * PROJ_CHUNK:(cc + 1) * PROJ_CHUNK, :])
            for c in range(PROJ_CHUNK // LANES):
                a = acc[:, c * LANES:(c + 1) * LANES]
                if with_norm:
                    ms = jnp.mean(a * a, axis=-1, keepdims=True)
                    a = a * lax.rsqrt(ms + RMS_EPS) * g
                p_ref[cc * (PROJ_CHUNK // LANES) + c] = a.astype(BF16)

    seg = _segment_of_step(pl.program_id(0), j)
    is_norm = (seg == SEG_QA) | (seg == SEG_KA) | (seg == SEG_QB) | (seg == SEG_KB)
    pl.when(is_norm)(functools.partial(segment, True))
    pl.when(jnp.logical_not(is_norm))(functools.partial(segment, False))


def _segment_of_step(i, j):
    return jnp.where(i % 2 == 0, j, N_SEGS - 1 - j)


def _projection(x2, mod3, g_norm, w_main, w_tail, gains, gk, seq_len):
    m, d = x2.shape
    tm = 1024
    tiles_per_seq = seq_len // tm
    return pl.pallas_call(
        _proj_kernel,
        grid=(m // tm, N_SEGS),
        in_specs=[
            pl.BlockSpec(memory_space=pl.ANY),
            pl.BlockSpec((1, 3, d), lambda i, j: (i // tiles_per_seq, 0, 0)),
            pl.BlockSpec((1, d), lambda i, j: (0, 0)),
            pl.BlockSpec((D_GROUP, d), lambda i, j: (_segment_of_step(i, j), 0)),
            pl.BlockSpec((2 * LANES, d), lambda i, j: (0, 0)),
            pl.BlockSpec((1, 1, LANES), lambda i, j: (_segment_of_step(i, j), 0, 0)),
            pl.BlockSpec((1, LANES), lambda i, j: (0, 0)),
        ],
        out_specs=[
            pl.BlockSpec((BLOCKS_PER_SEG, tm, LANES), lambda i, j: (_segment_of_step(i, j), i, 0)),
            pl.BlockSpec((tm, LANES), lambda i, j: (i, 0)),
            pl.BlockSpec((tm, LANES), lambda i, j: (i, 0)),
        ],
        out_shape=[
            jax.ShapeDtypeStruct((N_SEGS * BLOCKS_PER_SEG, m, LANES), BF16),
            jax.ShapeDtypeStruct((m, LANES), BF16),
            jax.ShapeDtypeStruct((m, LANES), F32),
        ],
        scratch_shapes=[
            pltpu.VMEM((tm, d), BF16),
            pltpu.VMEM((2, tm, d), F32),
            pltpu.SemaphoreType.DMA((2,)),
        ],
        compiler_params=pltpu.CompilerParams(
            dimension_semantics=("arbitrary", "arbitrary"), vmem_limit_bytes=VMEM_LIMIT_BYTES),
        name="in_proj",
    )(x2, mod3, g_norm, w_main, w_tail, gains, gk)


POS_TILE = MOBA_BLOCK
AUG_KLOC, AUG_KBLK, AUG_ONE_LOC, AUG_ONE_BLK, AUG_ONEHOT0 = 0, 1, 2, 3, 8


def _key_features(seq_len):
    pos = np.arange(seq_len)
    f = np.zeros((seq_len, LANES), np.float32)
    f[:, AUG_KLOC] = pos % POS_TILE
    f[:, AUG_KBLK] = pos // POS_TILE
    f[:, AUG_ONE_LOC] = 1.0
    f[:, AUG_ONE_BLK] = 1.0
    f[pos, AUG_ONEHOT0 + pos // POS_TILE] = 1.0
    return jnp.asarray(f, BF16)


def _alibi_query_features(slope, q_tile, tq):
    lane = lax.broadcasted_iota(jnp.int32, (tq, LANES), 1)
    row = lax.broadcasted_iota(jnp.int32, (tq, LANES), 0).astype(F32)
    f = jnp.where(lane == AUG_KLOC, slope, 0.0)
    f = jnp.where(lane == AUG_KBLK, slope * POS_TILE, f)
    f = jnp.where(lane == AUG_ONE_LOC, -slope * row, f)
    return jnp.where(lane == AUG_ONE_BLK, (-slope * POS_TILE) * q_tile.astype(F32), f)


FOLD_CHAINS = 4


def _fold_rows(x, op):
    r = x.reshape(x.shape[0] // 8, 8, x.shape[1])
    n_acc = min(FOLD_CHAINS, r.shape[0])
    accs = [r[k] for k in range(n_acc)]
    for g in range(n_acc, r.shape[0]):
        accs[g % n_acc] = op(accs[g % n_acc], r[g])
    while len(accs) > 1:
        accs = [op(accs[k], accs[k + 1]) for k in range(0, len(accs) - 1, 2)] + accs[len(accs) - len(accs) % 2:]
    return accs[0]


TILE_CHUNKS = (4, 2, 1)


def _for_tile_chunks(n_tiles, fn, chunks=TILE_CHUNKS):
    big = chunks[0]
    shift = big.bit_length() - 1

    def many(c, carry):
        fn(c * big, big)
        return carry

    n_big = lax.shift_right_logical(n_tiles, shift)
    lax.fori_loop(0, n_big, many, 0)
    done = n_big * big
    for size in chunks[1:]:
        has = (n_tiles & size) != 0

        @pl.when(has)
        def _(done=done, size=size):
            fn(done, size)

        done = done + (n_tiles & size)


def _softmax_pass_a(n_tiles, tile, score_rows, s_ref, mx_ref):
    tq = s_ref.shape[-1]

    def pass_a(first, count):
        for h in range(N_HEADS):
            s = score_rows(h, first, count) * LOG2E
            s_ref[h, pl.ds(first, count)] = s.reshape(count, tile, tq)
            mx_ref[h] = jnp.maximum(mx_ref[h], _fold_rows(s, jnp.maximum))

    _for_tile_chunks(n_tiles, pass_a)
    return [jnp.max(mx_ref[h], axis=0, keepdims=True) for h in range(N_HEADS)]


def _softmax_pass_b(n_tiles, tile, m, value_cols, s_ref, ls_ref, acc_ref):
    tq = acc_ref.shape[-1]
    ls_ref[...] = jnp.zeros(ls_ref.shape, F32)
    acc_ref[...] = jnp.zeros(acc_ref.shape, F32)

    def pass_b(first, count):
        for h in range(N_HEADS):
            p = jnp.exp2(s_ref[h, pl.ds(first, count)].reshape(count * tile, tq) - m[h])
            ls_ref[h] += _fold_rows(p, jnp.add)
            vt = jnp.concatenate([value_cols(h, first + u) for u in range(count)], axis=1)
            acc_ref[h] += jnp.dot(vt, p.astype(BF16), preferred_element_type=F32)

    _for_tile_chunks(n_tiles, pass_b)
    return [jnp.sum(ls_ref[h], axis=0, keepdims=True) for h in range(N_HEADS)]


def _gated_output(h, z_ref, l, acc_ref, o_ref):
    z = z_ref[h].astype(F32)
    o = (acc_ref[h] / l).T * (z * jax.nn.sigmoid(z))
    o_ref[:, h * HEAD_DIM:(h + 1) * HEAD_DIM] = o.astype(BF16)


def _transpose_values(v_ref, vt_ref, n_tiles, tile):
    for h in range(N_HEADS):
        for j in range(n_tiles):
            vt_ref[h, j] = v_ref[h, j * tile:(j + 1) * tile, :].astype(F32).T.astype(BF16)


DSA_TQ = 256
DSA_TK = POS_TILE
SEARCH_STEPS = 4
SEARCH_SLACK_STEPS = 9


def _dsa_kernel(q_ref, k_ref, v_ref, z_ref, qi_ref, kk_ref, wi_ref, kaug_ref, o_ref,
                vt_ref, qs_ref, sc_ref, ext_ref, t_ref, j_ref, qcat_ref, s_ref, mx_ref, ls_ref, acc_ref,
                *, top_k, seq_len):
    tq, tk = DSA_TQ, DSA_TK
    i = pl.program_id(1)
    t0 = i * tq
    nk = i + 1
    idx_scale = (IDX_DIM ** -0.5) * (IDX_HEADS ** -0.5)

    @pl.when(i == 0)
    def _():
        _transpose_values(v_ref, vt_ref, seq_len // tk, tk)

    lane = lax.broadcasted_iota(jnp.int32, (tq, LANES), 1)
    key_loc = lax.broadcasted_iota(jnp.int32, (tk, tq), 0).astype(F32)
    qry_loc = lax.broadcasted_iota(jnp.int32, (tk, tq), 1).astype(F32)

    for c in range(BLOCKS_PER_SEG):
        qp = qi_ref[c].astype(F32)
        qs_ref[2 * c] = jnp.where(lane < IDX_DIM, qp, 0.0).astype(BF16)
        qs_ref[2 * c + 1] = jnp.where(lane >= IDX_DIM, qp, 0.0).astype(BF16)
    w_t = (wi_ref[...] * idx_scale).T
    w_rows = [w_t[h:h + 1, :] for h in range(IDX_HEADS)]

    def score_tiles(first, count):
        for u in range(count):
            kt = first + u
            kk = kk_ref[pl.ds(pl.multiple_of(kt * tk, tk), tk), :]
            acc = jnp.zeros((tk, tq), F32)
            for h in range(IDX_HEADS):
                acc = acc + w_rows[h] * jnp.maximum(_dot_nt(kk, qs_ref[h]), 0.0)
            off = (kt * tk - t0).astype(F32)
            causal = (key_loc + off) <= qry_loc
            sc_ref[kt] = jnp.where(causal, acc, -jnp.inf)
            ext_ref[0] = jnp.minimum(ext_ref[0], _fold_rows(jnp.where(causal, acc, BIG), jnp.minimum))
            ext_ref[1] = jnp.maximum(ext_ref[1], _fold_rows(jnp.where(causal, acc, NEG_INF), jnp.maximum))

    ext_ref[0] = jnp.full((8, tq), BIG, F32)
    ext_ref[1] = jnp.full((8, tq), NEG_INF, F32)
    _for_tile_chunks(nk, score_tiles)

    t_ref[...] = jnp.full((1, tq), NEG_INF, F32)
    j_ref[...] = jnp.full((1, tq), -1.0, F32)

    def count(pred):
        def body(kt, acc):
            off = (kt * tk).astype(F32)
            return acc + _fold_rows(jnp.where(pred(sc_ref[kt], key_loc + off), 1.0, 0.0), jnp.add)
        return jnp.sum(lax.fori_loop(0, nk, body, jnp.zeros((8, tq), F32)), axis=0, keepdims=True)

    def write_mask(with_ties):
        def mask_tile(kt, carry):
            s = sc_ref[kt]
            thr = t_ref[...]
            sel = s > thr
            if with_ties:
                off = (kt * tk).astype(F32)
                sel = jnp.logical_or(sel, jnp.logical_and(s == thr, (key_loc + off) <= j_ref[...]))
            sc_ref[kt] = jnp.where(sel, 0.0, NEG_INF)
            return carry

        lax.fori_loop(0, nk, mask_tile, 0)

    @pl.when(t0 + tq > top_k)
    def _():
        kf = float(top_k)
        qpos = lax.broadcasted_iota(jnp.int32, (1, tq), 1) + t0
        done0 = jnp.where(qpos + 1 <= top_k, 1.0, 0.0)

        lo0 = jnp.min(ext_ref[0], axis=0, keepdims=True)
        hi0 = jnp.max(ext_ref[1], axis=0, keepdims=True)

        def step(x, degen, state):
            lo, hi, thr, done, tie = state
            c = count(lambda s, pos: s > x)
            active = done == 0.0
            live = jnp.logical_and(active, jnp.logical_not(degen))
            found = jnp.logical_and(live, c == kf)
            new_tie = jnp.logical_and(active, degen)
            thr = jnp.where(found, x, jnp.where(new_tie, hi, thr))
            tie = jnp.where(new_tie, 1.0, tie)
            done = jnp.where(jnp.logical_or(found, new_tie), 1.0, done)
            lo = jnp.where(jnp.logical_and(live, c > kf), x, lo)
            hi = jnp.where(jnp.logical_and(live, c < kf), x, hi)
            return lo, hi, thr, done, tie

        state0 = (lo0, hi0, jnp.full((1, tq), NEG_INF, F32), done0, jnp.zeros((1, tq), F32))

        def bisect(state):
            for _ in range(SEARCH_STEPS):
                lo, hi = state[0], state[1]
                mid = 0.5 * lo + 0.5 * hi
                degen = jnp.logical_or(mid <= lo, mid >= hi)
                state = step(mid, degen, state)
            return state

        unchecked_rounds = (int(np.log2(seq_len)) + SEARCH_SLACK_STEPS) // SEARCH_STEPS
        state1 = lax.fori_loop(0, unchecked_rounds, lambda _, st: bisect(st), state0)

        def flags(state):
            return jnp.max(1.0 - state[3]), jnp.max(state[4])

        def cond(carry):
            return carry[1] > 0.0

        def body(carry):
            state = bisect(carry[0])
            return (state,) + flags(state)

        (lo, hi, thr_loop, done, tie), _, any_tie = lax.while_loop(cond, body, (state1,) + flags(state1))
        t_ref[...] = thr_loop

        @pl.when(any_tie > 0.0)
        def _():
            reach_hi = count(lambda s, pos: s >= hi)
            thr = jnp.where(jnp.logical_and(tie > 0.0, reach_hi < kf), lo, thr_loop)
            t_ref[...] = thr
            need = kf - count(lambda s, pos: s > thr)
            n_bits = int(np.ceil(np.log2(seq_len + 1)))

            def jstep(_, carry):
                jlo, jhi = carry
                jmid = jnp.floor((jlo + jhi) * 0.5)
                c = count(lambda s, pos: jnp.logical_and(s == thr, pos <= jmid))
                ge = c >= need
                return jnp.where(ge, jlo, jmid), jnp.where(ge, jmid, jhi)

            _, jhi = lax.fori_loop(0, n_bits, jstep,
                                   (jnp.full((1, tq), -1.0, F32),
                                    jnp.full((1, tq), float(seq_len - 1), F32)))
            j_ref[...] = jnp.where(tie > 0.0, jhi, -1.0)
            write_mask(True)

        pl.when(any_tie <= 0.0)(functools.partial(write_mask, False))

    pl.when(t0 + tq <= top_k)(functools.partial(write_mask, False))

    for h in range(N_HEADS):
        feat = _alibi_query_features(_alibi_slope(h, N_HEADS), i, tq)
        qcat_ref[h] = jnp.concatenate([q_ref[h], feat.astype(BF16)], axis=1)

    def score_rows(h, first, count):
        ks = pl.ds(pl.multiple_of(first * tk, tk), count * tk)
        kcat = jnp.concatenate([k_ref[h, ks, :], kaug_ref[ks, :]], axis=1)
        return _dot_nt(kcat, qcat_ref[h]) + sc_ref[pl.ds(first, count)].reshape(count * tk, tq)

    mx_ref[...] = jnp.full(mx_ref.shape, NEG_INF, F32)
    value_cols = lambda h, kt: vt_ref[h, kt]
    m = _softmax_pass_a(nk, tk, score_rows, s_ref, mx_ref)
    lsum = _softmax_pass_b(nk, tk, m, value_cols, s_ref, ls_ref, acc_ref)
    for h in range(N_HEADS):
        _gated_output(h, z_ref, lsum[h], acc_ref, o_ref)


def _dsa_attention(p, kk, wi, kaug, bsz, seq_len):
    tq, tk = DSA_TQ, DSA_TK
    nq = seq_len // tq
    nkt = seq_len // tk
    m = bsz * seq_len
    top_k = min(DSA_TOPK_MAX, seq_len // 4)
    seg = lambda s: (lambda b, i: (s, b * nq + i, 0))
    seg_full = lambda s: (lambda b, i: (s, b, 0))
    kern = functools.partial(_dsa_kernel, top_k=top_k, seq_len=seq_len)
    return pl.pallas_call(
        kern,
        grid=(bsz, nq),
        in_specs=[
            pl.BlockSpec((BLOCKS_PER_SEG, tq, LANES), seg(SEG_QA)),
            pl.BlockSpec((BLOCKS_PER_SEG, seq_len, LANES), seg_full(SEG_KA)),
            pl.BlockSpec((BLOCKS_PER_SEG, seq_len, LANES), seg_full(SEG_VA)),
            pl.BlockSpec((BLOCKS_PER_SEG, tq, LANES), seg(SEG_ZA)),
            pl.BlockSpec((BLOCKS_PER_SEG, tq, LANES), seg(SEG_QIDX)),
            pl.BlockSpec((seq_len, LANES), lambda b, i: (b, 0)),
            pl.BlockSpec((tq, LANES), lambda b, i: (b * nq + i, 0)),
            pl.BlockSpec((seq_len, LANES), lambda b, i: (0, 0)),
        ],
        out_specs=pl.BlockSpec((tq, D_GROUP), lambda b, i: (b * nq + i, 0)),
        out_shape=jax.ShapeDtypeStruct((m, D_GROUP), BF16),
        scratch_shapes=[
            pltpu.VMEM((N_HEADS, nkt, HEAD_DIM, tk), BF16),
            pltpu.VMEM((IDX_HEADS, tq, LANES), BF16),
            pltpu.VMEM((nkt, tk, tq), F32),
            pltpu.VMEM((2, 8, tq), F32),
            pltpu.VMEM((1, tq), F32),
            pltpu.VMEM((1, tq), F32),
            pltpu.VMEM((N_HEADS, tq, 2 * LANES), BF16),
            pltpu.VMEM((N_HEADS, nkt, tk, tq), F32),
            pltpu.VMEM((N_HEADS, 8, tq), F32),
            pltpu.VMEM((N_HEADS, 8, tq), F32),
            pltpu.VMEM((N_HEADS, HEAD_DIM, tq), F32),
        ],
        compiler_params=pltpu.CompilerParams(
            dimension_semantics=("arbitrary", "arbitrary"), vmem_limit_bytes=VMEM_LIMIT_BYTES),
        name="dsa_attn",
    )(p, p, p, p, p, kk, wi, kaug)


def _moba_kernel(q_ref, k_ref, v_ref, z_ref, kaug_ref, o_ref,
                 kmean_ref, vt_ref, selb_ref, qcat_ref, s_ref, mx_ref, ls_ref, acc_ref, *, n_blocks):
    bs = MOBA_BLOCK
    n = pl.program_id(1)
    top_k = min(MOBA_TOPK_MAX, n_blocks - 1)

    @pl.when(n == 0)
    def _():
        kmean_ref[...] = jnp.zeros(kmean_ref.shape, F32)
        for h in range(N_HEADS):
            for j in range(n_blocks):
                kb = k_ref[h, j * bs:(j + 1) * bs, :].astype(F32)
                kmean_ref[h, j:j + 1, :] = jnp.mean(kb, axis=0, keepdims=True)
        _transpose_values(v_ref, vt_ref, n_blocks, bs)

    nb_rows = -(-n_blocks // 8) * 8
    blk = lax.broadcasted_iota(jnp.int32, (nb_rows, bs), 0)
    past = blk < n

    selb_ref[...] = jnp.zeros(selb_ref.shape, F32)
    for h in range(N_HEADS):
        q = q_ref[h]
        g = _dot_nt(kmean_ref[h].astype(BF16), q)[:nb_rows]
        for j in range(n_blocks):
            gj = g[j:j + 1, :]
            beats = jnp.logical_or(g > gj, jnp.logical_and(g == gj, blk < j))
            rank = jnp.sum(jnp.where(jnp.logical_and(beats, past), 1.0, 0.0), axis=0, keepdims=True)
            dropped = jnp.logical_and(rank >= float(top_k), j < n)
            selb_ref[AUG_ONEHOT0 + j:AUG_ONEHOT0 + j + 1, :] = jnp.where(dropped, NEG_INF, 0.0)
        feat = _alibi_query_features(_alibi_slope(h, N_HEADS), n, bs) + selb_ref[...].T
        qcat_ref[h] = jnp.concatenate([q, feat.astype(BF16)], axis=1)

    key_loc = lax.broadcasted_iota(jnp.int32, (bs, bs), 0)
    qry_loc = lax.broadcasted_iota(jnp.int32, (bs, bs), 1)
    causal = key_loc <= qry_loc

    def score_rows(h, first, count):
        ks = pl.ds(pl.multiple_of(first * bs, bs), count * bs)
        kcat = jnp.concatenate([k_ref[h, ks, :], kaug_ref[ks, :]], axis=1)
        return _dot_nt(kcat, qcat_ref[h])

    for h in range(N_HEADS):
        s = jnp.where(causal, score_rows(h, n, 1), NEG_INF) * LOG2E
        s_ref[h, n] = s
        mx_ref[h] = _fold_rows(s, jnp.maximum)
    value_cols = lambda h, j: vt_ref[h, j]
    m = _softmax_pass_a(n, bs, score_rows, s_ref, mx_ref)
    lsum = _softmax_pass_b(n + 1, bs, m, value_cols, s_ref, ls_ref, acc_ref)
    for h in range(N_HEADS):
        _gated_output(h, z_ref, lsum[h], acc_ref, o_ref)


def _moba_attention(p, kaug, bsz, seq_len):
    bs = MOBA_BLOCK
    nb = seq_len // bs
    m = bsz * seq_len
    seg = lambda s: (lambda b, i: (s, b * nb + i, 0))
    seg_full = lambda s: (lambda b, i: (s, b, 0))
    kern = functools.partial(_moba_kernel, n_blocks=nb)
    nb_pad = max(16, nb)
    assert AUG_ONEHOT0 + nb <= LANES
    return pl.pallas_call(
        kern,
        grid=(bsz, nb),
        in_specs=[
            pl.BlockSpec((BLOCKS_PER_SEG, bs, LANES), seg(SEG_QB)),
            pl.BlockSpec((BLOCKS_PER_SEG, seq_len, LANES), seg_full(SEG_KB)),
            pl.BlockSpec((BLOCKS_PER_SEG, seq_len, LANES), seg_full(SEG_VB)),
            pl.BlockSpec((BLOCKS_PER_SEG, bs, LANES), seg(SEG_ZB)),
            pl.BlockSpec((seq_len, LANES), lambda b, i: (0, 0)),
        ],
        out_specs=pl.BlockSpec((bs, D_GROUP), lambda b, i: (b * nb + i, 0)),
        out_shape=jax.ShapeDtypeStruct((m, D_GROUP), BF16),
        scratch_shapes=[
            pltpu.VMEM((N_HEADS, nb_pad, HEAD_DIM), F32),
            pltpu.VMEM((N_HEADS, nb, HEAD_DIM, bs), BF16),
            pltpu.VMEM((LANES, bs), F32),
            pltpu.VMEM((N_HEADS, bs, 2 * LANES), BF16),
            pltpu.VMEM((N_HEADS, nb, bs, bs), F32),
            pltpu.VMEM((N_HEADS, 8, bs), F32),
            pltpu.VMEM((N_HEADS, 8, bs), F32),
            pltpu.VMEM((N_HEADS, HEAD_DIM, bs), F32),
        ],
        compiler_params=pltpu.CompilerParams(
            dimension_semantics=("arbitrary", "arbitrary"), vmem_limit_bytes=VMEM_LIMIT_BYTES),
        name="moba_attn",
    )(p, p, p, p, kaug)


OUT_CHUNK = 512


def _out_kernel(ya_ref, yb_ref, wa_ref, wb_ref, x_ref, mod_ref, o_ref):
    ya = ya_ref[...]
    yb = yb_ref[...]
    for c in range(o_ref.shape[1] // OUT_CHUNK):
        cs = slice(c * OUT_CHUNK, (c + 1) * OUT_CHUNK)
        y = jnp.dot(ya, wa_ref[:, cs], preferred_element_type=F32)
        y = y + jnp.dot(yb, wb_ref[:, cs], preferred_element_type=F32)
        o_ref[:, cs] = x_ref[:, cs] + mod_ref[0, 2:3, cs] * y


def _out_projection(ya, yb, w_out_bf, x2, mod3, seq_len):
    m, d = x2.shape
    tm = 512
    tiles_per_seq = seq_len // tm
    return pl.pallas_call(
        _out_kernel,
        grid=(m // tm,),
        in_specs=[
            pl.BlockSpec((tm, D_GROUP), lambda i: (i, 0)),
            pl.BlockSpec((tm, D_GROUP), lambda i: (i, 0)),
            pl.BlockSpec((D_GROUP, d), lambda i: (0, 0)),
            pl.BlockSpec((D_GROUP, d), lambda i: (1, 0)),
            pl.BlockSpec((tm, d), lambda i: (i, 0)),
            pl.BlockSpec((1, 3, d), lambda i: (i // tiles_per_seq, 0, 0)),
        ],
        out_specs=pl.BlockSpec((tm, d), lambda i: (i, 0)),
        out_shape=jax.ShapeDtypeStruct((m, d), F32),
        compiler_params=pltpu.CompilerParams(
            dimension_semantics=("arbitrary",), vmem_limit_bytes=VMEM_LIMIT_BYTES),
        name="out_proj",
    )(ya, yb, w_out_bf, w_out_bf, x2, mod3)


RELAYOUT_ROWS = 512
RELAYOUT_HEAD = 128


def _relayout_kernel(a_ref, b_ref, mid_ref, main_ref, tail_ref, *, n_a_tiles, n_mid):
    r = pl.program_id(0)

    @pl.when(r < n_a_tiles)
    def _():
        main_ref[...] = a_ref[...].astype(BF16)

    @pl.when(r >= n_a_tiles)
    def _():
        main_ref[:RELAYOUT_ROWS - n_mid, :] = a_ref[n_mid:, :].astype(BF16)
        main_ref[RELAYOUT_ROWS - n_mid:, :] = b_ref[:n_mid, :].astype(BF16)

    kidx = mid_ref[:IDX_DIM, :].astype(BF16)
    tail_ref[0:IDX_DIM, :] = kidx
    tail_ref[IDX_DIM:2 * IDX_DIM, :] = kidx
    tail_ref[2 * IDX_DIM:2 * IDX_DIM + IDX_HEADS, :] = mid_ref[IDX_DIM:IDX_DIM + IDX_HEADS, :].astype(BF16)
    tail_ref[2 * IDX_DIM + IDX_HEADS:, :] = jnp.zeros(
        (tail_ref.shape[0] - 2 * IDX_DIM - IDX_HEADS, tail_ref.shape[1]), BF16)


def _relayout_weights(w_t):
    n_in, d = w_t.shape
    n_a = 4 * D_GROUP + IDX_HEADS * IDX_DIM
    n_mid = IDX_DIM + IDX_HEADS
    tr = RELAYOUT_ROWS
    assert n_in == n_a + n_mid + 4 * D_GROUP and n_a % tr == 0 and (4 * D_GROUP) % tr == 0
    assert n_mid % 16 == 0 and n_mid <= RELAYOUT_HEAD and tr % RELAYOUT_HEAD == 0 and n_a % n_mid == 0
    n_main = n_in - n_mid
    n_a_tiles = n_a // tr
    return pl.pallas_call(
        functools.partial(_relayout_kernel, n_a_tiles=n_a_tiles, n_mid=n_mid),
        grid=(n_main // tr,),
        in_specs=[
            pl.BlockSpec((tr, d), lambda r: (r, 0)),
            pl.BlockSpec((RELAYOUT_HEAD, d),
                         lambda r: ((jnp.maximum(r, n_a_tiles - 1) + 1) * (tr // RELAYOUT_HEAD), 0)),
            pl.BlockSpec((n_mid, d), lambda r: (n_a // n_mid, 0)),
        ],
        out_specs=[
            pl.BlockSpec((tr, d), lambda r: (r, 0)),
            pl.BlockSpec((2 * LANES, d), lambda r: (0, 0)),
        ],
        out_shape=[
            jax.ShapeDtypeStruct((n_main, d), BF16),
            jax.ShapeDtypeStruct((2 * LANES, d), BF16),
        ],
        compiler_params=pltpu.CompilerParams(
            dimension_semantics=("arbitrary",), vmem_limit_bytes=VMEM_LIMIT_BYTES),
        name="w_relayout",
    )(w_t, w_t, w_t)


def _layer(x, c, w_ada, b_ada, g_norm, w_in, q_norm_a, k_norm_a, k_norm_idx, q_norm_b, k_norm_b, w_out):
    bsz, seq_len, d = x.shape
    assert seq_len % MOBA_BLOCK == 0 and seq_len % 1024 == 0 and d % 1024 == 0
    x2 = x.reshape(bsz * seq_len, d)

    mod3 = _modulation(c, w_ada, b_ada).reshape(bsz, 3, d)

    w_main, w_tail = _relayout_weights(jnp.swapaxes(w_in, 0, 1))
    ones = jnp.ones((HEAD_DIM,), F32)
    sm_scale = HEAD_DIM ** -0.5
    gains = jnp.stack([q_norm_a * sm_scale, k_norm_a, ones, ones, ones,
                       q_norm_b * sm_scale, k_norm_b, ones, ones]).reshape(N_SEGS, 1, HEAD_DIM)
    for h in range(N_HEADS):
        slope = _alibi_slope(h, N_HEADS)
        assert float(np.float32(slope).astype(jnp.bfloat16)) == slope, "ALiBi slopes must be exact in bf16"
    assert seq_len // POS_TILE <= 256 and POS_TILE <= 256, "positions must split into bf16-exact parts"
    kaug = _key_features(seq_len)
    gk = jnp.concatenate([k_norm_idx, k_norm_idx]).reshape(1, LANES)

    p, kk, wi = _projection(x2, mod3, g_norm.reshape(1, d), w_main, w_tail, gains, gk, seq_len)
    ya = _dsa_attention(p, kk, wi, kaug, bsz, seq_len)
    yb = _moba_attention(p, kaug, bsz, seq_len)
    out = _out_projection(ya, yb, w_out.astype(BF16), x2, mod3, seq_len)
    return out.reshape(bsz, seq_len, d)


def kernel(x, c, w_ada, b_ada, g_norm, w_in, q_norm_a, k_norm_a, k_norm_idx, q_norm_b, k_norm_b, w_out):
    for i in range(w_ada.shape[0]):
        x = _layer(x, c, w_ada[i], b_ada[i], g_norm[i], w_in[i], q_norm_a[i], k_norm_a[i],
                   k_norm_idx[i], q_norm_b[i], k_norm_b[i], w_out[i])
    return x
```

```python
import functools

import jax
import jax.numpy as jnp
import numpy as np
from jax import lax
from jax.experimental import pallas as pl
from jax.experimental.pallas import tpu as pltpu

F32 = jnp.float32
BF16 = jnp.bfloat16

HEAD_DIM = 128
N_HEADS = 8
D_GROUP = N_HEADS * HEAD_DIM
IDX_HEADS = 16
IDX_DIM = 64
DSA_TOPK_MAX = 256
MOBA_BLOCK = 256
MOBA_TOPK_MAX = 3
RMS_EPS = 1e-6
NEG_INF = -1e30
BIG = 1e30
LOG2E = 1.4426950408889634

LANES = 128
VMEM_LIMIT_BYTES = 56 * 1024 * 1024

SEG_QA, SEG_KA, SEG_VA, SEG_ZA, SEG_QIDX, SEG_QB, SEG_KB, SEG_VB, SEG_ZB = range(9)
N_SEGS = 9
BLOCKS_PER_SEG = D_GROUP // LANES


def _alibi_slope(h, n):
    return float(2.0 ** (-8.0 * (h + 1) / n))


def _dot_nt(a, b):
    return lax.dot_general(a, b, (((1,), (1,)), ((), ())), preferred_element_type=F32)


def _mod_kernel(c_ref, w_ref, b_ref, o_ref):
    c = c_ref[...]
    s = c * jax.nn.sigmoid(c)
    o_ref[...] = jnp.dot(s, w_ref[...], preferred_element_type=F32) + b_ref[...]


def _modulation(c, w_ada, b_ada):
    bsz, d = c.shape
    n = w_ada.shape[1]
    tn = 1024
    return pl.pallas_call(
        _mod_kernel,
        grid=(n // tn,),
        in_specs=[
            pl.BlockSpec((bsz, d), lambda j: (0, 0)),
            pl.BlockSpec((d, tn), lambda j: (0, j)),
            pl.BlockSpec((1, tn), lambda j: (0, j)),
        ],
        out_specs=pl.BlockSpec((bsz, tn), lambda j: (0, j)),
        out_shape=jax.ShapeDtypeStruct((bsz, n), F32),
        compiler_params=pltpu.CompilerParams(
            dimension_semantics=("arbitrary",), vmem_limit_bytes=VMEM_LIMIT_BYTES),
        name="adaln_mod",
    )(c, w_ada, b_ada.reshape(1, n))


PROJ_CHUNK = 2 * LANES


def _proj_kernel(x_ref, mod_ref, g_ref, w_ref, wt_ref, gain_ref, gk_ref,
                 p_ref, kk_ref, wi_ref, h_ref):
    j = pl.program_id(1)

    @pl.when(j == 0)
    def _():
        x = x_ref[...]
        ms = jnp.mean(x * x, axis=-1, keepdims=True)
        shift = mod_ref[0, 0:1, :]
        gs = g_ref[...] * (1.0 + mod_ref[0, 1:2, :])
        hb = (x * lax.rsqrt(ms + RMS_EPS) * gs + shift).astype(BF16)
        h_ref[...] = hb
        t = _dot_nt(hb, wt_ref[...])
        tk = t[:, :LANES]
        kms = jnp.mean(tk * tk, axis=-1, keepdims=True)
        kk_ref[...] = (tk * lax.rsqrt(kms + RMS_EPS) * gk_ref[...]).astype(BF16)
        wi_ref[...] = t[:, LANES:]

    def segment(with_norm):
        g = gain_ref[0]
        h = h_ref[...]
        for cc in range(D_GROUP // PROJ_CHUNK):
            acc = _dot_nt(h, w_ref[cc * PROJ_CHUNK:(cc + 1) * PROJ_CHUNK, :])
            for c in range(PROJ_CHUNK // LANES):
                a = acc[:, c * LANES:(c + 1) * LANES]
                if with_norm:
                    ms = jnp.mean(a * a, axis=-1, keepdims=True)
                    a = a * lax.rsqrt(ms + RMS_EPS) * g
                p_ref[cc * (PROJ_CHUNK // LANES) + c] = a.astype(BF16)

    seg = _segment_of_step(pl.program_id(0), j)
    is_norm = (seg == SEG_QA) | (seg == SEG_KA) | (seg == SEG_QB) | (seg == SEG_KB)
    pl.when(is_norm)(functools.partial(segment, True))
    pl.when(jnp.logical_not(is_norm))(functools.partial(segment, False))


def _segment_of_step(i, j):
    return jnp.where(i % 2 == 0, j, N_SEGS - 1 - j)


def _projection(x2, mod3, g_norm, w_main, w_tail, gains, gk, seq_len):
    m, d = x2.shape
    tm = 1024
    tiles_per_seq = seq_len // tm
    return pl.pallas_call(
        _proj_kernel,
        grid=(m // tm, N_SEGS),
        in_specs=[
            pl.BlockSpec((tm, d), lambda i, j: (i, 0)),
            pl.BlockSpec((1, 3, d), lambda i, j: (i // tiles_per_seq, 0, 0)),
            pl.BlockSpec((1, d), lambda i, j: (0, 0)),
            pl.BlockSpec((D_GROUP, d), lambda i, j: (_segment_of_step(i, j), 0)),
            pl.BlockSpec((2 * LANES, d), lambda i, j: (0, 0)),
            pl.BlockSpec((1, 1, LANES), lambda i, j: (_segment_of_step(i, j), 0, 0)),
            pl.BlockSpec((1, LANES), lambda i, j: (0, 0)),
        ],
        out_specs=[
            pl.BlockSpec((BLOCKS_PER_SEG, tm, LANES), lambda i, j: (_segment_of_step(i, j), i, 0)),
            pl.BlockSpec((tm, LANES), lambda i, j: (i, 0)),
            pl.BlockSpec((tm, LANES), lambda i, j: (i, 0)),
        ],
        out_shape=[
            jax.ShapeDtypeStruct((N_SEGS * BLOCKS_PER_SEG, m, LANES), BF16),
            jax.ShapeDtypeStruct((m, LANES), BF16),
            jax.ShapeDtypeStruct((m, LANES), F32),
        ],
        scratch_shapes=[pltpu.VMEM((tm, d), BF16)],
        compiler_params=pltpu.CompilerParams(
            dimension_semantics=("arbitrary", "arbitrary"), vmem_limit_bytes=VMEM_LIMIT_BYTES),
        name="in_proj",
    )(x2, mod3, g_norm, w_main, w_tail, gains, gk)


POS_TILE = MOBA_BLOCK
AUG_KLOC, AUG_KBLK, AUG_ONE_LOC, AUG_ONE_BLK, AUG_ONEHOT0 = 0, 1, 2, 3, 8


def _key_features(seq_len):
    pos = np.arange(seq_len)
    f = np.zeros((seq_len, LANES), np.float32)
    f[:, AUG_KLOC] = pos % POS_TILE
    f[:, AUG_KBLK] = pos // POS_TILE
    f[:, AUG_ONE_LOC] = 1.0
    f[:, AUG_ONE_BLK] = 1.0
    f[pos, AUG_ONEHOT0 + pos // POS_TILE] = 1.0
    return jnp.asarray(f, BF16)


def _alibi_query_features(slope, q_tile, tq):
    lane = lax.broadcasted_iota(jnp.int32, (tq, LANES), 1)
    row = lax.broadcasted_iota(jnp.int32, (tq, LANES), 0).astype(F32)
    f = jnp.where(lane == AUG_KLOC, slope, 0.0)
    f = jnp.where(lane == AUG_KBLK, slope * POS_TILE, f)
    f = jnp.where(lane == AUG_ONE_LOC, -slope * row, f)
    return jnp.where(lane == AUG_ONE_BLK, (-slope * POS_TILE) * q_tile.astype(F32), f)


FOLD_CHAINS = 4


def _fold_rows(x, op):
    r = x.reshape(x.shape[0] // 8, 8, x.shape[1])
    n_acc = min(FOLD_CHAINS, r.shape[0])
    accs = [r[k] for k in range(n_acc)]
    for g in range(n_acc, r.shape[0]):
        accs[g % n_acc] = op(accs[g % n_acc], r[g])
    while len(accs) > 1:
        accs = [op(accs[k], accs[k + 1]) for k in range(0, len(accs) - 1, 2)] + accs[len(accs) - len(accs) % 2:]
    return accs[0]


TILE_CHUNKS = (4, 2, 1)


def _for_tile_chunks(n_tiles, fn, chunks=TILE_CHUNKS):
    big = chunks[0]
    shift = big.bit_length() - 1

    def many(c, carry):
        fn(c * big, big)
        return carry

    n_big = lax.shift_right_logical(n_tiles, shift)
    lax.fori_loop(0, n_big, many, 0)
    done = n_big * big
    for size in chunks[1:]:
        has = (n_tiles & size) != 0

        @pl.when(has)
        def _(done=done, size=size):
            fn(done, size)

        done = done + (n_tiles & size)


def _softmax_pass_a(n_tiles, tile, score_rows, s_ref, mx_ref):
    tq = s_ref.shape[-1]

    def pass_a(first, count):
        for h in range(N_HEADS):
            s = score_rows(h, first, count) * LOG2E
            s_ref[h, pl.ds(first, count)] = s.reshape(count, tile, tq)
            mx_ref[h] = jnp.maximum(mx_ref[h], _fold_rows(s, jnp.maximum))

    _for_tile_chunks(n_tiles, pass_a)
    return [jnp.max(mx_ref[h], axis=0, keepdims=True) for h in range(N_HEADS)]


def _softmax_pass_b(n_tiles, tile, m, value_cols, s_ref, ls_ref, acc_ref):
    tq = acc_ref.shape[-1]
    ls_ref[...] = jnp.zeros(ls_ref.shape, F32)
    acc_ref[...] = jnp.zeros(acc_ref.shape, F32)

    def pass_b(first, count):
        for h in range(N_HEADS):
            p = jnp.exp2(s_ref[h, pl.ds(first, count)].reshape(count * tile, tq) - m[h])
            ls_ref[h] += _fold_rows(p, jnp.add)
            vt = jnp.concatenate([value_cols(h, first + u) for u in range(count)], axis=1)
            acc_ref[h] += jnp.dot(vt, p.astype(BF16), preferred_element_type=F32)

    _for_tile_chunks(n_tiles, pass_b)
    return [jnp.sum(ls_ref[h], axis=0, keepdims=True) for h in range(N_HEADS)]


def _gated_output(h, z_ref, l, acc_ref, o_ref):
    z = z_ref[h].astype(F32)
    o = (acc_ref[h] / l).T * (z * jax.nn.sigmoid(z))
    o_ref[:, h * HEAD_DIM:(h + 1) * HEAD_DIM] = o.astype(BF16)


def _transpose_values(v_ref, vt_ref, n_tiles, tile):
    for h in range(N_HEADS):
        for j in range(n_tiles):
            vt_ref[h, j] = v_ref[h, j * tile:(j + 1) * tile, :].astype(F32).T.astype(BF16)


DSA_TQ = 256
DSA_TK = POS_TILE
SEARCH_STEPS = 4
SEARCH_SLACK_STEPS = 9


def _dsa_kernel(q_ref, k_ref, v_ref, z_ref, qi_ref, kk_ref, wi_ref, kaug_ref, o_ref,
                vt_ref, qs_ref, sc_ref, ext_ref, t_ref, j_ref, qcat_ref, s_ref, mx_ref, ls_ref, acc_ref,
                *, top_k, seq_len):
    tq, tk = DSA_TQ, DSA_TK
    i = pl.program_id(1)
    t0 = i * tq
    nk = i + 1
    idx_scale = (IDX_DIM ** -0.5) * (IDX_HEADS ** -0.5)

    @pl.when(i == 0)
    def _():
        _transpose_values(v_ref, vt_ref, seq_len // tk, tk)

    lane = lax.broadcasted_iota(jnp.int32, (tq, LANES), 1)
    key_loc = lax.broadcasted_iota(jnp.int32, (tk, tq), 0).astype(F32)
    qry_loc = lax.broadcasted_iota(jnp.int32, (tk, tq), 1).astype(F32)

    for c in range(BLOCKS_PER_SEG):
        qp = qi_ref[c].astype(F32)
        qs_ref[2 * c] = jnp.where(lane < IDX_DIM, qp, 0.0).astype(BF16)
        qs_ref[2 * c + 1] = jnp.where(lane >= IDX_DIM, qp, 0.0).astype(BF16)
    w_t = (wi_ref[...] * idx_scale).T
    w_rows = [w_t[h:h + 1, :] for h in range(IDX_HEADS)]

    def score_tiles(first, count):
        for u in range(count):
            kt = first + u
            kk = kk_ref[pl.ds(pl.multiple_of(kt * tk, tk), tk), :]
            acc = jnp.zeros((tk, tq), F32)
            for h in range(IDX_HEADS):
                acc = acc + w_rows[h] * jnp.maximum(_dot_nt(kk, qs_ref[h]), 0.0)
            off = (kt * tk - t0).astype(F32)
            causal = (key_loc + off) <= qry_loc
            sc_ref[kt] = jnp.where(causal, acc, -jnp.inf)
            ext_ref[0] = jnp.minimum(ext_ref[0], _fold_rows(jnp.where(causal, acc, BIG), jnp.minimum))
            ext_ref[1] = jnp.maximum(ext_ref[1], _fold_rows(jnp.where(causal, acc, NEG_INF), jnp.maximum))

    ext_ref[0] = jnp.full((8, tq), BIG, F32)
    ext_ref[1] = jnp.full((8, tq), NEG_INF, F32)
    _for_tile_chunks(nk, score_tiles)

    t_ref[...] = jnp.full((1, tq), NEG_INF, F32)
    j_ref[...] = jnp.full((1, tq), -1.0, F32)

    def count(pred):
        def body(kt, acc):
            off = (kt * tk).astype(F32)
            return acc + _fold_rows(jnp.where(pred(sc_ref[kt], key_loc + off), 1.0, 0.0), jnp.add)
        return jnp.sum(lax.fori_loop(0, nk, body, jnp.zeros((8, tq), F32)), axis=0, keepdims=True)

    def write_mask(with_ties):
        def mask_tile(kt, carry):
            s = sc_ref[kt]
            thr = t_ref[...]
            sel = s > thr
            if with_ties:
                off = (kt * tk).astype(F32)
                sel = jnp.logical_or(sel, jnp.logical_and(s == thr, (key_loc + off) <= j_ref[...]))
            sc_ref[kt] = jnp.where(sel, 0.0, NEG_INF)
            return carry

        lax.fori_loop(0, nk, mask_tile, 0)

    @pl.when(t0 + tq > top_k)
    def _():
        kf = float(top_k)
        qpos = lax.broadcasted_iota(jnp.int32, (1, tq), 1) + t0
        done0 = jnp.where(qpos + 1 <= top_k, 1.0, 0.0)

        lo0 = jnp.min(ext_ref[0], axis=0, keepdims=True)
        hi0 = jnp.max(ext_ref[1], axis=0, keepdims=True)

        def step(x, degen, state):
            lo, hi, thr, done, tie = state
            c = count(lambda s, pos: s > x)
            active = done == 0.0
            live = jnp.logical_and(active, jnp.logical_not(degen))
            found = jnp.logical_and(live, c == kf)
            new_tie = jnp.logical_and(active, degen)
            thr = jnp.where(found, x, jnp.where(new_tie, hi, thr))
            tie = jnp.where(new_tie, 1.0, tie)
            done = jnp.where(jnp.logical_or(found, new_tie), 1.0, done)
            lo = jnp.where(jnp.logical_and(live, c > kf), x, lo)
            hi = jnp.where(jnp.logical_and(live, c < kf), x, hi)
            return lo, hi, thr, done, tie

        state0 = (lo0, hi0, jnp.full((1, tq), NEG_INF, F32), done0, jnp.zeros((1, tq), F32))

        def bisect(state):
            for _ in range(SEARCH_STEPS):
                lo, hi = state[0], state[1]
                mid = 0.5 * lo + 0.5 * hi
                degen = jnp.logical_or(mid <= lo, mid >= hi)
                state = step(mid, degen, state)
            return state

        unchecked_rounds = (int(np.log2(seq_len)) + SEARCH_SLACK_STEPS) // SEARCH_STEPS
        state1 = lax.fori_loop(0, unchecked_rounds, lambda _, st: bisect(st), state0)

        def flags(state):
            return jnp.max(1.0 - state[3]), jnp.max(state[4])

        def cond(carry):
            return carry[1] > 0.0

        def body(carry):
            state = bisect(carry[0])
            return (state,) + flags(state)

        (lo, hi, thr_loop, done, tie), _, any_tie = lax.while_loop(cond, body, (state1,) + flags(state1))
        t_ref[...] = thr_loop

        @pl.when(any_tie > 0.0)
        def _():
            reach_hi = count(lambda s, pos: s >= hi)
            thr = jnp.where(jnp.logical_and(tie > 0.0, reach_hi < kf), lo, thr_loop)
            t_ref[...] = thr
            need = kf - count(lambda s, pos: s > thr)
            n_bits = int(np.ceil(np.log2(seq_len + 1)))

            def jstep(_, carry):
                jlo, jhi = carry
                jmid = jnp.floor((jlo + jhi) * 0.5)
                c = count(lambda s, pos: jnp.logical_and(s == thr, pos <= jmid))
                ge = c >= need
                return jnp.where(ge, jlo, jmid), jnp.where(ge, jmid, jhi)

            _, jhi = lax.fori_loop(0, n_bits, jstep,
                                   (jnp.full((1, tq), -1.0, F32),
                                    jnp.full((1, tq), float(seq_len - 1), F32)))
            j_ref[...] = jnp.where(tie > 0.0, jhi, -1.0)
            write_mask(True)

        pl.when(any_tie <= 0.0)(functools.partial(write_mask, False))

    pl.when(t0 + tq <= top_k)(functools.partial(write_mask, False))

    for h in range(N_HEADS):
        feat = _alibi_query_features(_alibi_slope(h, N_HEADS), i, tq)
        qcat_ref[h] = jnp.concatenate([q_ref[h], feat.astype(BF16)], axis=1)

    def score_rows(h, first, count):
        ks = pl.ds(pl.multiple_of(first * tk, tk), count * tk)
        kcat = jnp.concatenate([k_ref[h, ks, :], kaug_ref[ks, :]], axis=1)
        return _dot_nt(kcat, qcat_ref[h]) + sc_ref[pl.ds(first, count)].reshape(count * tk, tq)

    mx_ref[...] = jnp.full(mx_ref.shape, NEG_INF, F32)
    value_cols = lambda h, kt: vt_ref[h, kt]
    m = _softmax_pass_a(nk, tk, score_rows, s_ref, mx_ref)
    lsum = _softmax_pass_b(nk, tk, m, value_cols, s_ref, ls_ref, acc_ref)
    for h in range(N_HEADS):
        _gated_output(h, z_ref, lsum[h], acc_ref, o_ref)


def _dsa_attention(p, kk, wi, kaug, bsz, seq_len):
    tq, tk = DSA_TQ, DSA_TK
    nq = seq_len // tq
    nkt = seq_len // tk
    m = bsz * seq_len
    top_k = min(DSA_TOPK_MAX, seq_len // 4)
    seg = lambda s: (lambda b, i: (s, b * nq + i, 0))
    seg_full = lambda s: (lambda b, i: (s, b, 0))
    kern = functools.partial(_dsa_kernel, top_k=top_k, seq_len=seq_len)
    return pl.pallas_call(
        kern,
        grid=(bsz, nq),
        in_specs=[
            pl.BlockSpec((BLOCKS_PER_SEG, tq, LANES), seg(SEG_QA)),
            pl.BlockSpec((BLOCKS_PER_SEG, seq_len, LANES), seg_full(SEG_KA)),
            pl.BlockSpec((BLOCKS_PER_SEG, seq_len, LANES), seg_full(SEG_VA)),
            pl.BlockSpec((BLOCKS_PER_SEG, tq, LANES), seg(SEG_ZA)),
            pl.BlockSpec((BLOCKS_PER_SEG, tq, LANES), seg(SEG_QIDX)),
            pl.BlockSpec((seq_len, LANES), lambda b, i: (b, 0)),
            pl.BlockSpec((tq, LANES), lambda b, i: (b * nq + i, 0)),
            pl.BlockSpec((seq_len, LANES), lambda b, i: (0, 0)),
        ],
        out_specs=pl.BlockSpec((tq, D_GROUP), lambda b, i: (b * nq + i, 0)),
        out_shape=jax.ShapeDtypeStruct((m, D_GROUP), BF16),
        scratch_shapes=[
            pltpu.VMEM((N_HEADS, nkt, HEAD_DIM, tk), BF16),
            pltpu.VMEM((IDX_HEADS, tq, LANES), BF16),
            pltpu.VMEM((nkt, tk, tq), F32),
            pltpu.VMEM((2, 8, tq), F32),
            pltpu.VMEM((1, tq), F32),
            pltpu.VMEM((1, tq), F32),
            pltpu.VMEM((N_HEADS, tq, 2 * LANES), BF16),
            pltpu.VMEM((N_HEADS, nkt, tk, tq), F32),
            pltpu.VMEM((N_HEADS, 8, tq), F32),
            pltpu.VMEM((N_HEADS, 8, tq), F32),
            pltpu.VMEM((N_HEADS, HEAD_DIM, tq), F32),
        ],
        compiler_params=pltpu.CompilerParams(
            dimension_semantics=("arbitrary", "arbitrary"), vmem_limit_bytes=VMEM_LIMIT_BYTES),
        name="dsa_attn",
    )(p, p, p, p, p, kk, wi, kaug)


def _moba_kernel(q_ref, k_ref, v_ref, z_ref, kaug_ref, o_ref,
                 kmean_ref, vt_ref, selb_ref, qcat_ref, s_ref, mx_ref, ls_ref, acc_ref, *, n_blocks):
    bs = MOBA_BLOCK
    n = pl.program_id(1)
    top_k = min(MOBA_TOPK_MAX, n_blocks - 1)

    @pl.when(n == 0)
    def _():
        kmean_ref[...] = jnp.zeros(kmean_ref.shape, F32)
        for h in range(N_HEADS):
            for j in range(n_blocks):
                kb = k_ref[h, j * bs:(j + 1) * bs, :].astype(F32)
                kmean_ref[h, j:j + 1, :] = jnp.mean(kb, axis=0, keepdims=True)
        _transpose_values(v_ref, vt_ref, n_blocks, bs)

    nb_rows = -(-n_blocks // 8) * 8
    blk = lax.broadcasted_iota(jnp.int32, (nb_rows, bs), 0)
    past = blk < n

    selb_ref[...] = jnp.zeros(selb_ref.shape, F32)
    for h in range(N_HEADS):
        q = q_ref[h]
        g = _dot_nt(kmean_ref[h].astype(BF16), q)[:nb_rows]
        for j in range(n_blocks):
            gj = g[j:j + 1, :]
            beats = jnp.logical_or(g > gj, jnp.logical_and(g == gj, blk < j))
            rank = jnp.sum(jnp.where(jnp.logical_and(beats, past), 1.0, 0.0), axis=0, keepdims=True)
            dropped = jnp.logical_and(rank >= float(top_k), j < n)
            selb_ref[AUG_ONEHOT0 + j:AUG_ONEHOT0 + j + 1, :] = jnp.where(dropped, NEG_INF, 0.0)
        feat = _alibi_query_features(_alibi_slope(h, N_HEADS), n, bs) + selb_ref[...].T
        qcat_ref[h] = jnp.concatenate([q, feat.astype(BF16)], axis=1)

    key_loc = lax.broadcasted_iota(jnp.int32, (bs, bs), 0)
    qry_loc = lax.broadcasted_iota(jnp.int32, (bs, bs), 1)
    causal = key_loc <= qry_loc

    def score_rows(h, first, count):
        ks = pl.ds(pl.multiple_of(first * bs, bs), count * bs)
        kcat = jnp.concatenate([k_ref[h, ks, :], kaug_ref[ks, :]], axis=1)
        return _dot_nt(kcat, qcat_ref[h])

    for h in range(N_HEADS):
        s = jnp.where(causal, score_rows(h, n, 1), NEG_INF) * LOG2E
        s_ref[h, n] = s
        mx_ref[h] = _fold_rows(s, jnp.maximum)
    value_cols = lambda h, j: vt_ref[h, j]
    m = _softmax_pass_a(n, bs, score_rows, s_ref, mx_ref)
    lsum = _softmax_pass_b(n + 1, bs, m, value_cols, s_ref, ls_ref, acc_ref)
    for h in range(N_HEADS):
        _gated_output(h, z_ref, lsum[h], acc_ref, o_ref)


def _moba_attention(p, kaug, bsz, seq_len):
    bs = MOBA_BLOCK
    nb = seq_len // bs
    m = bsz * seq_len
    seg = lambda s: (lambda b, i: (s, b * nb + i, 0))
    seg_full = lambda s: (lambda b, i: (s, b, 0))
    kern = functools.partial(_moba_kernel, n_blocks=nb)
    nb_pad = max(16, nb)
    assert AUG_ONEHOT0 + nb <= LANES
    return pl.pallas_call(
        kern,
        grid=(bsz, nb),
        in_specs=[
            pl.BlockSpec((BLOCKS_PER_SEG, bs, LANES), seg(SEG_QB)),
            pl.BlockSpec((BLOCKS_PER_SEG, seq_len, LANES), seg_full(SEG_KB)),
            pl.BlockSpec((BLOCKS_PER_SEG, seq_len, LANES), seg_full(SEG_VB)),
            pl.BlockSpec((BLOCKS_PER_SEG, bs, LANES), seg(SEG_ZB)),
            pl.BlockSpec((seq_len, LANES), lambda b, i: (0, 0)),
        ],
        out_specs=pl.BlockSpec((bs, D_GROUP), lambda b, i: (b * nb + i, 0)),
        out_shape=jax.ShapeDtypeStruct((m, D_GROUP), BF16),
        scratch_shapes=[
            pltpu.VMEM((N_HEADS, nb_pad, HEAD_DIM), F32),
            pltpu.VMEM((N_HEADS, nb, HEAD_DIM, bs), BF16),
            pltpu.VMEM((LANES, bs), F32),
            pltpu.VMEM((N_HEADS, bs, 2 * LANES), BF16),
            pltpu.VMEM((N_HEADS, nb, bs, bs), F32),
            pltpu.VMEM((N_HEADS, 8, bs), F32),
            pltpu.VMEM((N_HEADS, 8, bs), F32),
            pltpu.VMEM((N_HEADS, HEAD_DIM, bs), F32),
        ],
        compiler_params=pltpu.CompilerParams(
            dimension_semantics=("arbitrary", "arbitrary"), vmem_limit_bytes=VMEM_LIMIT_BYTES),
        name="moba_attn",
    )(p, p, p, p, kaug)


OUT_CHUNK = 512


def _out_kernel(ya_ref, yb_ref, wa_ref, wb_ref, x_ref, mod_ref, o_ref):
    ya = ya_ref[...]
    yb = yb_ref[...]
    for c in range(o_ref.shape[1] // OUT_CHUNK):
        cs = slice(c * OUT_CHUNK, (c + 1) * OUT_CHUNK)
        y = jnp.dot(ya, wa_ref[:, cs], preferred_element_type=F32)
        y = y + jnp.dot(yb, wb_ref[:, cs], preferred_element_type=F32)
        o_ref[:, cs] = x_ref[:, cs] + mod_ref[0, 2:3, cs] * y


def _out_projection(ya, yb, w_out_bf, x2, mod3, seq_len):
    m, d = x2.shape
    tm = 1024
    tiles_per_seq = seq_len // tm
    return pl.pallas_call(
        _out_kernel,
        grid=(m // tm,),
        in_specs=[
            pl.BlockSpec((tm, D_GROUP), lambda i: (i, 0)),
            pl.BlockSpec((tm, D_GROUP), lambda i: (i, 0)),
            pl.BlockSpec((D_GROUP, d), lambda i: (0, 0), pipeline_mode=pl.Buffered(1)),
            pl.BlockSpec((D_GROUP, d), lambda i: (1, 0), pipeline_mode=pl.Buffered(1)),
            pl.BlockSpec((tm, d), lambda i: (i, 0)),
            pl.BlockSpec((1, 3, d), lambda i: (i // tiles_per_seq, 0, 0)),
        ],
        out_specs=pl.BlockSpec((tm, d), lambda i: (i, 0)),
        out_shape=jax.ShapeDtypeStruct((m, d), F32),
        compiler_params=pltpu.CompilerParams(
            dimension_semantics=("arbitrary",), vmem_limit_bytes=VMEM_LIMIT_BYTES),
        name="out_proj",
    )(ya, yb, w_out_bf, w_out_bf, x2, mod3)


RELAYOUT_ROWS = 512
RELAYOUT_HEAD = 128


def _relayout_kernel(a_ref, b_ref, mid_ref, main_ref, tail_ref, *, n_a_tiles, n_mid):
    r = pl.program_id(0)

    @pl.when(r < n_a_tiles)
    def _():
        main_ref[...] = a_ref[...].astype(BF16)

    @pl.when(r >= n_a_tiles)
    def _():
        main_ref[:RELAYOUT_ROWS - n_mid, :] = a_ref[n_mid:, :].astype(BF16)
        main_ref[RELAYOUT_ROWS - n_mid:, :] = b_ref[:n_mid, :].astype(BF16)

    kidx = mid_ref[:IDX_DIM, :].astype(BF16)
    tail_ref[0:IDX_DIM, :] = kidx
    tail_ref[IDX_DIM:2 * IDX_DIM, :] = kidx
    tail_ref[2 * IDX_DIM:2 * IDX_DIM + IDX_HEADS, :] = mid_ref[IDX_DIM:IDX_DIM + IDX_HEADS, :].astype(BF16)
    tail_ref[2 * IDX_DIM + IDX_HEADS:, :] = jnp.zeros(
        (tail_ref.shape[0] - 2 * IDX_DIM - IDX_HEADS, tail_ref.shape[1]), BF16)


def _relayout_weights(w_t):
    n_in, d = w_t.shape
    n_a = 4 * D_GROUP + IDX_HEADS * IDX_DIM
    n_mid = IDX_DIM + IDX_HEADS
    tr = RELAYOUT_ROWS
    assert n_in == n_a + n_mid + 4 * D_GROUP and n_a % tr == 0 and (4 * D_GROUP) % tr == 0
    assert n_mid % 16 == 0 and n_mid <= RELAYOUT_HEAD and tr % RELAYOUT_HEAD == 0 and n_a % n_mid == 0
    n_main = n_in - n_mid
    n_a_tiles = n_a // tr
    return pl.pallas_call(
        functools.partial(_relayout_kernel, n_a_tiles=n_a_tiles, n_mid=n_mid),
        grid=(n_main // tr,),
        in_specs=[
            pl.BlockSpec((tr, d), lambda r: (r, 0)),
            pl.BlockSpec((RELAYOUT_HEAD, d),
                         lambda r: ((jnp.maximum(r, n_a_tiles - 1) + 1) * (tr // RELAYOUT_HEAD), 0)),
            pl.BlockSpec((n_mid, d), lambda r: (n_a // n_mid, 0)),
        ],
        out_specs=[
            pl.BlockSpec((tr, d), lambda r: (r, 0)),
            pl.BlockSpec((2 * LANES, d), lambda r: (0, 0)),
        ],
        out_shape=[
            jax.ShapeDtypeStruct((n_main, d), BF16),
            jax.ShapeDtypeStruct((2 * LANES, d), BF16),
        ],
        compiler_params=pltpu.CompilerParams(
            dimension_semantics=("arbitrary",), vmem_limit_bytes=VMEM_LIMIT_BYTES),
        name="w_relayout",
    )(w_t, w_t, w_t)


def _layer(x, c, w_ada, b_ada, g_norm, w_in, q_norm_a, k_norm_a, k_norm_idx, q_norm_b, k_norm_b, w_out):
    bsz, seq_len, d = x.shape
    assert seq_len % MOBA_BLOCK == 0 and seq_len % 1024 == 0 and d % 1024 == 0
    x2 = x.reshape(bsz * seq_len, d)

    mod3 = _modulation(c, w_ada, b_ada).reshape(bsz, 3, d)

    w_main, w_tail = _relayout_weights(jnp.swapaxes(w_in, 0, 1))
    ones = jnp.ones((HEAD_DIM,), F32)
    sm_scale = HEAD_DIM ** -0.5
    gains = jnp.stack([q_norm_a * sm_scale, k_norm_a, ones, ones, ones,
                       q_norm_b * sm_scale, k_norm_b, ones, ones]).reshape(N_SEGS, 1, HEAD_DIM)
    for h in range(N_HEADS):
        slope = _alibi_slope(h, N_HEADS)
        assert float(np.float32(slope).astype(jnp.bfloat16)) == slope, "ALiBi slopes must be exact in bf16"
    assert seq_len // POS_TILE <= 256 and POS_TILE <= 256, "positions must split into bf16-exact parts"
    kaug = _key_features(seq_len)
    gk = jnp.concatenate([k_norm_idx, k_norm_idx]).reshape(1, LANES)

    p, kk, wi = _projection(x2, mod3, g_norm.reshape(1, d), w_main, w_tail, gains, gk, seq_len)
    ya = _dsa_attention(p, kk, wi, kaug, bsz, seq_len)
    yb = _moba_attention(p, kaug, bsz, seq_len)
    out = _out_projection(ya, yb, w_out.astype(BF16), x2, mod3, seq_len)
    return out.reshape(bsz, seq_len, d)


def kernel(x, c, w_ada, b_ada, g_norm, w_in, q_norm_a, k_norm_a, k_norm_idx, q_norm_b, k_norm_b, w_out):
    for i in range(w_ada.shape[0]):
        x = _layer(x, c, w_ada[i], b_ada[i], g_norm[i], w_in[i], q_norm_a[i], k_norm_a[i],
                   k_norm_idx[i], q_norm_b[i], k_norm_b[i], w_out[i])
    return x
```
